```python
import math
import jax, jax.numpy as jnp
from jax import lax
import numpy as np

D_MODEL = 1024
BATCH = 2
SEQ = 8192
DEPTH = 4

N_MEM = 256
SSM_GROUP_CH = 16
SSM_STATE = 64
SSM_WIDTH = D_MODEL // 2
SSM_GROUPS = SSM_WIDTH // SSM_GROUP_CH
GMLP_WIDTH = D_MODEL // 2
GMLP_GROUPS = 8
GMLP_GROUP_CH = GMLP_WIDTH // GMLP_GROUPS
CHUNK = 128
MEM_HEADS = 4
MEM_WIDTH = D_MODEL // 2
MEM_HEAD_DIM = MEM_WIDTH // MEM_HEADS
N_BRANCH = 3
IN_WIDTH = SSM_WIDTH + 2 * GMLP_WIDTH + MEM_WIDTH
BRANCH_WIDTH = 512
N_GROUPS_MOE = 4
EXP_PER_GROUP = 8
N_EXPERTS = N_GROUPS_MOE * EXP_PER_GROUP
D_EXPERT = 256
TOP_K_INNER = 2
MOE_BLOCK = 128
DEEPNORM_ALPHA = (2.0 * DEPTH) ** 0.25
DEEPNORM_BETA = (8.0 * DEPTH) ** -0.25
LN_EPS = 1e-5

kernel_name = "hybrid_s5_gmlp_memattn_hmoe_deepnorm"


def _layer_norm(x, g, b):
    xf = x.astype(jnp.float32)
    mu = jnp.mean(xf, axis=-1, keepdims=True)
    var = jnp.mean(jnp.square(xf - mu), axis=-1, keepdims=True)
    y = (xf - mu) * lax.rsqrt(var + LN_EPS)
    return (y * g.astype(jnp.float32) + b.astype(jnp.float32)).astype(x.dtype)


def _diag_scan(lam_bar, bu, reverse):
    a = jnp.broadcast_to(lam_bar, bu.shape)

    def combine(left, right):
        a_l, b_l = left
        a_r, b_r = right
        return a_l * a_r, a_r * b_l + b_r

    _, h = lax.associative_scan(combine, (a, bu), axis=1, reverse=reverse)
    return h


def _s5_branch(u, lam_re, lam_im, log_step, b_re, b_im, c_re, c_im, d_skip, w_glu, b_glu):
    dtype = u.dtype
    bsz, seq, _ = u.shape
    f32 = jnp.float32
    uf = u.astype(f32).reshape(bsz, seq, SSM_GROUPS, SSM_GROUP_CH)
    uc = uf.astype(jnp.complex64)
    b_mat = lax.complex(b_re.astype(f32), b_im.astype(f32))
    c_mat = lax.complex(c_re.astype(f32), c_im.astype(f32))
    y = d_skip.astype(f32).reshape(SSM_GROUPS, SSM_GROUP_CH) * uf
    for direction in range(2):
        lam = lax.complex(lam_re[direction].astype(f32), lam_im[direction].astype(f32))
        step = jnp.exp(log_step[direction].astype(f32))[:, None]
        lam_bar = jnp.exp(lam * step)
        b_bar = ((lam_bar - 1.0) / lam)[:, :, None] * b_mat
        bu = jnp.einsum('gph,bsgh->bsgp', b_bar, uc)
        h = _diag_scan(lam_bar, bu, reverse=(direction == 1))
        y = y + jnp.einsum('ghp,bsgp->bsgh', c_mat, h).real
    y = jax.nn.gelu(y.reshape(bsz, seq, SSM_WIDTH))
    gate = jax.nn.sigmoid(y @ w_glu.astype(f32) + b_glu.astype(f32))
    return (y * gate).astype(dtype)


def _gmlp_branch(uv, ln_g, ln_b, w_s, b_s):
    uv = jax.nn.gelu(uv)
    u, v = jnp.split(uv, 2, axis=-1)
    v = _layer_norm(v, ln_g, ln_b)
    bsz, seq, _ = v.shape
    vc = v.reshape(bsz, seq // CHUNK, CHUNK, GMLP_GROUPS, GMLP_GROUP_CH)
    s = jnp.einsum('gij,bcjgd->bcigd', w_s, vc) + b_s.T[None, None, :, :, None]
    return u * s.reshape(bsz, seq, GMLP_WIDTH)


def _memory_attention(q, mem, w_kv):
    bsz, seq, _ = q.shape
    k, v = jnp.split(mem @ w_kv, 2, axis=-1)
    qh = q.reshape(bsz, seq, MEM_HEADS, MEM_HEAD_DIM)
    kh = k.reshape(bsz, N_MEM, MEM_HEADS, MEM_HEAD_DIM)
    vh = v.reshape(bsz, N_MEM, MEM_HEADS, MEM_HEAD_DIM)
    scores = jnp.einsum('bshd,bmhd->bhsm', qh, kh).astype(jnp.float32) * (MEM_HEAD_DIM ** -0.5)
    p = jax.nn.softmax(scores, axis=-1).astype(q.dtype)
    o = jnp.einsum('bhsm,bmhd->bshd', p, vh)
    return o.reshape(bsz, seq, MEM_WIDTH)


def _hier_moe(x, w_rg, b_rg, w_re, b_re, w_gate_e, w_up_e, w_down_e):
    bsz, seq, d = x.shape
    xt = x.reshape(-1, MOE_BLOCK, d)

    def block(xb):
        lg = (xb @ w_rg + b_rg).astype(jnp.float32)
        pg = jax.nn.softmax(lg, axis=-1)
        g_star = jnp.argmax(lg, axis=-1)
        g_oh = jax.nn.one_hot(g_star, N_GROUPS_MOE, dtype=jnp.float32)
        pg_sel = jnp.sum(pg * g_oh, axis=-1)
        le = (jnp.einsum('nd,gde->nge', xb, w_re) + b_re).astype(jnp.float32)
        le_sel = jnp.einsum('ng,nge->ne', g_oh, le)
        top_v, top_i = lax.top_k(le_sel, TOP_K_INNER)
        pe = jax.nn.softmax(top_v, axis=-1)
        expert_id = g_star[:, None] * EXP_PER_GROUP + top_i
        e_oh = jax.nn.one_hot(expert_id, N_EXPERTS, dtype=jnp.float32)
        weights = jnp.einsum('nk,nke->ne', pg_sel[:, None] * pe, e_oh).astype(xb.dtype)
        h = jax.nn.silu(jnp.einsum('nd,edf->nef', xb, w_gate_e)) * jnp.einsum('nd,edf->nef', xb, w_up_e)
        return jnp.einsum('nef,efd->nd', h * weights[:, :, None], w_down_e)

    return lax.map(block, xt).reshape(bsz, seq, d)


def setup_inputs(seed: int = 0) -> dict:
    key = jax.random.key(seed)
    ks = iter(jax.random.split(key, 40))
    L, D = DEPTH, D_MODEL
    f32 = jnp.float32

    def nrm(shape, scale):
        return jax.random.normal(next(ks), shape, f32) * scale

    n_idx = jnp.arange(SSM_STATE, dtype=f32)
    lam_re = -0.5 + nrm((L, 2, SSM_GROUPS, SSM_STATE), 0.01)
    lam_im = math.pi * n_idx + nrm((L, 2, SSM_GROUPS, SSM_STATE), 0.01)
    log_step = jax.random.uniform(next(ks), (L, 2, SSM_GROUPS), f32,
                                  minval=math.log(1e-3), maxval=math.log(1e-1))
    w_kv_k = nrm((L, D, MEM_WIDTH), D ** -0.5)
    w_kv_v = nrm((L, D, MEM_WIDTH), D ** -0.5 * DEEPNORM_BETA)
    return {
        "x": nrm((BATCH, SEQ, D), 1.0),
        "mem": nrm((BATCH, N_MEM, D), 1.0),
        "w_in": nrm((L, D, IN_WIDTH), D ** -0.5),
        "w_gate": nrm((L, D, N_BRANCH * D), D ** -0.5),
        "b_gate": nrm((L, N_BRANCH * D), 0.1),
        "ssm_lam_re": lam_re,
        "ssm_lam_im": lam_im,
        "ssm_log_step": log_step,
        "ssm_b_re": nrm((L, SSM_GROUPS, SSM_STATE, SSM_GROUP_CH), (2.0 * SSM_GROUP_CH) ** -0.5),
        "ssm_b_im": nrm((L, SSM_GROUPS, SSM_STATE, SSM_GROUP_CH), (2.0 * SSM_GROUP_CH) ** -0.5),
        "ssm_c_re": nrm((L, SSM_GROUPS, SSM_GROUP_CH, SSM_STATE), (2.0 * SSM_STATE) ** -0.5),
        "ssm_c_im": nrm((L, SSM_GROUPS, SSM_GROUP_CH, SSM_STATE), (2.0 * SSM_STATE) ** -0.5),
        "ssm_d": nrm((L, SSM_WIDTH), 1.0),
        "w_glu": nrm((L, SSM_WIDTH, SSM_WIDTH), SSM_WIDTH ** -0.5),
        "b_glu": nrm((L, SSM_WIDTH), 0.01),
        "gmlp_ln_g": 1.0 + nrm((L, GMLP_WIDTH), 0.01),
        "gmlp_ln_b": nrm((L, GMLP_WIDTH), 0.01),
        "w_spatial": nrm((L, GMLP_GROUPS, CHUNK, CHUNK), CHUNK ** -0.5),
        "b_spatial": 1.0 + nrm((L, GMLP_GROUPS, CHUNK), 0.01),
        "w_kv": jnp.concatenate([w_kv_k, w_kv_v], axis=-1),
        "w_br": nrm((L, N_BRANCH, BRANCH_WIDTH, D), BRANCH_WIDTH ** -0.5 * DEEPNORM_BETA),
        "w_out": nrm((L, D, D), D ** -0.5 * DEEPNORM_BETA),
        "ln1_g": 1.0 + nrm((L, D), 0.01),
        "ln1_b": nrm((L, D), 0.01),
        "w_router_g": nrm((L, D, N_GROUPS_MOE), D ** -0.5),
        "b_router_g": nrm((L, N_GROUPS_MOE), 0.01),
        "w_router_e": nrm((L, N_GROUPS_MOE, D, EXP_PER_GROUP), D ** -0.5),
        "b_router_e": nrm((L, N_GROUPS_MOE, EXP_PER_GROUP), 0.01),
        "w_exp_gate": nrm((L, N_EXPERTS, D, D_EXPERT), D ** -0.5),
        "w_exp_up": nrm((L, N_EXPERTS, D, D_EXPERT), D ** -0.5),
        "w_exp_down": nrm((L, N_EXPERTS, D_EXPERT, D), D_EXPERT ** -0.5 * DEEPNORM_BETA),
        "ln2_g": 1.0 + nrm((L, D), 0.01),
        "ln2_b": nrm((L, D), 0.01),
    }


def reference(x, mem, w_in, w_gate, b_gate, ssm_lam_re, ssm_lam_im, ssm_log_step,
              ssm_b_re, ssm_b_im, ssm_c_re, ssm_c_im, ssm_d, w_glu, b_glu,
              gmlp_ln_g, gmlp_ln_b, w_spatial, b_spatial, w_kv, w_br, w_out,
              ln1_g, ln1_b, w_router_g, b_router_g, w_router_e, b_router_e,
              w_exp_gate, w_exp_up, w_exp_down, ln2_g, ln2_b):
    bsz, seq, d = x.shape
    for l in range(DEPTH):
        proj = x @ w_in[l]
        u_ssm = proj[..., :SSM_WIDTH]
        uv_gmlp = proj[..., SSM_WIDTH:SSM_WIDTH + 2 * GMLP_WIDTH]
        q_mem = proj[..., SSM_WIDTH + 2 * GMLP_WIDTH:]
        y_ssm = _s5_branch(u_ssm, ssm_lam_re[l], ssm_lam_im[l], ssm_log_step[l],
                           ssm_b_re[l], ssm_b_im[l], ssm_c_re[l], ssm_c_im[l],
                           ssm_d[l], w_glu[l], b_glu[l])
        y_gmlp = _gmlp_branch(uv_gmlp, gmlp_ln_g[l], gmlp_ln_b[l], w_spatial[l], b_spatial[l])
        y_mem = _memory_attention(q_mem, mem, w_kv[l])
        gates = jax.nn.sigmoid(x @ w_gate[l] + b_gate[l]).reshape(bsz, seq, N_BRANCH, d)
        merged = (gates[:, :, 0] * (y_ssm @ w_br[l, 0])
                  + gates[:, :, 1] * (y_gmlp @ w_br[l, 1])
                  + gates[:, :, 2] * (y_mem @ w_br[l, 2]))
        x = _layer_norm(DEEPNORM_ALPHA * x + merged @ w_out[l], ln1_g[l], ln1_b[l])
        moe_out = _hier_moe(x, w_router_g[l], b_router_g[l], w_router_e[l], b_router_e[l],
                            w_exp_gate[l], w_exp_up[l], w_exp_down[l])
        x = _layer_norm(DEEPNORM_ALPHA * x + moe_out, ln2_g[l], ln2_b[l])
    return x
```

```python
import functools
import math

import jax
import jax.numpy as jnp
from jax import lax
from jax.experimental import pallas as pl
from jax.experimental.pallas import tpu as pltpu

F32 = jnp.float32
BF16 = jnp.bfloat16

LANES = 128
HALF = LANES // 2
SSM_T = HALF
GROUP_CH = 16
N_STATE = 64
LN_EPS = 1e-5
VMEM_LIMIT = 48 * 1024 * 1024

NT_DIMS = (((1,), (1,)), ((), ()))
TN_DIMS = (((0,), (0,)), ((), ()))


def _dot(a, b):
    return jnp.dot(a, b, preferred_element_type=F32)


def _dot_hi(a, b):
    return jnp.dot(a, b, preferred_element_type=F32, precision=lax.Precision.HIGHEST)


def _layer_norm(x, g, b):
    mu = jnp.mean(x, axis=-1, keepdims=True)
    xc = x - mu
    var = jnp.mean(xc * xc, axis=-1, keepdims=True)
    return xc * lax.rsqrt(var + LN_EPS) * g + b


def _lane_iota(shape):
    return lax.broadcasted_iota(jnp.int32, shape, len(shape) - 1)


def _swap_halves(x):
    return pltpu.roll(x, HALF, 1)


def _ssm_tables_kernel(prow_ref, pcol_ref, bt_ref, c_ref, ct_ref,
                       wtab_ref, pst_ref, g_ref, dec_ref):
    T = SSM_T
    prow = prow_ref[0, 0]
    pcol = pcol_ref[0, 0]
    bt2 = bt_ref[0, 0]
    c2 = c_ref[0, 0]
    ct = ct_ref[0, 0]
    lane1 = _lane_iota((1, LANES))
    lo1 = lane1 < HALF
    sgn1 = jnp.where(lo1, -1.0, 1.0).astype(F32)

    lane_bc = _lane_iota((GROUP_CH, LANES))
    lo_bc = lane_bc < HALF
    c2s = jnp.where(lo_bc, -1.0, 1.0) * _swap_halves(c2)

    bbar = []
    a_row = []
    th_row = []
    for d in range(2):
        lre = prow[2 * d:2 * d + 1]
        lim = prow[2 * d + 1:2 * d + 2]
        dt = jnp.exp(prow[4 + d:5 + d])
        a = lre * dt
        th = lim * dt
        er = jnp.exp(a)
        lb_re = er * jnp.cos(th)
        lb_im = er * jnp.sin(th)
        num_re = lb_re - 1.0
        den = lre * lre + lim * lim
        coef_re = (num_re * lre + lb_im * lim) / den
        coef_im = (lb_im * lre - num_re * lim) / den
        bbar.append(coef_re * bt2 + (sgn1 * coef_im) * _swap_halves(bt2))
        a_row.append(a)
        th_row.append(th)

    s_col = lax.broadcasted_iota(jnp.int32, (T, 1), 0).astype(F32)
    for d in range(2):
        k = (T - 1.0) - s_col if d == 0 else s_col
        e = jnp.exp(k * a_row[d])
        ang = k * th_row[d]
        pr = e * jnp.cos(ang)
        pi = e * jnp.sin(ang)
        bsw = sgn1 * _swap_halves(bbar[d])
        for hp in range(GROUP_CH):
            blk = pr * bbar[d][hp:hp + 1] + pi * bsw[hp:hp + 1]
            pst_ref[0, 0, hp * T:(hp + 1) * T, d * LANES:(d + 1) * LANES] = blk.astype(pst_ref.dtype)

    for d in range(2):
        e = jnp.exp(float(T) * a_row[d])
        ang = float(T) * th_row[d]
        dec_ref[0, 0, :, d * LANES:(d + 1) * LANES] = e * jnp.where(lo1, jnp.cos(ang), jnp.sin(ang))

    lane_p = _lane_iota((N_STATE, LANES))
    lo_p = lane_p < HALF
    t_lane = jnp.where(lo_p, lane_p, lane_p - HALF).astype(F32)
    pw = []
    for d in range(2):
        dtc = jnp.exp(pcol[:, 4 + d:5 + d])
        a_c = pcol[:, 2 * d:2 * d + 1] * dtc
        th_c = pcol[:, 2 * d + 1:2 * d + 2] * dtc
        pw.append((a_c, th_c))
    for d in range(2):
        a_c, th_c = pw[d]
        k = t_lane + 1.0 if d == 0 else float(T) - t_lane
        e = jnp.exp(k * a_c)
        pr = e * jnp.cos(k * th_c)
        pi = e * jnp.sin(k * th_c)
        for j in range(GROUP_CH // 2):
            cre = jnp.where(lo_p, ct[:, 2 * j:2 * j + 1], ct[:, 2 * j + 1:2 * j + 2])
            cim = jnp.where(lo_p, ct[:, GROUP_CH + 2 * j:GROUP_CH + 2 * j + 1],
                            ct[:, GROUP_CH + 2 * j + 1:GROUP_CH + 2 * j + 2])
            g_re = cre * pr - cim * pi
            g_im = -(cre * pi + cim * pr)
            g_ref[0, 0, d * LANES:d * LANES + N_STATE, j * LANES:(j + 1) * LANES] = g_re.astype(g_ref.dtype)
            g_ref[0, 0, d * LANES + N_STATE:(d + 1) * LANES, j * LANES:(j + 1) * LANES] = g_im.astype(g_ref.dtype)

    res = []
    kb0 = None
    for d in range(2):
        a_c, th_c = pw[d]
        if d == 0:
            k = jnp.maximum(lane_p - HALF, 0).astype(F32)
        else:
            k = jnp.maximum(HALF - lane_p, 0).astype(F32)
        e = jnp.exp(k * a_c)
        rhs = jnp.concatenate([e * jnp.cos(k * th_c), -(e * jnp.sin(k * th_c))], axis=0)
        bsw = _swap_halves(bbar[d])
        b_re = jnp.where(lo_bc, bbar[d], bsw)
        b_im = jnp.where(lo_bc, bsw, bbar[d])
        bc = (b_re[:, None, :] * c2[None, :, :] + b_im[:, None, :] * c2s[None, :, :])
        bc = bc.reshape(GROUP_CH * GROUP_CH, LANES)
        res.append(_dot_hi(bc, rhs))
        if d == 1:
            lane_bcf = _lane_iota(bc.shape)
            kb0 = jnp.sum(jnp.where(lane_bcf < HALF, bc, 0.0), axis=1, keepdims=True)
    lane_w = _lane_iota(res[0].shape)
    wtab = jnp.where(lane_w >= HALF, res[0], res[1])
    wtab_ref[0, 0] = wtab + jnp.where(lane_w == HALF, kb0, 0.0)


def _ssm_tables(prow, pcol, bt2, c2, ct2):
    L, G = prow.shape[0], prow.shape[1]
    T = SSM_T
    blk = lambda shape: pl.BlockSpec((1, 1) + shape, lambda l, g: (l, g, 0, 0))
    return pl.pallas_call(
        _ssm_tables_kernel,
        grid=(L, G),
        in_specs=[blk((8, LANES)), blk((N_STATE, 8)), blk((GROUP_CH, LANES)),
                  blk((GROUP_CH, LANES)), blk((N_STATE, 2 * GROUP_CH))],
        out_specs=[blk((GROUP_CH * GROUP_CH, LANES)), blk((GROUP_CH * T, 2 * LANES)),
                   blk((2 * LANES, GROUP_CH * T)), blk((1, 2 * LANES))],
        out_shape=[jax.ShapeDtypeStruct((L, G, GROUP_CH * GROUP_CH, LANES), F32),
                   jax.ShapeDtypeStruct((L, G, GROUP_CH * T, 2 * LANES), BF16),
                   jax.ShapeDtypeStruct((L, G, 2 * LANES, GROUP_CH * T), BF16),
                   jax.ShapeDtypeStruct((L, G, 1, 2 * LANES), F32)],
        name="ssm_tables",
    )(prow, pcol, bt2, c2, ct2)


def _cmul_packed(x, d_re, d_sw):
    return x * d_re + _swap_halves(x) * d_sw


def _ssm_kernel(z_ref, wtab_ref, pst_ref, g_ref, dec_ref, dvec_ref, o_ref, m_ref, *, rows_per_batch):
    T = SSM_T
    n_rows = z_ref.shape[1] // GROUP_CH
    lane_t = _lane_iota((T, LANES))
    lo_t = lane_t < HALF

    def build(hp, carry):
        r0 = pl.multiple_of(hp * T, T)
        for j in range(GROUP_CH // 2):
            wa = jnp.broadcast_to(wtab_ref[0, 0, pl.ds(hp * GROUP_CH + 2 * j, 1), :], (T, LANES))
            wb = jnp.broadcast_to(wtab_ref[0, 0, pl.ds(hp * GROUP_CH + 2 * j + 1, 1), :], (T, LANES))
            ra = pltpu.roll(wa, HALF, 1, stride=1, stride_axis=0)
            rb = pltpu.roll(wb, 0, 1, stride=1, stride_axis=0)
            m_ref[pl.ds(r0, T), j * LANES:(j + 1) * LANES] = jnp.where(lo_t, ra, rb).astype(m_ref.dtype)
        return carry

    lax.fori_loop(0, GROUP_CH, build, 0)

    lane = _lane_iota((n_rows, LANES))
    lo = lane < HALF
    a0, a1 = [], []
    for j in range(GROUP_CH // 2):
        pe = z_ref[0, pl.ds(2 * j, n_rows, stride=GROUP_CH), :]
        po = z_ref[0, pl.ds(2 * j + 1, n_rows, stride=GROUP_CH), :]
        a0.append(jnp.where(lo, pe, _swap_halves(po)))
        a1.append(jnp.where(lo, _swap_halves(pe), po))
    a0 = jnp.concatenate(a0, axis=1)
    a1 = jnp.concatenate(a1, axis=1)
    a_f32 = jnp.concatenate([a0, a1], axis=0)
    a_bf = a_f32.astype(BF16)

    out = _dot(a_bf, m_ref[...]) + dvec_ref[0, 0] * a_f32
    s = _dot(a_bf, pst_ref[0, 0])
    sf0, sb0 = s[:n_rows, :LANES], s[:n_rows, LANES:]
    sf1, sb1 = s[n_rows:, :LANES], s[n_rows:, LANES:]

    sgn = jnp.where(_lane_iota((1, LANES)) < HALF, -1.0, 1.0).astype(F32)
    lo1 = _lane_iota((1, LANES)) < HALF

    def split(p):
        sw = _swap_halves(p)
        return jnp.where(lo1, p, sw), sgn * jnp.where(lo1, sw, p)

    def square(p):
        d_re, d_sw = split(p)
        return _cmul_packed(p, d_re, d_sw)

    dec = dec_ref[0, 0]
    dec_f, dec_b = dec[:, :LANES], dec[:, LANES:]
    df_re, df_sw = split(dec_f)
    db_re, db_sw = split(dec_b)

    row = lax.broadcasted_iota(jnp.int32, (n_rows, LANES), 0)
    rib = row % rows_per_batch

    ef = _cmul_packed(sf0, df_re, df_sw) + sf1
    eb = sb0 + _cmul_packed(sb1, db_re, db_sw)
    pf, pb = square(dec_f), square(dec_b)
    step = 1
    while step < rows_per_batch:
        pf_re, pf_sw = split(pf)
        pb_re, pb_sw = split(pb)
        shf = jnp.where(rib >= step, pltpu.roll(ef, step, 0), 0.0)
        ef = ef + _cmul_packed(shf, pf_re, pf_sw)
        shb = jnp.where(rib < rows_per_batch - step, pltpu.roll(eb, n_rows - step, 0), 0.0)
        eb = eb + _cmul_packed(shb, pb_re, pb_sw)
        pf, pb = square(pf), square(pb)
        step *= 2
    hf0 = jnp.where(rib >= 1, pltpu.roll(ef, 1, 0), 0.0)
    hf1 = _cmul_packed(hf0, df_re, df_sw) + sf0
    hb1 = jnp.where(rib < rows_per_batch - 1, pltpu.roll(eb, n_rows - 1, 0), 0.0)
    hb0 = sb1 + _cmul_packed(hb1, db_re, db_sw)
    h_in = jnp.concatenate([jnp.concatenate([hf0, hb0], axis=1),
                            jnp.concatenate([hf1, hb1], axis=1)], axis=0)
    out = out + _dot(h_in.astype(BF16), g_ref[0, 0])

    o0, o1 = out[:n_rows], out[n_rows:]
    for j in range(GROUP_CH // 2):
        t0 = o0[:, j * LANES:(j + 1) * LANES]
        t1 = o1[:, j * LANES:(j + 1) * LANES]
        o_ref[0, pl.ds(2 * j, n_rows, stride=GROUP_CH), :] = jnp.where(lo, t0, _swap_halves(t1))
        o_ref[0, pl.ds(2 * j + 1, n_rows, stride=GROUP_CH), :] = jnp.where(lo, _swap_halves(t0), t1)


def _ssm_apply(z, wtab, pst, gmat, dec, dvec, layer, rows_per_batch):
    G, R, _ = z.shape
    T = SSM_T
    tab = lambda shape: pl.BlockSpec((1, 1) + shape, lambda g: (layer, g, 0, 0))
    return pl.pallas_call(
        functools.partial(_ssm_kernel, rows_per_batch=rows_per_batch),
        grid=(G,),
        in_specs=[pl.BlockSpec((1, R, LANES), lambda g: (g, 0, 0)),
                  tab((GROUP_CH * GROUP_CH, LANES)), tab((GROUP_CH * T, 2 * LANES)),
                  tab((2 * LANES, GROUP_CH * T)), tab((1, 2 * LANES)), tab((1, GROUP_CH * T))],
        out_specs=pl.BlockSpec((1, R, LANES), lambda g: (g, 0, 0)),
        out_shape=jax.ShapeDtypeStruct(z.shape, F32),
        scratch_shapes=[pltpu.VMEM((GROUP_CH * T, GROUP_CH * T), BF16)],
        compiler_params=pltpu.CompilerParams(dimension_semantics=("arbitrary",),
                                             vmem_limit_bytes=VMEM_LIMIT),
        name="ssm_seq",
    )(z, wtab, pst, gmat, dec, dvec)


def _kv_kernel(mem_ref, w_ref, o_ref):
    o_ref[0] = _dot(mem_ref[0].astype(BF16), w_ref[...]).astype(o_ref.dtype)


def _kv_proj(mem, w_kv):
    B, M, D = mem.shape
    return pl.pallas_call(
        _kv_kernel,
        grid=(B,),
        in_specs=[pl.BlockSpec((1, M, D), lambda b: (b, 0, 0)),
                  pl.BlockSpec(w_kv.shape, lambda b: (0, 0))],
        out_specs=pl.BlockSpec((1, M, w_kv.shape[1]), lambda b: (b, 0, 0)),
        out_shape=jax.ShapeDtypeStruct((B, M, w_kv.shape[1]), BF16),
        name="kv_proj",
    )(mem, w_kv)


def _front_kernel(x_ref, wssm_t_ref, wuv_ref, wq_ref, wg_ref, bg_ref, lng_ref, lnb_ref,
                  ws_ref, bs_t_ref, kv_ref, wbr1_ref, wbr2_ref, z_ref, m_ref, *, n_heads):
    tn, d_model = x_ref.shape
    xb = x_ref[...].astype(BF16)

    u_t = lax.dot_general(wssm_t_ref[...], xb, NT_DIMS, preferred_element_type=F32)
    n_grp = u_t.shape[0] // GROUP_CH
    for a in range(tn // LANES):
        z_ref[:, a * GROUP_CH:(a + 1) * GROUP_CH, :] = (
            u_t[:, a * LANES:(a + 1) * LANES].reshape(n_grp, GROUP_CH, LANES))

    uv = jax.nn.gelu(_dot(xb, wuv_ref[...]))
    half = uv.shape[1] // 2
    u, v = uv[:, :half], uv[:, half:]
    vb = _layer_norm(v, lng_ref[...], lnb_ref[...]).astype(BF16)
    lane = _lane_iota((LANES, LANES))
    lo = lane < HALF
    s_rows = []
    for a in range(tn // LANES):
        va = vb[a * LANES:(a + 1) * LANES]
        tiles = []
        for j in range(half // LANES):
            rhs = va[:, j * LANES:(j + 1) * LANES]
            ev = _dot(ws_ref[2 * j], rhs)
            od = _dot(ws_ref[2 * j + 1], rhs)
            bias = jnp.where(lo, bs_t_ref[:, 2 * j:2 * j + 1], bs_t_ref[:, 2 * j + 1:2 * j + 2])
            tiles.append(jnp.where(lo, ev, od) + bias)
        s_rows.append(jnp.concatenate(tiles, axis=1))
    y_gmlp = u * jnp.concatenate(s_rows, axis=0)

    q = _dot(xb, wq_ref[...])
    hd = q.shape[1] // n_heads
    kv = kv_ref[0]
    outs = []
    for h in range(n_heads):
        qh = q[:, h * hd:(h + 1) * hd].astype(BF16)
        kh = kv[:, h * hd:(h + 1) * hd]
        vh = kv[:, q.shape[1] + h * hd:q.shape[1] + (h + 1) * hd]
        sc = lax.dot_general(qh, kh, NT_DIMS, preferred_element_type=F32) * (hd ** -0.5)
        sc = sc - jnp.max(sc, axis=-1, keepdims=True)
        p = jnp.exp(sc)
        p = p / jnp.sum(p, axis=-1, keepdims=True)
        outs.append(_dot(p.astype(BF16), vh))
    y_mem = jnp.concatenate(outs, axis=1)

    gates = jax.nn.sigmoid(_dot(xb, wg_ref[...]) + bg_ref[...])
    m_ref[...] = (gates[:, :d_model] * _dot(y_gmlp.astype(BF16), wbr1_ref[...])
                  + gates[:, d_model:] * _dot(y_mem.astype(BF16), wbr2_ref[...]))


def _front(x2, wssm_t, wuv, wq, wg12, bg12, lng, lnb, ws, bs_t, kv, wbr1, wbr2, *, tn, n_heads,
           tokens_per_batch):
    n, d = x2.shape
    n_grp = wssm_t.shape[0] // GROUP_CH
    rows = tn // LANES * GROUP_CH
    full = lambda arr: pl.BlockSpec(arr.shape, lambda i: (0,) * arr.ndim)
    tiles_per_batch = tokens_per_batch // tn
    return pl.pallas_call(
        functools.partial(_front_kernel, n_heads=n_heads),
        grid=(n // tn,),
        in_specs=[pl.BlockSpec((tn, d), lambda i: (i, 0)),
                  full(wssm_t), full(wuv), full(wq), full(wg12), full(bg12), full(lng), full(lnb),
                  full(ws), full(bs_t),
                  pl.BlockSpec((1,) + kv.shape[1:], lambda i: (i // tiles_per_batch, 0, 0)),
                  full(wbr1), full(wbr2)],
        out_specs=[pl.BlockSpec((n_grp, rows, LANES), lambda i: (0, i, 0)),
                   pl.BlockSpec((tn, d), lambda i: (i, 0))],
        out_shape=[jax.ShapeDtypeStruct((n_grp, n // LANES * GROUP_CH, LANES), F32),
                   jax.ShapeDtypeStruct((n, d), F32)],
        compiler_params=pltpu.CompilerParams(dimension_semantics=("arbitrary",),
                                             vmem_limit_bytes=VMEM_LIMIT),
        name="mixer_front",
    )(x2, wssm_t, wuv, wq, wg12, bg12, lng, lnb, ws, bs_t, kv, wbr1, wbr2)


def _route(lg, le):
    n_exp = float(le.shape[1])
    lane = _lane_iota(le.shape).astype(F32)
    gmax = jnp.max(lg, axis=-1, keepdims=True)
    denom = jnp.sum(jnp.exp(lg - gmax), axis=-1, keepdims=True) * (1.0 / 8.0)
    pg_sel = 1.0 / denom
    first = jnp.min(jnp.where(lg == gmax, lane, n_exp), axis=-1, keepdims=True)
    in_grp = jnp.floor(lane * 0.125) == jnp.floor(first * 0.125)
    neg = -jnp.inf
    l1 = jnp.where(in_grp, le, neg)
    m1 = jnp.max(l1, axis=-1, keepdims=True)
    i1 = jnp.min(jnp.where(l1 == m1, lane, n_exp), axis=-1, keepdims=True)
    l2 = jnp.where(lane == i1, neg, l1)
    m2 = jnp.max(l2, axis=-1, keepdims=True)
    i2 = jnp.min(jnp.where(l2 == m2, lane, n_exp), axis=-1, keepdims=True)
    e2 = jnp.exp(m2 - m1)
    p1 = 1.0 / (1.0 + e2)
    p2 = e2 / (1.0 + e2)
    return jnp.where(lane == i1, pg_sel * p1, 0.0) + jnp.where(lane == i2, pg_sel * p2, 0.0)


def _back_kernel(x_ref, m12_ref, y_ref, wglu_t_ref, bglu_ref, wbr0_ref, wg0_ref, bg0_ref, wout_ref,
                 lng_ref, lnb_ref, wrg_ref, brg_ref, wre_ref, bre_ref, x1_ref, wts_ref, *, alpha):
    tn = x_ref.shape[0]
    x = x_ref[...]
    xb = x.astype(BF16)
    n_grp = y_ref.shape[0]
    cols = []
    for a in range(tn // LANES):
        y_t = y_ref[:, a * GROUP_CH:(a + 1) * GROUP_CH, :].reshape(n_grp * GROUP_CH, LANES)
        y_t = jax.nn.gelu(y_t)
        gate = jax.nn.sigmoid(_dot(wglu_t_ref[...], y_t.astype(BF16)) + bglu_ref[...])
        cols.append((y_t * gate).astype(BF16))
    ys_t = jnp.concatenate(cols, axis=1)
    br0 = lax.dot_general(ys_t, wbr0_ref[...], TN_DIMS, preferred_element_type=F32)
    g0 = jax.nn.sigmoid(_dot(xb, wg0_ref[...]) + bg0_ref[...])
    merged = m12_ref[...] + g0 * br0
    h = _dot(merged.astype(BF16), wout_ref[...])
    x1 = _layer_norm(alpha * x + h, lng_ref[...], lnb_ref[...])
    x1_ref[...] = x1
    lg = _dot_hi(x1, wrg_ref[...]) + brg_ref[...]
    le = _dot_hi(x1, wre_ref[...]) + bre_ref[...]
    wts_ref[...] = _route(lg, le)


def _back(x2, m12, y, wglu_t, bglu, wbr0, wg0, bg0, wout, lng, lnb, wrg, brg, wre, bre, *, tn, alpha):
    n, d = x2.shape
    n_grp = y.shape[0]
    rows = tn // LANES * GROUP_CH
    n_exp = wre.shape[1]
    full = lambda arr: pl.BlockSpec(arr.shape, lambda i: (0,) * arr.ndim)
    return pl.pallas_call(
        functools.partial(_back_kernel, alpha=alpha),
        grid=(n // tn,),
        in_specs=[pl.BlockSpec((tn, d), lambda i: (i, 0)),
                  pl.BlockSpec((tn, d), lambda i: (i, 0)),
                  pl.BlockSpec((n_grp, rows, LANES), lambda i: (0, i, 0)),
                  full(wglu_t), full(bglu), full(wbr0), full(wg0), full(bg0), full(wout),
                  full(lng), full(lnb), full(wrg), full(brg), full(wre), full(bre)],
        out_specs=[pl.BlockSpec((tn, d), lambda i: (i, 0)),
                   pl.BlockSpec((tn, n_exp), lambda i: (i, 0))],
        out_shape=[jax.ShapeDtypeStruct((n, d), F32),
                   jax.ShapeDtypeStruct((n, n_exp), F32)],
        compiler_params=pltpu.CompilerParams(dimension_semantics=("arbitrary",),
                                             vmem_limit_bytes=VMEM_LIMIT),
        name="mixer_back",
    )(x2, m12, y, wglu_t, bglu, wbr0, wg0, bg0, wout, lng, lnb, wrg, brg, wre, bre)


def _moe_kernel(x_ref, wts_ref, wg_ref, wu_ref, wd_ref, lng_ref, lnb_ref, o_ref, acc_ref, xb_ref, *, alpha):
    e = pl.program_id(1)

    @pl.when(e == 0)
    def _():
        acc_ref[...] = jnp.zeros_like(acc_ref)
        xb_ref[...] = x_ref[...].astype(BF16)

    xb = xb_ref[...]
    h = jax.nn.silu(_dot(xb, wg_ref[0])) * _dot(xb, wu_ref[0])
    wts = wts_ref[...]
    w_e = jnp.sum(jnp.where(_lane_iota(wts.shape) == e, wts, 0.0), axis=-1, keepdims=True)
    acc_ref[...] += _dot((h * w_e).astype(BF16), wd_ref[0])

    @pl.when(e == pl.num_programs(1) - 1)
    def _():
        o_ref[...] = _layer_norm(alpha * x_ref[...] + acc_ref[...], lng_ref[...], lnb_ref[...])


def _moe(x1, wts, wg, wu, wd, lng, lnb, *, tm, alpha):
    n, d = x1.shape
    n_exp, _, f = wg.shape
    full = lambda arr: pl.BlockSpec(arr.shape, lambda i, e: (0,) * arr.ndim)
    return pl.pallas_call(
        functools.partial(_moe_kernel, alpha=alpha),
        grid=(n // tm, n_exp),
        in_specs=[pl.BlockSpec((tm, d), lambda i, e: (i, 0)),
                  pl.BlockSpec((tm, n_exp), lambda i, e: (i, 0)),
                  pl.BlockSpec((1, d, f), lambda i, e: (e, 0, 0)),
                  pl.BlockSpec((1, d, f), lambda i, e: (e, 0, 0)),
                  pl.BlockSpec((1, f, d), lambda i, e: (e, 0, 0)),
                  full(lng), full(lnb)],
        out_specs=pl.BlockSpec((tm, d), lambda i, e: (i, 0)),
        out_shape=jax.ShapeDtypeStruct((n, d), F32),
        scratch_shapes=[pltpu.VMEM((tm, d), F32), pltpu.VMEM((tm, d), BF16)],
        compiler_params=pltpu.CompilerParams(dimension_semantics=("arbitrary", "arbitrary"),
                                             vmem_limit_bytes=VMEM_LIMIT),
        name="moe_dense",
    )(x1, wts, wg, wu, wd, lng, lnb)


def _dup(v):
    return jnp.concatenate([v, v], axis=-1)


def kernel(x, mem, w_in, w_gate, b_gate, ssm_lam_re, ssm_lam_im, ssm_log_step, ssm_b_re, ssm_b_im, ssm_c_re, ssm_c_im, ssm_d, w_glu, b_glu, gmlp_ln_g, gmlp_ln_b, w_spatial, b_spatial, w_kv, w_br, w_out, ln1_g, ln1_b, w_router_g, b_router_g, w_router_e, b_router_e, w_exp_gate, w_exp_up, w_exp_down, ln2_g, ln2_b):
    bsz, seq, d = x.shape
    depth = w_in.shape[0]
    n = bsz * seq
    n_grp = ssm_lam_re.shape[2]
    ssm_w = n_grp * GROUP_CH
    gmlp_w = gmlp_ln_g.shape[1]
    n_heads = 4
    n_moe_grp, exp_per_grp = b_router_e.shape[1], b_router_e.shape[2]
    alpha = (2.0 * depth) ** 0.25
    tn = 256
    tm = 512 if n % 512 == 0 else 256

    ls = jnp.broadcast_to(ssm_log_step[..., None], ssm_lam_re.shape)
    six = [ssm_lam_re[:, 0], ssm_lam_im[:, 0], ssm_lam_re[:, 1], ssm_lam_im[:, 1], ls[:, 0], ls[:, 1]]
    prow = jnp.stack([_dup(v) for v in six] + [jnp.zeros_like(_dup(six[0]))] * 2, axis=2)
    pcol = jnp.stack(six + [jnp.zeros_like(six[0])] * 2, axis=-1)
    bt2 = jnp.concatenate([jnp.swapaxes(ssm_b_re, 2, 3), jnp.swapaxes(ssm_b_im, 2, 3)], axis=-1)
    c2 = jnp.concatenate([ssm_c_re, ssm_c_im], axis=-1)
    ct2 = jnp.concatenate([jnp.swapaxes(ssm_c_re, 2, 3), jnp.swapaxes(ssm_c_im, 2, 3)], axis=-1)
    wtab, pst, gmat, dec = _ssm_tables(prow, pcol, bt2, c2, ct2)
    dvec = jnp.repeat(ssm_d.reshape(depth, n_grp, 1, GROUP_CH), SSM_T, axis=-1)

    x2 = x.reshape(n, d)
    for l in range(depth):
        wl = w_in[l].astype(BF16)
        wssm_t = wl[:, :ssm_w].T
        wuv = wl[:, ssm_w:ssm_w + 2 * gmlp_w]
        wq = wl[:, ssm_w + 2 * gmlp_w:]
        wg = w_gate[l].astype(BF16)
        bg = b_gate[l].reshape(1, -1)
        kv = _kv_proj(mem, w_kv[l].astype(BF16))
        z, m12 = _front(
            x2, wssm_t, wuv, wq, wg[:, d:], bg[:, d:],
            gmlp_ln_g[l].reshape(1, -1), gmlp_ln_b[l].reshape(1, -1),
            w_spatial[l].astype(BF16), b_spatial[l].T, kv,
            w_br[l, 1].astype(BF16), w_br[l, 2].astype(BF16),
            tn=tn, n_heads=n_heads, tokens_per_batch=seq)
        y = _ssm_apply(z, wtab, pst, gmat, dec, dvec, l, seq // LANES)
        wrg = jnp.repeat(w_router_g[l], exp_per_grp, axis=1)
        brg = jnp.repeat(b_router_g[l], exp_per_grp).reshape(1, -1)
        wre = jnp.transpose(w_router_e[l], (1, 0, 2)).reshape(d, n_moe_grp * exp_per_grp)
        bre = b_router_e[l].reshape(1, -1)
        x1, wts = _back(
            x2, m12, y, w_glu[l].T.astype(BF16), b_glu[l].reshape(-1, 1), w_br[l, 0].astype(BF16),
            wg[:, :d], bg[:, :d], w_out[l].astype(BF16),
            ln1_g[l].reshape(1, -1), ln1_b[l].reshape(1, -1), wrg, brg, wre, bre,
            tn=tn, alpha=alpha)
        x2 = _moe(x1, wts, w_exp_gate[l].astype(BF16), w_exp_up[l].astype(BF16),
                  w_exp_down[l].astype(BF16), ln2_g[l].reshape(1, -1), ln2_b[l].reshape(1, -1),
                  tm=tm, alpha=alpha)
    return x2.reshape(bsz, seq, d)
```

```python
import functools
import math

import jax
import jax.numpy as jnp
from jax import lax
from jax.experimental import pallas as pl
from jax.experimental.pallas import tpu as pltpu
from jax.experimental.pallas import tpu_sc as plsc

F32 = jnp.float32
BF16 = jnp.bfloat16

LANES = 128
HALF = LANES // 2
SSM_T = HALF
GROUP_CH = 16
N_STATE = 64
LN_EPS = 1e-5
VMEM_LIMIT = 48 * 1024 * 1024

NT_DIMS = (((1,), (1,)), ((), ()))
TN_DIMS = (((0,), (0,)), ((), ()))


def _dot(a, b):
    return jnp.dot(a, b, preferred_element_type=F32)


def _dot_hi(a, b):
    return jnp.dot(a, b, preferred_element_type=F32, precision=lax.Precision.HIGHEST)


def _layer_norm(x, g, b):
    mu = jnp.mean(x, axis=-1, keepdims=True)
    xc = x - mu
    var = jnp.mean(xc * xc, axis=-1, keepdims=True)
    return xc * lax.rsqrt(var + LN_EPS) * g + b


def _lane_iota(shape):
    return lax.broadcasted_iota(jnp.int32, shape, len(shape) - 1)


def _swap_halves(x):
    return pltpu.roll(x, HALF, 1)


def _ssm_tables_kernel(prow_ref, pcol_ref, bt_ref, c_ref, ct_ref,
                       wtab_ref, pst_ref, g_ref, dec_ref):
    T = SSM_T
    prow = prow_ref[0, 0]
    pcol = pcol_ref[0, 0]
    bt2 = bt_ref[0, 0]
    c2 = c_ref[0, 0]
    ct = ct_ref[0, 0]
    lane1 = _lane_iota((1, LANES))
    lo1 = lane1 < HALF
    sgn1 = jnp.where(lo1, -1.0, 1.0).astype(F32)

    lane_bc = _lane_iota((GROUP_CH, LANES))
    lo_bc = lane_bc < HALF
    c2s = jnp.where(lo_bc, -1.0, 1.0) * _swap_halves(c2)

    bbar = []
    a_row = []
    th_row = []
    for d in range(2):
        lre = prow[2 * d:2 * d + 1]
        lim = prow[2 * d + 1:2 * d + 2]
        dt = jnp.exp(prow[4 + d:5 + d])
        a = lre * dt
        th = lim * dt
        er = jnp.exp(a)
        lb_re = er * jnp.cos(th)
        lb_im = er * jnp.sin(th)
        num_re = lb_re - 1.0
        den = lre * lre + lim * lim
        coef_re = (num_re * lre + lb_im * lim) / den
        coef_im = (lb_im * lre - num_re * lim) / den
        bbar.append(coef_re * bt2 + (sgn1 * coef_im) * _swap_halves(bt2))
        a_row.append(a)
        th_row.append(th)

    s_col = lax.broadcasted_iota(jnp.int32, (T, 1), 0).astype(F32)
    for d in range(2):
        k = (T - 1.0) - s_col if d == 0 else s_col
        e = jnp.exp(k * a_row[d])
        ang = k * th_row[d]
        pr = e * jnp.cos(ang)
        pi = e * jnp.sin(ang)
        bsw = sgn1 * _swap_halves(bbar[d])
        for hp in range(GROUP_CH):
            blk = pr * bbar[d][hp:hp + 1] + pi * bsw[hp:hp + 1]
            pst_ref[0, 0, hp * T:(hp + 1) * T, d * LANES:(d + 1) * LANES] = blk.astype(pst_ref.dtype)

    for d in range(2):
        e = jnp.exp(float(T) * a_row[d])
        ang = float(T) * th_row[d]
        dec_ref[0, 0, :, d * LANES:(d + 1) * LANES] = e * jnp.where(lo1, jnp.cos(ang), jnp.sin(ang))

    lane_p = _lane_iota((N_STATE, LANES))
    lo_p = lane_p < HALF
    t_lane = jnp.where(lo_p, lane_p, lane_p - HALF).astype(F32)
    pw = []
    for d in range(2):
        dtc = jnp.exp(pcol[:, 4 + d:5 + d])
        a_c = pcol[:, 2 * d:2 * d + 1] * dtc
        th_c = pcol[:, 2 * d + 1:2 * d + 2] * dtc
        pw.append((a_c, th_c))
    for d in range(2):
        a_c, th_c = pw[d]
        k = t_lane + 1.0 if d == 0 else float(T) - t_lane
        e = jnp.exp(k * a_c)
        pr = e * jnp.cos(k * th_c)
        pi = e * jnp.sin(k * th_c)
        for j in range(GROUP_CH // 2):
            cre = jnp.where(lo_p, ct[:, 2 * j:2 * j + 1], ct[:, 2 * j + 1:2 * j + 2])
            cim = jnp.where(lo_p, ct[:, GROUP_CH + 2 * j:GROUP_CH + 2 * j + 1],
                            ct[:, GROUP_CH + 2 * j + 1:GROUP_CH + 2 * j + 2])
            g_re = cre * pr - cim * pi
            g_im = -(cre * pi + cim * pr)
            g_ref[0, 0, d * LANES:d * LANES + N_STATE, j * LANES:(j + 1) * LANES] = g_re.astype(g_ref.dtype)
            g_ref[0, 0, d * LANES + N_STATE:(d + 1) * LANES, j * LANES:(j + 1) * LANES] = g_im.astype(g_ref.dtype)

    res = []
    kb0 = None
    for d in range(2):
        a_c, th_c = pw[d]
        if d == 0:
            k = jnp.maximum(lane_p - HALF, 0).astype(F32)
        else:
            k = jnp.maximum(HALF - lane_p, 0).astype(F32)
        e = jnp.exp(k * a_c)
        rhs = jnp.concatenate([e * jnp.cos(k * th_c), -(e * jnp.sin(k * th_c))], axis=0)
        bsw = _swap_halves(bbar[d])
        b_re = jnp.where(lo_bc, bbar[d], bsw)
        b_im = jnp.where(lo_bc, bsw, bbar[d])
        bc = (b_re[:, None, :] * c2[None, :, :] + b_im[:, None, :] * c2s[None, :, :])
        bc = bc.reshape(GROUP_CH * GROUP_CH, LANES)
        res.append(_dot_hi(bc, rhs))
        if d == 1:
            lane_bcf = _lane_iota(bc.shape)
            kb0 = jnp.sum(jnp.where(lane_bcf < HALF, bc, 0.0), axis=1, keepdims=True)
    lane_w = _lane_iota(res[0].shape)
    wtab = jnp.where(lane_w >= HALF, res[0], res[1])
    wtab_ref[0, 0] = wtab + jnp.where(lane_w == HALF, kb0, 0.0)


def _ssm_tables(prow, pcol, bt2, c2, ct2):
    L, G = prow.shape[0], prow.shape[1]
    T = SSM_T
    blk = lambda shape: pl.BlockSpec((1, 1) + shape, lambda l, g: (l, g, 0, 0))
    return pl.pallas_call(
        _ssm_tables_kernel,
        grid=(L, G),
        in_specs=[blk((8, LANES)), blk((N_STATE, 8)), blk((GROUP_CH, LANES)),
                  blk((GROUP_CH, LANES)), blk((N_STATE, 2 * GROUP_CH))],
        out_specs=[blk((GROUP_CH * GROUP_CH, LANES)), blk((GROUP_CH * T, 2 * LANES)),
                   blk((2 * LANES, GROUP_CH * T)), blk((1, 2 * LANES))],
        out_shape=[jax.ShapeDtypeStruct((L, G, GROUP_CH * GROUP_CH, LANES), F32),
                   jax.ShapeDtypeStruct((L, G, GROUP_CH * T, 2 * LANES), BF16),
                   jax.ShapeDtypeStruct((L, G, 2 * LANES, GROUP_CH * T), BF16),
                   jax.ShapeDtypeStruct((L, G, 1, 2 * LANES), F32)],
        name="ssm_tables",
    )(prow, pcol, bt2, c2, ct2)


def _cmul_packed(x, d_re, d_sw):
    return x * d_re + _swap_halves(x) * d_sw


def _ssm_kernel(z_ref, wtab_ref, pst_ref, g_ref, dec_ref, dvec_ref, o_ref, m_ref, *, rows_per_batch):
    T = SSM_T
    n_rows = z_ref.shape[1] // GROUP_CH
    lane_t = _lane_iota((T, LANES))
    lo_t = lane_t < HALF

    def build(hp, carry):
        r0 = pl.multiple_of(hp * T, T)
        for j in range(GROUP_CH // 2):
            wa = jnp.broadcast_to(wtab_ref[0, 0, pl.ds(hp * GROUP_CH + 2 * j, 1), :], (T, LANES))
            wb = jnp.broadcast_to(wtab_ref[0, 0, pl.ds(hp * GROUP_CH + 2 * j + 1, 1), :], (T, LANES))
            ra = pltpu.roll(wa, HALF, 1, stride=1, stride_axis=0)
            rb = pltpu.roll(wb, 0, 1, stride=1, stride_axis=0)
            m_ref[pl.ds(r0, T), j * LANES:(j + 1) * LANES] = jnp.where(lo_t, ra, rb).astype(m_ref.dtype)
        return carry

    lax.fori_loop(0, GROUP_CH, build, 0)

    lane = _lane_iota((n_rows, LANES))
    lo = lane < HALF
    a0, a1 = [], []
    for j in range(GROUP_CH // 2):
        pe = z_ref[0, pl.ds(2 * j, n_rows, stride=GROUP_CH), :]
        po = z_ref[0, pl.ds(2 * j + 1, n_rows, stride=GROUP_CH), :]
        a0.append(jnp.where(lo, pe, _swap_halves(po)))
        a1.append(jnp.where(lo, _swap_halves(pe), po))
    a0 = jnp.concatenate(a0, axis=1)
    a1 = jnp.concatenate(a1, axis=1)
    a_f32 = jnp.concatenate([a0, a1], axis=0)
    a_bf = a_f32.astype(BF16)

    out = _dot(a_bf, m_ref[...]) + dvec_ref[0, 0] * a_f32
    s = _dot(a_bf, pst_ref[0, 0])
    sf0, sb0 = s[:n_rows, :LANES], s[:n_rows, LANES:]
    sf1, sb1 = s[n_rows:, :LANES], s[n_rows:, LANES:]

    sgn = jnp.where(_lane_iota((1, LANES)) < HALF, -1.0, 1.0).astype(F32)
    lo1 = _lane_iota((1, LANES)) < HALF

    def split(p):
        sw = _swap_halves(p)
        return jnp.where(lo1, p, sw), sgn * jnp.where(lo1, sw, p)

    def square(p):
        d_re, d_sw = split(p)
        return _cmul_packed(p, d_re, d_sw)

    dec = dec_ref[0, 0]
    dec_f, dec_b = dec[:, :LANES], dec[:, LANES:]
    df_re, df_sw = split(dec_f)
    db_re, db_sw = split(dec_b)

    row = lax.broadcasted_iota(jnp.int32, (n_rows, LANES), 0)
    rib = row % rows_per_batch

    ef = _cmul_packed(sf0, df_re, df_sw) + sf1
    eb = sb0 + _cmul_packed(sb1, db_re, db_sw)
    pf, pb = square(dec_f), square(dec_b)
    step = 1
    while step < rows_per_batch:
        pf_re, pf_sw = split(pf)
        pb_re, pb_sw = split(pb)
        shf = jnp.where(rib >= step, pltpu.roll(ef, step, 0), 0.0)
        ef = ef + _cmul_packed(shf, pf_re, pf_sw)
        shb = jnp.where(rib < rows_per_batch - step, pltpu.roll(eb, n_rows - step, 0), 0.0)
        eb = eb + _cmul_packed(shb, pb_re, pb_sw)
        pf, pb = square(pf), square(pb)
        step *= 2
    hf0 = jnp.where(rib >= 1, pltpu.roll(ef, 1, 0), 0.0)
    hf1 = _cmul_packed(hf0, df_re, df_sw) + sf0
    hb1 = jnp.where(rib < rows_per_batch - 1, pltpu.roll(eb, n_rows - 1, 0), 0.0)
    hb0 = sb1 + _cmul_packed(hb1, db_re, db_sw)
    h_in = jnp.concatenate([jnp.concatenate([hf0, hb0], axis=1),
                            jnp.concatenate([hf1, hb1], axis=1)], axis=0)
    out = out + _dot(h_in.astype(BF16), g_ref[0, 0])

    o0, o1 = out[:n_rows], out[n_rows:]
    for j in range(GROUP_CH // 2):
        t0 = o0[:, j * LANES:(j + 1) * LANES]
        t1 = o1[:, j * LANES:(j + 1) * LANES]
        o_ref[0, pl.ds(2 * j, n_rows, stride=GROUP_CH), :] = jnp.where(lo, t0, _swap_halves(t1))
        o_ref[0, pl.ds(2 * j + 1, n_rows, stride=GROUP_CH), :] = jnp.where(lo, _swap_halves(t0), t1)


def _ssm_apply(z, wtab, pst, gmat, dec, dvec, layer, rows_per_batch):
    G, R, _ = z.shape
    T = SSM_T
    tab = lambda shape: pl.BlockSpec((1, 1) + shape, lambda g: (layer, g, 0, 0))
    return pl.pallas_call(
        functools.partial(_ssm_kernel, rows_per_batch=rows_per_batch),
        grid=(G,),
        in_specs=[pl.BlockSpec((1, R, LANES), lambda g: (g, 0, 0)),
                  tab((GROUP_CH * GROUP_CH, LANES)), tab((GROUP_CH * T, 2 * LANES)),
                  tab((2 * LANES, GROUP_CH * T)), tab((1, 2 * LANES)), tab((1, GROUP_CH * T))],
        out_specs=pl.BlockSpec((1, R, LANES), lambda g: (g, 0, 0)),
        out_shape=jax.ShapeDtypeStruct(z.shape, F32),
        scratch_shapes=[pltpu.VMEM((GROUP_CH * T, GROUP_CH * T), BF16)],
        compiler_params=pltpu.CompilerParams(dimension_semantics=("arbitrary",),
                                             vmem_limit_bytes=VMEM_LIMIT),
        name="ssm_seq",
    )(z, wtab, pst, gmat, dec, dvec)


def _kv_kernel(mem_ref, w_ref, o_ref):
    o_ref[0] = _dot(mem_ref[0].astype(BF16), w_ref[...]).astype(o_ref.dtype)


def _kv_proj(mem, w_kv):
    B, M, D = mem.shape
    return pl.pallas_call(
        _kv_kernel,
        grid=(B,),
        in_specs=[pl.BlockSpec((1, M, D), lambda b: (b, 0, 0)),
                  pl.BlockSpec(w_kv.shape, lambda b: (0, 0))],
        out_specs=pl.BlockSpec((1, M, w_kv.shape[1]), lambda b: (b, 0, 0)),
        out_shape=jax.ShapeDtypeStruct((B, M, w_kv.shape[1]), BF16),
        name="kv_proj",
    )(mem, w_kv)


def _front_kernel(x_ref, wssm_t_ref, wuv_ref, wq_ref, wg_ref, bg_ref, lng_ref, lnb_ref,
                  ws_ref, bs_t_ref, kv_ref, wbr1_ref, wbr2_ref, z_ref, m_ref, *, n_heads):
    tn, d_model = x_ref.shape
    xb = x_ref[...].astype(BF16)

    u_t = lax.dot_general(wssm_t_ref[...], xb, NT_DIMS, preferred_element_type=F32)
    n_grp = u_t.shape[0] // GROUP_CH
    for a in range(tn // LANES):
        z_ref[:, a * GROUP_CH:(a + 1) * GROUP_CH, :] = (
            u_t[:, a * LANES:(a + 1) * LANES].reshape(n_grp, GROUP_CH, LANES))

    uv = jax.nn.gelu(_dot(xb, wuv_ref[...]))
    half = uv.shape[1] // 2
    u, v = uv[:, :half], uv[:, half:]
    vb = _layer_norm(v, lng_ref[...], lnb_ref[...]).astype(BF16)
    lane = _lane_iota((LANES, LANES))
    lo = lane < HALF
    s_rows = []
    for a in range(tn // LANES):
        va = vb[a * LANES:(a + 1) * LANES]
        tiles = []
        for j in range(half // LANES):
            rhs = va[:, j * LANES:(j + 1) * LANES]
            ev = _dot(ws_ref[2 * j], rhs)
            od = _dot(ws_ref[2 * j + 1], rhs)
            bias = jnp.where(lo, bs_t_ref[:, 2 * j:2 * j + 1], bs_t_ref[:, 2 * j + 1:2 * j + 2])
            tiles.append(jnp.where(lo, ev, od) + bias)
        s_rows.append(jnp.concatenate(tiles, axis=1))
    y_gmlp = u * jnp.concatenate(s_rows, axis=0)

    q = _dot(xb, wq_ref[...])
    hd = q.shape[1] // n_heads
    kv = kv_ref[0]
    outs = []
    for h in range(n_heads):
        qh = q[:, h * hd:(h + 1) * hd].astype(BF16)
        kh = kv[:, h * hd:(h + 1) * hd]
        vh = kv[:, q.shape[1] + h * hd:q.shape[1] + (h + 1) * hd]
        sc = lax.dot_general(qh, kh, NT_DIMS, preferred_element_type=F32) * (hd ** -0.5)
        sc = sc - jnp.max(sc, axis=-1, keepdims=True)
        p = jnp.exp(sc)
        p = p / jnp.sum(p, axis=-1, keepdims=True)
        outs.append(_dot(p.astype(BF16), vh))
    y_mem = jnp.concatenate(outs, axis=1)

    gates = jax.nn.sigmoid(_dot(xb, wg_ref[...]) + bg_ref[...])
    m_ref[...] = (gates[:, :d_model] * _dot(y_gmlp.astype(BF16), wbr1_ref[...])
                  + gates[:, d_model:] * _dot(y_mem.astype(BF16), wbr2_ref[...]))


def _front(x2, wssm_t, wuv, wq, wg12, bg12, lng, lnb, ws, bs_t, kv, wbr1, wbr2, *, tn, n_heads,
           tokens_per_batch):
    n, d = x2.shape
    n_grp = wssm_t.shape[0] // GROUP_CH
    rows = tn // LANES * GROUP_CH
    full = lambda arr: pl.BlockSpec(arr.shape, lambda i: (0,) * arr.ndim)
    tiles_per_batch = tokens_per_batch // tn
    return pl.pallas_call(
        functools.partial(_front_kernel, n_heads=n_heads),
        grid=(n // tn,),
        in_specs=[pl.BlockSpec((tn, d), lambda i: (i, 0)),
                  full(wssm_t), full(wuv), full(wq), full(wg12), full(bg12), full(lng), full(lnb),
                  full(ws), full(bs_t),
                  pl.BlockSpec((1,) + kv.shape[1:], lambda i: (i // tiles_per_batch, 0, 0)),
                  full(wbr1), full(wbr2)],
        out_specs=[pl.BlockSpec((n_grp, rows, LANES), lambda i: (0, i, 0)),
                   pl.BlockSpec((tn, d), lambda i: (i, 0))],
        out_shape=[jax.ShapeDtypeStruct((n_grp, n // LANES * GROUP_CH, LANES), F32),
                   jax.ShapeDtypeStruct((n, d), F32)],
        compiler_params=pltpu.CompilerParams(dimension_semantics=("arbitrary",),
                                             vmem_limit_bytes=VMEM_LIMIT),
        name="mixer_front",
    )(x2, wssm_t, wuv, wq, wg12, bg12, lng, lnb, ws, bs_t, kv, wbr1, wbr2)


def _route_t(lg, le):
    n_exp = float(le.shape[0])
    sub = lax.broadcasted_iota(jnp.int32, le.shape, 0).astype(F32)
    gmax = jnp.max(lg, axis=0, keepdims=True)
    denom = jnp.sum(jnp.exp(lg - gmax), axis=0, keepdims=True) * (1.0 / 8.0)
    pg_sel = 1.0 / denom
    first = jnp.min(jnp.where(lg == gmax, sub, n_exp), axis=0, keepdims=True)
    in_grp = jnp.floor(sub * 0.125) == jnp.floor(first * 0.125)
    neg = -jnp.inf
    l1 = jnp.where(in_grp, le, neg)
    m1 = jnp.max(l1, axis=0, keepdims=True)
    i1 = jnp.min(jnp.where(l1 == m1, sub, n_exp), axis=0, keepdims=True)
    l2 = jnp.where(sub == i1, neg, l1)
    m2 = jnp.max(l2, axis=0, keepdims=True)
    i2 = jnp.min(jnp.where(l2 == m2, sub, n_exp), axis=0, keepdims=True)
    e2 = jnp.exp(m2 - m1)
    p1 = 1.0 / (1.0 + e2)
    p2 = e2 / (1.0 + e2)
    return i1, i2, pg_sel * p1, pg_sel * p2


def _back_kernel(x_ref, m12_ref, y_ref, wglu_t_ref, bglu_ref, wbr0_ref, wg0_ref, bg0_ref, wout_ref,
                 lng_ref, lnb_ref, wrg_ref, brg_ref, wre_ref, bre_ref, x1_ref, route_ref, *, alpha):
    tn = x_ref.shape[0]
    x = x_ref[...]
    xb = x.astype(BF16)
    n_grp = y_ref.shape[0]
    cols = []
    for a in range(tn // LANES):
        y_t = y_ref[:, a * GROUP_CH:(a + 1) * GROUP_CH, :].reshape(n_grp * GROUP_CH, LANES)
        y_t = jax.nn.gelu(y_t)
        gate = jax.nn.sigmoid(_dot(wglu_t_ref[...], y_t.astype(BF16)) + bglu_ref[...])
        cols.append((y_t * gate).astype(BF16))
    ys_t = jnp.concatenate(cols, axis=1)
    br0 = lax.dot_general(ys_t, wbr0_ref[...], TN_DIMS, preferred_element_type=F32)
    g0 = jax.nn.sigmoid(_dot(xb, wg0_ref[...]) + bg0_ref[...])
    merged = m12_ref[...] + g0 * br0
    h = _dot(merged.astype(BF16), wout_ref[...])
    x1 = _layer_norm(alpha * x + h, lng_ref[...], lnb_ref[...])
    x1_ref[...] = x1
    hi = lax.Precision.HIGHEST
    lg = lax.dot_general(wrg_ref[...], x1, NT_DIMS, preferred_element_type=F32, precision=hi) + brg_ref[...]
    le = lax.dot_general(wre_ref[...], x1, NT_DIMS, preferred_element_type=F32, precision=hi) + bre_ref[...]
    i1, i2, w1, w2 = _route_t(lg, le)
    route_ref[...] = jnp.concatenate([i1, i2, w1, w2, jnp.zeros((4, tn), F32)], axis=0)


def _back(x2, m12, y, wglu_t, bglu, wbr0, wg0, bg0, wout, lng, lnb, wrg, brg, wre, bre, *, tn, alpha):
    n, d = x2.shape
    n_grp = y.shape[0]
    rows = tn // LANES * GROUP_CH
    full = lambda arr: pl.BlockSpec(arr.shape, lambda i: (0,) * arr.ndim)
    return pl.pallas_call(
        functools.partial(_back_kernel, alpha=alpha),
        grid=(n // tn,),
        in_specs=[pl.BlockSpec((tn, d), lambda i: (i, 0)),
                  pl.BlockSpec((tn, d), lambda i: (i, 0)),
                  pl.BlockSpec((n_grp, rows, LANES), lambda i: (0, i, 0)),
                  full(wglu_t), full(bglu), full(wbr0), full(wg0), full(bg0), full(wout),
                  full(lng), full(lnb), full(wrg), full(brg), full(wre), full(bre)],
        out_specs=[pl.BlockSpec((tn, d), lambda i: (i, 0)),
                   pl.BlockSpec((8, tn), lambda i: (0, i))],
        out_shape=[jax.ShapeDtypeStruct((n, d), F32),
                   jax.ShapeDtypeStruct((8, n), F32)],
        compiler_params=pltpu.CompilerParams(dimension_semantics=("arbitrary",),
                                             vmem_limit_bytes=VMEM_LIMIT),
        name="mixer_back",
    )(x2, m12, y, wglu_t, bglu, wbr0, wg0, bg0, wout, lng, lnb, wrg, brg, wre, bre)


MOE_TILE = 256
MAX_TILES_LANES = 256


def _pos_kernel(route_ref, pos_ref, meta_ref, cnt_ref, offs_ref, carry_ref, *, n_exp):
    phase = pl.program_id(0)
    i = pl.program_id(1)
    tp = route_ref.shape[1]
    sub = lax.broadcasted_iota(jnp.int32, (n_exp, tp), 0).astype(F32)
    i1 = route_ref[0:1, :]
    i2 = route_ref[1:2, :]
    sel = jnp.where((sub == i1) | (sub == i2), 1.0, 0.0)
    tile_cnt = jnp.sum(sel, axis=1, keepdims=True)

    @pl.when((phase == 0) & (i == 0))
    def _():
        cnt_ref[...] = jnp.zeros_like(cnt_ref)

    @pl.when(phase == 0)
    def _():
        cnt_ref[...] += jnp.broadcast_to(tile_cnt, cnt_ref.shape)

    @pl.when((phase == 1) & (i == 0))
    def _():
        cnt = cnt_ref[...]
        padded = jnp.ceil(cnt * (1.0 / MOE_TILE)) * float(MOE_TILE)
        sub_e = lax.broadcasted_iota(jnp.int32, cnt.shape, 0)
        lane_e = lax.broadcasted_iota(jnp.int32, cnt.shape, 1)
        row = jnp.sum(jnp.where(sub_e == lane_e, padded, 0.0), axis=0, keepdims=True)
        offs = jnp.sum(jnp.where(lane_e < sub_e, row, 0.0), axis=1, keepdims=True)
        offs_ref[...] = jnp.broadcast_to(offs, offs_ref.shape)
        carry_ref[...] = jnp.zeros_like(carry_ref)
        total = jnp.sum(jnp.where(lane_e < n_exp, row, 0.0), axis=1, keepdims=True)[0:1]
        t = lax.broadcasted_iota(jnp.int32, (n_exp, MAX_TILES_LANES), 1).astype(F32) * float(MOE_TILE)
        texp = jnp.sum(jnp.where(offs <= t, 1.0, 0.0), axis=0, keepdims=True) - 1.0
        valid = jnp.where(t[0:1] < total, 1.0, 0.0)
        meta = jnp.concatenate([texp, valid, jnp.zeros((6, MAX_TILES_LANES), F32)], axis=0)
        meta_ref[...] = meta.astype(jnp.int32)

    @pl.when(phase == 1)
    def _():
        r = lax.broadcasted_iota(jnp.int32, (tp, tp), 0)
        c = lax.broadcasted_iota(jnp.int32, (tp, tp), 1)
        upper = jnp.where(r < c, 1.0, 0.0).astype(BF16)
        rank = _dot(sel.astype(BF16), upper)
        slot = offs_ref[:, 0:1] + carry_ref[:, 0:1] + rank
        pos_a = jnp.sum(jnp.where(sub == i1, slot, 0.0), axis=0, keepdims=True)
        pos_b = jnp.sum(jnp.where(sub == i2, slot, 0.0), axis=0, keepdims=True)
        pos = jnp.concatenate([pos_a, pos_b, jnp.zeros((6, tp), F32)], axis=0)
        pos_ref[...] = pos.astype(jnp.int32)
        carry_ref[...] += jnp.broadcast_to(tile_cnt, carry_ref.shape)


def _positions(route, *, n_exp, tp):
    n = route.shape[1]
    return pl.pallas_call(
        functools.partial(_pos_kernel, n_exp=n_exp),
        grid=(2, n // tp),
        in_specs=[pl.BlockSpec((8, tp), lambda p, i: (0, i))],
        out_specs=[pl.BlockSpec((8, tp), lambda p, i: (0, i * p)),
                   pl.BlockSpec((8, MAX_TILES_LANES), lambda p, i: (0, 0))],
        out_shape=[jax.ShapeDtypeStruct((8, n), jnp.int32),
                   jax.ShapeDtypeStruct((8, MAX_TILES_LANES), jnp.int32)],
        scratch_shapes=[pltpu.VMEM((n_exp, LANES), F32)] * 3,
        compiler_params=pltpu.CompilerParams(dimension_semantics=("arbitrary", "arbitrary")),
        name="moe_positions",
    )(route)


SC_CORES = 2
SC_SUBCORES = 16
SC_LANES = 16
SC_GATHER_ROWS = 32


def _sc_mesh():
    return plsc.VectorSubcoreMesh(core_axis_name="c", subcore_axis_name="s",
                                  num_cores=SC_CORES, num_subcores=SC_SUBCORES)


def _sc_invert(pos_a, pos_b, n_slots):
    n = pos_a.shape[0]

    def body(pa_hbm, pb_hbm, src_hbm, pa_v, pb_v, src_v):
        wid = lax.axis_index("s") * SC_CORES + lax.axis_index("c")

        @pl.when(wid == 0)
        def _():
            pltpu.sync_copy(pa_hbm, pa_v)
            pltpu.sync_copy(pb_hbm, pb_v)
            zero = jnp.zeros((SC_LANES,), jnp.int32)

            @pl.loop(0, n_slots // SC_LANES)
            def _(j):
                src_v[pl.ds(pl.multiple_of(j * SC_LANES, SC_LANES), SC_LANES)] = zero

            lane = lax.iota(jnp.int32, SC_LANES)

            @pl.loop(0, n // SC_LANES)
            def _(j):
                off = pl.multiple_of(j * SC_LANES, SC_LANES)
                tok = lane + off
                plsc.store_scatter(src_v, [pa_v[pl.ds(off, SC_LANES)]], tok)
                plsc.store_scatter(src_v, [pb_v[pl.ds(off, SC_LANES)]], tok)

            pltpu.sync_copy(src_v, src_hbm)

    return pl.kernel(
        body, out_type=jax.ShapeDtypeStruct((n_slots,), jnp.int32), mesh=_sc_mesh(),
        scratch_types=[pltpu.VMEM((n,), jnp.int32), pltpu.VMEM((n,), jnp.int32),
                       pltpu.VMEM((n_slots,), jnp.int32)],
        compiler_params=pltpu.CompilerParams(needs_layout_passes=False),
        name="moe_invert",
    )(pos_a, pos_b)


def _sc_gather(table, idx):
    m = idx.shape[0]
    d = table.shape[1]
    n_workers = SC_CORES * SC_SUBCORES
    per_w = m // n_workers
    rows = SC_GATHER_ROWS

    def body(table_hbm, idx_hbm, out_hbm, idx_v, rows_v, sem):
        wid = lax.axis_index("s") * SC_CORES + lax.axis_index("c")
        base = pl.multiple_of(wid * per_w, rows)
        pltpu.sync_copy(idx_hbm.at[pl.ds(base, per_w)], idx_v)

        @pl.loop(0, per_w // rows)
        def _(j):
            off = pl.multiple_of(j * rows, rows)
            pltpu.async_copy(table_hbm.at[idx_v.at[pl.ds(off, rows)]], rows_v, sem).wait()
            pltpu.sync_copy(rows_v, out_hbm.at[pl.ds(base + off, rows)])

    return pl.kernel(
        body, out_type=jax.ShapeDtypeStruct((m, d), table.dtype), mesh=_sc_mesh(),
        scratch_types=[pltpu.VMEM((per_w,), jnp.int32), pltpu.VMEM((rows, d), table.dtype),
                       pltpu.SemaphoreType.DMA],
        name="moe_gather",
    )(table, idx)


def _expert_kernel(meta_ref, xs_ref, wg_ref, wu_ref, wd_ref, ys_ref):
    t = pl.program_id(0)

    @pl.when(meta_ref[1, t] == 1)
    def _():
        xb = xs_ref[...].astype(BF16)
        h = jax.nn.silu(_dot(xb, wg_ref[0])) * _dot(xb, wu_ref[0])
        ys_ref[...] = _dot(h.astype(BF16), wd_ref[0])

    @pl.when(meta_ref[1, t] != 1)
    def _():
        ys_ref[...] = jnp.zeros_like(ys_ref)


def _experts(meta, xs, wg, wu, wd):
    n_slots, d = xs.shape
    _, _, f = wg.shape
    grid_spec = pltpu.PrefetchScalarGridSpec(
        num_scalar_prefetch=1,
        grid=(n_slots // MOE_TILE,),
        in_specs=[pl.BlockSpec((MOE_TILE, d), lambda t, meta: (t, 0)),
                  pl.BlockSpec((1, d, f), lambda t, meta: (meta[0, t], 0, 0)),
                  pl.BlockSpec((1, d, f), lambda t, meta: (meta[0, t], 0, 0)),
                  pl.BlockSpec((1, f, d), lambda t, meta: (meta[0, t], 0, 0))],
        out_specs=pl.BlockSpec((MOE_TILE, d), lambda t, meta: (t, 0)),
    )
    return pl.pallas_call(
        _expert_kernel, grid_spec=grid_spec,
        out_shape=jax.ShapeDtypeStruct((n_slots, d), F32),
        compiler_params=pltpu.CompilerParams(dimension_semantics=("arbitrary",),
                                             vmem_limit_bytes=VMEM_LIMIT),
        name="moe_experts",
    )(meta, xs, wg, wu, wd)


def _combine_kernel(x_ref, ya_ref, yb_ref, route_ref, lng_ref, lnb_ref, o_ref, *, alpha):
    w = route_ref[...].T
    moe = w[:, 2:3] * ya_ref[...] + w[:, 3:4] * yb_ref[...]
    o_ref[...] = _layer_norm(alpha * x_ref[...] + moe, lng_ref[...], lnb_ref[...])


def _combine(x1, ya, yb, route, lng, lnb, *, tn, alpha):
    n, d = x1.shape
    full = lambda arr: pl.BlockSpec(arr.shape, lambda i: (0,) * arr.ndim)
    row = pl.BlockSpec((tn, d), lambda i: (i, 0))
    return pl.pallas_call(
        functools.partial(_combine_kernel, alpha=alpha),
        grid=(n // tn,),
        in_specs=[row, row, row, pl.BlockSpec((8, tn), lambda i: (0, i)), full(lng), full(lnb)],
        out_specs=row,
        out_shape=jax.ShapeDtypeStruct((n, d), F32),
        compiler_params=pltpu.CompilerParams(dimension_semantics=("arbitrary",)),
        name="moe_combine",
    )(x1, ya, yb, route, lng, lnb)


def _moe(x1, route, wg, wu, wd, lng, lnb, *, alpha):
    n, d = x1.shape
    n_exp = wg.shape[0]
    n_slots = (2 * n // MOE_TILE + n_exp) * MOE_TILE
    assert n_slots // MOE_TILE <= MAX_TILES_LANES
    pos, meta = _positions(route, n_exp=n_exp, tp=min(512, n))
    src = _sc_invert(pos[0], pos[1], n_slots)
    xs = _sc_gather(x1, src)
    ys = _experts(meta, xs, wg, wu, wd)
    ya = _sc_gather(ys, pos[0])
    yb = _sc_gather(ys, pos[1])
    return _combine(x1, ya, yb, route, lng, lnb, tn=min(512, n), alpha=alpha)


def _dup(v):
    return jnp.concatenate([v, v], axis=-1)


def kernel(x, mem, w_in, w_gate, b_gate, ssm_lam_re, ssm_lam_im, ssm_log_step, ssm_b_re, ssm_b_im, ssm_c_re, ssm_c_im, ssm_d, w_glu, b_glu, gmlp_ln_g, gmlp_ln_b, w_spatial, b_spatial, w_kv, w_br, w_out, ln1_g, ln1_b, w_router_g, b_router_g, w_router_e, b_router_e, w_exp_gate, w_exp_up, w_exp_down, ln2_g, ln2_b):
    bsz, seq, d = x.shape
    depth = w_in.shape[0]
    n = bsz * seq
    n_grp = ssm_lam_re.shape[2]
    ssm_w = n_grp * GROUP_CH
    gmlp_w = gmlp_ln_g.shape[1]
    n_heads = 4
    n_moe_grp, exp_per_grp = b_router_e.shape[1], b_router_e.shape[2]
    alpha = (2.0 * depth) ** 0.25
    tn = 256

    ls = jnp.broadcast_to(ssm_log_step[..., None], ssm_lam_re.shape)
    six = [ssm_lam_re[:, 0], ssm_lam_im[:, 0], ssm_lam_re[:, 1], ssm_lam_im[:, 1], ls[:, 0], ls[:, 1]]
    prow = jnp.stack([_dup(v) for v in six] + [jnp.zeros_like(_dup(six[0]))] * 2, axis=2)
    pcol = jnp.stack(six + [jnp.zeros_like(six[0])] * 2, axis=-1)
    bt2 = jnp.concatenate([jnp.swapaxes(ssm_b_re, 2, 3), jnp.swapaxes(ssm_b_im, 2, 3)], axis=-1)
    c2 = jnp.concatenate([ssm_c_re, ssm_c_im], axis=-1)
    ct2 = jnp.concatenate([jnp.swapaxes(ssm_c_re, 2, 3), jnp.swapaxes(ssm_c_im, 2, 3)], axis=-1)
    wtab, pst, gmat, dec = _ssm_tables(prow, pcol, bt2, c2, ct2)
    dvec = jnp.repeat(ssm_d.reshape(depth, n_grp, 1, GROUP_CH), SSM_T, axis=-1)

    x2 = x.reshape(n, d)
    for l in range(depth):
        wl = w_in[l].astype(BF16)
        wssm_t = wl[:, :ssm_w].T
        wuv = wl[:, ssm_w:ssm_w + 2 * gmlp_w]
        wq = wl[:, ssm_w + 2 * gmlp_w:]
        wg = w_gate[l].astype(BF16)
        bg = b_gate[l].reshape(1, -1)
        kv = _kv_proj(mem, w_kv[l].astype(BF16))
        z, m12 = _front(
            x2, wssm_t, wuv, wq, wg[:, d:], bg[:, d:],
            gmlp_ln_g[l].reshape(1, -1), gmlp_ln_b[l].reshape(1, -1),
            w_spatial[l].astype(BF16), b_spatial[l].T, kv,
            w_br[l, 1].astype(BF16), w_br[l, 2].astype(BF16),
            tn=tn, n_heads=n_heads, tokens_per_batch=seq)
        y = _ssm_apply(z, wtab, pst, gmat, dec, dvec, l, seq // LANES)
        wrg_t = jnp.repeat(w_router_g[l].T, exp_per_grp, axis=0)
        brg_t = jnp.repeat(b_router_g[l], exp_per_grp).reshape(-1, 1)
        wre_t = jnp.transpose(w_router_e[l], (0, 2, 1)).reshape(n_moe_grp * exp_per_grp, d)
        bre_t = b_router_e[l].reshape(-1, 1)
        x1, route = _back(
            x2, m12, y, w_glu[l].T.astype(BF16), b_glu[l].reshape(-1, 1), w_br[l, 0].astype(BF16),
            wg[:, :d], bg[:, :d], w_out[l].astype(BF16),
            ln1_g[l].reshape(1, -1), ln1_b[l].reshape(1, -1), wrg_t, brg_t, wre_t, bre_t,
            tn=tn, alpha=alpha)
        x2 = _moe(x1, route, w_exp_gate[l].astype(BF16), w_exp_up[l].astype(BF16),
                  w_exp_down[l].astype(BF16), ln2_g[l].reshape(1, -1), ln2_b[l].reshape(1, -1),
                  alpha=alpha)
    return x2.reshape(bsz, seq, d)
```

```python
import functools
import math

import jax
import jax.numpy as jnp
from jax import lax
from jax.experimental import pallas as pl
from jax.experimental.pallas import tpu as pltpu
from jax.experimental.pallas import tpu_sc as plsc

F32 = jnp.float32
BF16 = jnp.bfloat16

LANES = 128
HALF = LANES // 2
SSM_T = HALF
GROUP_CH = 16
N_STATE = 64
LN_EPS = 1e-5
VMEM_LIMIT = 48 * 1024 * 1024

NT_DIMS = (((1,), (1,)), ((), ()))
TN_DIMS = (((0,), (0,)), ((), ()))


def _dot(a, b):
    return jnp.dot(a, b, preferred_element_type=F32)


def _dot_hi(a, b):
    return jnp.dot(a, b, preferred_element_type=F32, precision=lax.Precision.HIGHEST)


def _layer_norm(x, g, b):
    mu = jnp.mean(x, axis=-1, keepdims=True)
    xc = x - mu
    var = jnp.mean(xc * xc, axis=-1, keepdims=True)
    return xc * lax.rsqrt(var + LN_EPS) * g + b


def _lane_iota(shape):
    return lax.broadcasted_iota(jnp.int32, shape, len(shape) - 1)


def _swap_halves(x):
    return pltpu.roll(x, HALF, 1)


def _ssm_tables_kernel(prow_ref, pcol_ref, bt_ref, c_ref, ct_ref,
                       wtab_ref, pst_ref, g_ref, dec_ref):
    T = SSM_T
    prow = prow_ref[0, 0]
    pcol = pcol_ref[0, 0]
    bt2 = bt_ref[0, 0]
    c2 = c_ref[0, 0]
    ct = ct_ref[0, 0]
    lane1 = _lane_iota((1, LANES))
    lo1 = lane1 < HALF
    sgn1 = jnp.where(lo1, -1.0, 1.0).astype(F32)

    lane_bc = _lane_iota((GROUP_CH, LANES))
    lo_bc = lane_bc < HALF
    c2s = jnp.where(lo_bc, -1.0, 1.0) * _swap_halves(c2)

    bbar = []
    a_row = []
    th_row = []
    for d in range(2):
        lre = prow[2 * d:2 * d + 1]
        lim = prow[2 * d + 1:2 * d + 2]
        dt = jnp.exp(prow[4 + d:5 + d])
        a = lre * dt
        th = lim * dt
        er = jnp.exp(a)
        lb_re = er * jnp.cos(th)
        lb_im = er * jnp.sin(th)
        num_re = lb_re - 1.0
        den = lre * lre + lim * lim
        coef_re = (num_re * lre + lb_im * lim) / den
        coef_im = (lb_im * lre - num_re * lim) / den
        bbar.append(coef_re * bt2 + (sgn1 * coef_im) * _swap_halves(bt2))
        a_row.append(a)
        th_row.append(th)

    s_col = lax.broadcasted_iota(jnp.int32, (T, 1), 0).astype(F32)
    for d in range(2):
        k = (T - 1.0) - s_col if d == 0 else s_col
        e = jnp.exp(k * a_row[d])
        ang = k * th_row[d]
        pr = e * jnp.cos(ang)
        pi = e * jnp.sin(ang)
        bsw = sgn1 * _swap_halves(bbar[d])
        for hp in range(GROUP_CH):
            blk = pr * bbar[d][hp:hp + 1] + pi * bsw[hp:hp + 1]
            pst_ref[0, 0, hp * T:(hp + 1) * T, d * LANES:(d + 1) * LANES] = blk.astype(pst_ref.dtype)

    for d in range(2):
        e = jnp.exp(float(T) * a_row[d])
        ang = float(T) * th_row[d]
        dec_ref[0, 0, :, d * LANES:(d + 1) * LANES] = e * jnp.where(lo1, jnp.cos(ang), jnp.sin(ang))

    lane_p = _lane_iota((N_STATE, LANES))
    lo_p = lane_p < HALF
    t_lane = jnp.where(lo_p, lane_p, lane_p - HALF).astype(F32)
    pw = []
    for d in range(2):
        dtc = jnp.exp(pcol[:, 4 + d:5 + d])
        a_c = pcol[:, 2 * d:2 * d + 1] * dtc
        th_c = pcol[:, 2 * d + 1:2 * d + 2] * dtc
        pw.append((a_c, th_c))
    for d in range(2):
        a_c, th_c = pw[d]
        k = t_lane + 1.0 if d == 0 else float(T) - t_lane
        e = jnp.exp(k * a_c)
        pr = e * jnp.cos(k * th_c)
        pi = e * jnp.sin(k * th_c)
        for j in range(GROUP_CH // 2):
            cre = jnp.where(lo_p, ct[:, 2 * j:2 * j + 1], ct[:, 2 * j + 1:2 * j + 2])
            cim = jnp.where(lo_p, ct[:, GROUP_CH + 2 * j:GROUP_CH + 2 * j + 1],
                            ct[:, GROUP_CH + 2 * j + 1:GROUP_CH + 2 * j + 2])
            g_re = cre * pr - cim * pi
            g_im = -(cre * pi + cim * pr)
            g_ref[0, 0, d * LANES:d * LANES + N_STATE, j * LANES:(j + 1) * LANES] = g_re.astype(g_ref.dtype)
            g_ref[0, 0, d * LANES + N_STATE:(d + 1) * LANES, j * LANES:(j + 1) * LANES] = g_im.astype(g_ref.dtype)

    res = []
    kb0 = None
    for d in range(2):
        a_c, th_c = pw[d]
        if d == 0:
            k = jnp.maximum(lane_p - HALF, 0).astype(F32)
        else:
            k = jnp.maximum(HALF - lane_p, 0).astype(F32)
        e = jnp.exp(k * a_c)
        rhs = jnp.concatenate([e * jnp.cos(k * th_c), -(e * jnp.sin(k * th_c))], axis=0)
        bsw = _swap_halves(bbar[d])
        b_re = jnp.where(lo_bc, bbar[d], bsw)
        b_im = jnp.where(lo_bc, bsw, bbar[d])
        bc = (b_re[:, None, :] * c2[None, :, :] + b_im[:, None, :] * c2s[None, :, :])
        bc = bc.reshape(GROUP_CH * GROUP_CH, LANES)
        res.append(_dot_hi(bc, rhs))
        if d == 1:
            lane_bcf = _lane_iota(bc.shape)
            kb0 = jnp.sum(jnp.where(lane_bcf < HALF, bc, 0.0), axis=1, keepdims=True)
    lane_w = _lane_iota(res[0].shape)
    wtab = jnp.where(lane_w >= HALF, res[0], res[1])
    wtab_ref[0, 0] = wtab + jnp.where(lane_w == HALF, kb0, 0.0)


def _ssm_tables(prow, pcol, bt2, c2, ct2):
    L, G = prow.shape[0], prow.shape[1]
    T = SSM_T
    blk = lambda shape: pl.BlockSpec((1, 1) + shape, lambda l, g: (l, g, 0, 0))
    return pl.pallas_call(
        _ssm_tables_kernel,
        grid=(L, G),
        in_specs=[blk((8, LANES)), blk((N_STATE, 8)), blk((GROUP_CH, LANES)),
                  blk((GROUP_CH, LANES)), blk((N_STATE, 2 * GROUP_CH))],
        out_specs=[blk((GROUP_CH * GROUP_CH, LANES)), blk((GROUP_CH * T, 2 * LANES)),
                   blk((2 * LANES, GROUP_CH * T)), blk((1, 2 * LANES))],
        out_shape=[jax.ShapeDtypeStruct((L, G, GROUP_CH * GROUP_CH, LANES), F32),
                   jax.ShapeDtypeStruct((L, G, GROUP_CH * T, 2 * LANES), BF16),
                   jax.ShapeDtypeStruct((L, G, 2 * LANES, GROUP_CH * T), BF16),
                   jax.ShapeDtypeStruct((L, G, 1, 2 * LANES), F32)],
        name="ssm_tables",
    )(prow, pcol, bt2, c2, ct2)


def _cmul_packed(x, d_re, d_sw):
    return x * d_re + _swap_halves(x) * d_sw


def _ssm_kernel(z_ref, wtab_ref, pst_ref, g_ref, dec_ref, dvec_ref, o_ref, m_ref, *, rows_per_batch):
    T = SSM_T
    n_rows = z_ref.shape[1] // GROUP_CH
    lane_t = _lane_iota((T, LANES))
    lo_t = lane_t < HALF

    def build(hp, carry):
        r0 = pl.multiple_of(hp * T, T)
        for j in range(GROUP_CH // 2):
            wa = jnp.broadcast_to(wtab_ref[0, 0, pl.ds(hp * GROUP_CH + 2 * j, 1), :], (T, LANES))
            wb = jnp.broadcast_to(wtab_ref[0, 0, pl.ds(hp * GROUP_CH + 2 * j + 1, 1), :], (T, LANES))
            ra = pltpu.roll(wa, HALF, 1, stride=1, stride_axis=0)
            rb = pltpu.roll(wb, 0, 1, stride=1, stride_axis=0)
            m_ref[pl.ds(r0, T), j * LANES:(j + 1) * LANES] = jnp.where(lo_t, ra, rb).astype(m_ref.dtype)
        return carry

    lax.fori_loop(0, GROUP_CH, build, 0)

    lane = _lane_iota((n_rows, LANES))
    lo = lane < HALF
    a0, a1 = [], []
    for j in range(GROUP_CH // 2):
        pe = z_ref[0, pl.ds(2 * j, n_rows, stride=GROUP_CH), :]
        po = z_ref[0, pl.ds(2 * j + 1, n_rows, stride=GROUP_CH), :]
        a0.append(jnp.where(lo, pe, _swap_halves(po)))
        a1.append(jnp.where(lo, _swap_halves(pe), po))
    a0 = jnp.concatenate(a0, axis=1)
    a1 = jnp.concatenate(a1, axis=1)
    a_f32 = jnp.concatenate([a0, a1], axis=0)
    a_bf = a_f32.astype(BF16)

    out = _dot(a_bf, m_ref[...]) + dvec_ref[0, 0] * a_f32
    s = _dot(a_bf, pst_ref[0, 0])
    sf0, sb0 = s[:n_rows, :LANES], s[:n_rows, LANES:]
    sf1, sb1 = s[n_rows:, :LANES], s[n_rows:, LANES:]

    sgn = jnp.where(_lane_iota((1, LANES)) < HALF, -1.0, 1.0).astype(F32)
    lo1 = _lane_iota((1, LANES)) < HALF

    def split(p):
        sw = _swap_halves(p)
        return jnp.where(lo1, p, sw), sgn * jnp.where(lo1, sw, p)

    def square(p):
        d_re, d_sw = split(p)
        return _cmul_packed(p, d_re, d_sw)

    dec = dec_ref[0, 0]
    dec_f, dec_b = dec[:, :LANES], dec[:, LANES:]
    df_re, df_sw = split(dec_f)
    db_re, db_sw = split(dec_b)

    row = lax.broadcasted_iota(jnp.int32, (n_rows, LANES), 0)
    rib = row % rows_per_batch

    ef = _cmul_packed(sf0, df_re, df_sw) + sf1
    eb = sb0 + _cmul_packed(sb1, db_re, db_sw)
    pf, pb = square(dec_f), square(dec_b)
    step = 1
    while step < rows_per_batch:
        pf_re, pf_sw = split(pf)
        pb_re, pb_sw = split(pb)
        shf = jnp.where(rib >= step, pltpu.roll(ef, step, 0), 0.0)
        ef = ef + _cmul_packed(shf, pf_re, pf_sw)
        shb = jnp.where(rib < rows_per_batch - step, pltpu.roll(eb, n_rows - step, 0), 0.0)
        eb = eb + _cmul_packed(shb, pb_re, pb_sw)
        pf, pb = square(pf), square(pb)
        step *= 2
    hf0 = jnp.where(rib >= 1, pltpu.roll(ef, 1, 0), 0.0)
    hf1 = _cmul_packed(hf0, df_re, df_sw) + sf0
    hb1 = jnp.where(rib < rows_per_batch - 1, pltpu.roll(eb, n_rows - 1, 0), 0.0)
    hb0 = sb1 + _cmul_packed(hb1, db_re, db_sw)
    h_in = jnp.concatenate([jnp.concatenate([hf0, hb0], axis=1),
                            jnp.concatenate([hf1, hb1], axis=1)], axis=0)
    out = out + _dot(h_in.astype(BF16), g_ref[0, 0])

    o0, o1 = out[:n_rows], out[n_rows:]
    for j in range(GROUP_CH // 2):
        t0 = o0[:, j * LANES:(j + 1) * LANES]
        t1 = o1[:, j * LANES:(j + 1) * LANES]
        o_ref[0, pl.ds(2 * j, n_rows, stride=GROUP_CH), :] = jnp.where(lo, t0, _swap_halves(t1))
        o_ref[0, pl.ds(2 * j + 1, n_rows, stride=GROUP_CH), :] = jnp.where(lo, _swap_halves(t0), t1)


def _ssm_apply(z, wtab, pst, gmat, dec, dvec, layer, rows_per_batch):
    G, R, _ = z.shape
    T = SSM_T
    tab = lambda shape: pl.BlockSpec((1, 1) + shape, lambda g: (layer, g, 0, 0))
    return pl.pallas_call(
        functools.partial(_ssm_kernel, rows_per_batch=rows_per_batch),
        grid=(G,),
        in_specs=[pl.BlockSpec((1, R, LANES), lambda g: (g, 0, 0)),
                  tab((GROUP_CH * GROUP_CH, LANES)), tab((GROUP_CH * T, 2 * LANES)),
                  tab((2 * LANES, GROUP_CH * T)), tab((1, 2 * LANES)), tab((1, GROUP_CH * T))],
        out_specs=pl.BlockSpec((1, R, LANES), lambda g: (g, 0, 0)),
        out_shape=jax.ShapeDtypeStruct(z.shape, F32),
        scratch_shapes=[pltpu.VMEM((GROUP_CH * T, GROUP_CH * T), BF16)],
        compiler_params=pltpu.CompilerParams(dimension_semantics=("arbitrary",),
                                             vmem_limit_bytes=VMEM_LIMIT),
        name="ssm_seq",
    )(z, wtab, pst, gmat, dec, dvec)


def _kv_kernel(mem_ref, w_ref, o_ref):
    o_ref[0] = _dot(mem_ref[0].astype(BF16), w_ref[0]).astype(o_ref.dtype)


def _kv_proj(mem, w_kv, layer):
    B, M, D = mem.shape
    width = w_kv.shape[2]
    return pl.pallas_call(
        _kv_kernel,
        grid=(B,),
        in_specs=[pl.BlockSpec((1, M, D), lambda b: (b, 0, 0)),
                  pl.BlockSpec((1, D, width), lambda b: (layer, 0, 0))],
        out_specs=pl.BlockSpec((1, M, width), lambda b: (b, 0, 0)),
        out_shape=jax.ShapeDtypeStruct((B, M, width), BF16),
        name="kv_proj",
    )(mem, w_kv)


def _front_kernel(x_ref, wssm_t_ref, wu_ref, wv_ref, wq_ref, wg1_ref, wg2_ref, bg1_ref, bg2_ref,
                  lng_ref, lnb_ref, ws_ref, bs_t_ref, kv_ref, wbr1_ref, wbr2_ref, z_ref, m_ref, *, n_heads):
    tn, d_model = x_ref.shape
    xb = x_ref[...].astype(BF16)

    u_t = lax.dot_general(wssm_t_ref[0], xb, NT_DIMS, preferred_element_type=F32)
    n_grp = u_t.shape[0] // GROUP_CH
    for a in range(tn // LANES):
        z_ref[:, a * GROUP_CH:(a + 1) * GROUP_CH, :] = (
            u_t[:, a * LANES:(a + 1) * LANES].reshape(n_grp, GROUP_CH, LANES))

    u = jax.nn.gelu(_dot(xb, wu_ref[0]))
    v = jax.nn.gelu(_dot(xb, wv_ref[0]))
    half = v.shape[1]
    vb = _layer_norm(v, lng_ref[0], lnb_ref[0]).astype(BF16)
    lane = _lane_iota((LANES, LANES))
    lo = lane < HALF
    bs_t = bs_t_ref[0]
    s_rows = []
    for a in range(tn // LANES):
        va = vb[a * LANES:(a + 1) * LANES]
        tiles = []
        for j in range(half // LANES):
            rhs = va[:, j * LANES:(j + 1) * LANES]
            ev = _dot(ws_ref[0, 2 * j], rhs)
            od = _dot(ws_ref[0, 2 * j + 1], rhs)
            bias = jnp.where(lo, bs_t[:, 2 * j:2 * j + 1], bs_t[:, 2 * j + 1:2 * j + 2])
            tiles.append(jnp.where(lo, ev, od) + bias)
        s_rows.append(jnp.concatenate(tiles, axis=1))
    y_gmlp = u * jnp.concatenate(s_rows, axis=0)

    q = _dot(xb, wq_ref[0])
    hd = q.shape[1] // n_heads
    kv = kv_ref[0]
    outs = []
    for h in range(n_heads):
        qh = q[:, h * hd:(h + 1) * hd].astype(BF16)
        kh = kv[:, h * hd:(h + 1) * hd]
        vh = kv[:, q.shape[1] + h * hd:q.shape[1] + (h + 1) * hd]
        sc = lax.dot_general(qh, kh, NT_DIMS, preferred_element_type=F32) * (hd ** -0.5)
        sc = sc - jnp.max(sc, axis=-1, keepdims=True)
        p = jnp.exp(sc)
        p = p / jnp.sum(p, axis=-1, keepdims=True)
        outs.append(_dot(p.astype(BF16), vh))
    y_mem = jnp.concatenate(outs, axis=1)

    g1 = jax.nn.sigmoid(_dot(xb, wg1_ref[0]) + bg1_ref[0])
    g2 = jax.nn.sigmoid(_dot(xb, wg2_ref[0]) + bg2_ref[0])
    m_ref[...] = (g1 * _dot(y_gmlp.astype(BF16), wbr1_ref[0, 0])
                  + g2 * _dot(y_mem.astype(BF16), wbr2_ref[0, 0]))


def _layer_block(arr, layer, col_block=None, col=0, branch=None):
    if branch is not None:
        return pl.BlockSpec((1, 1) + arr.shape[2:], lambda *_: (layer, branch, 0, 0))
    shape = arr.shape[1:]
    if col_block is not None:
        shape = shape[:-1] + (col_block,)
    nd = len(shape)
    return pl.BlockSpec((1,) + shape, lambda *_: (layer,) + (0,) * (nd - 1) + (col,))


def _front(x2, wssm_t, w_in, w_gate, b_gate, lng, lnb, ws, bs_t, kv, w_br, *, layer, tn, n_heads,
           tokens_per_batch):
    n, d = x2.shape
    n_grp = wssm_t.shape[1] // GROUP_CH
    width = wssm_t.shape[1]
    rows = tn // LANES * GROUP_CH
    tiles_per_batch = tokens_per_batch // tn
    lb = functools.partial(_layer_block, layer=layer)
    return pl.pallas_call(
        functools.partial(_front_kernel, n_heads=n_heads),
        grid=(n // tn,),
        in_specs=[pl.BlockSpec((tn, d), lambda i: (i, 0)),
                  lb(wssm_t),
                  lb(w_in, col_block=width, col=1), lb(w_in, col_block=width, col=2),
                  lb(w_in, col_block=width, col=3),
                  lb(w_gate, col_block=d, col=1), lb(w_gate, col_block=d, col=2),
                  lb(b_gate, col_block=d, col=1), lb(b_gate, col_block=d, col=2),
                  lb(lng), lb(lnb), lb(ws), lb(bs_t),
                  pl.BlockSpec((1,) + kv.shape[1:], lambda i: (i // tiles_per_batch, 0, 0)),
                  lb(w_br, branch=1), lb(w_br, branch=2)],
        out_specs=[pl.BlockSpec((n_grp, rows, LANES), lambda i: (0, i, 0)),
                   pl.BlockSpec((tn, d), lambda i: (i, 0))],
        out_shape=[jax.ShapeDtypeStruct((n_grp, n // LANES * GROUP_CH, LANES), F32),
                   jax.ShapeDtypeStruct((n, d), F32)],
        compiler_params=pltpu.CompilerParams(dimension_semantics=("arbitrary",),
                                             vmem_limit_bytes=VMEM_LIMIT),
        name="mixer_front",
    )(x2, wssm_t, w_in, w_in, w_in, w_gate, w_gate, b_gate, b_gate, lng, lnb, ws, bs_t, kv, w_br, w_br)


def _route_t(lg, le):
    n_exp = float(le.shape[0])
    sub = lax.broadcasted_iota(jnp.int32, le.shape, 0).astype(F32)
    gmax = jnp.max(lg, axis=0, keepdims=True)
    denom = jnp.sum(jnp.exp(lg - gmax), axis=0, keepdims=True) * (1.0 / 8.0)
    pg_sel = 1.0 / denom
    first = jnp.min(jnp.where(lg == gmax, sub, n_exp), axis=0, keepdims=True)
    in_grp = jnp.floor(sub * 0.125) == jnp.floor(first * 0.125)
    neg = -jnp.inf
    l1 = jnp.where(in_grp, le, neg)
    m1 = jnp.max(l1, axis=0, keepdims=True)
    i1 = jnp.min(jnp.where(l1 == m1, sub, n_exp), axis=0, keepdims=True)
    l2 = jnp.where(sub == i1, neg, l1)
    m2 = jnp.max(l2, axis=0, keepdims=True)
    i2 = jnp.min(jnp.where(l2 == m2, sub, n_exp), axis=0, keepdims=True)
    e2 = jnp.exp(m2 - m1)
    p1 = 1.0 / (1.0 + e2)
    p2 = e2 / (1.0 + e2)
    return i1, i2, pg_sel * p1, pg_sel * p2


def _back_kernel(x_ref, m12_ref, y_ref, wglu_t_ref, bglu_ref, wbr0_ref, wg0_ref, bg0_ref, wout_ref,
                 lng_ref, lnb_ref, wrg_ref, brg_ref, wre_ref, bre_ref, x1_ref, route_ref, *, alpha):
    tn = x_ref.shape[0]
    x = x_ref[...]
    xb = x.astype(BF16)
    n_grp = y_ref.shape[0]
    cols = []
    for a in range(tn // LANES):
        y_t = y_ref[:, a * GROUP_CH:(a + 1) * GROUP_CH, :].reshape(n_grp * GROUP_CH, LANES)
        y_t = jax.nn.gelu(y_t)
        gate = jax.nn.sigmoid(_dot(wglu_t_ref[0], y_t.astype(BF16)) + bglu_ref[0])
        cols.append((y_t * gate).astype(BF16))
    ys_t = jnp.concatenate(cols, axis=1)
    br0 = lax.dot_general(ys_t, wbr0_ref[0, 0], TN_DIMS, preferred_element_type=F32)
    g0 = jax.nn.sigmoid(_dot(xb, wg0_ref[0]) + bg0_ref[0])
    merged = m12_ref[...] + g0 * br0
    h = _dot(merged.astype(BF16), wout_ref[0])
    x1 = _layer_norm(alpha * x + h, lng_ref[0], lnb_ref[0])
    x1_ref[...] = x1
    hi = lax.Precision.HIGHEST
    lg = lax.dot_general(wrg_ref[0], x1, NT_DIMS, preferred_element_type=F32, precision=hi) + brg_ref[0]
    le = lax.dot_general(wre_ref[0], x1, NT_DIMS, preferred_element_type=F32, precision=hi) + bre_ref[0]
    i1, i2, w1, w2 = _route_t(lg, le)
    route_ref[...] = jnp.concatenate([i1, i2, w1, w2, jnp.zeros((4, tn), F32)], axis=0)


def _back(x2, m12, y, wglu_t, bglu, w_br, w_gate, b_gate, wout, lng, lnb, wrg, brg, wre, bre, *, layer,
          tn, alpha):
    n, d = x2.shape
    n_grp = y.shape[0]
    rows = tn // LANES * GROUP_CH
    lb = functools.partial(_layer_block, layer=layer)
    return pl.pallas_call(
        functools.partial(_back_kernel, alpha=alpha),
        grid=(n // tn,),
        in_specs=[pl.BlockSpec((tn, d), lambda i: (i, 0)),
                  pl.BlockSpec((tn, d), lambda i: (i, 0)),
                  pl.BlockSpec((n_grp, rows, LANES), lambda i: (0, i, 0)),
                  lb(wglu_t), lb(bglu), lb(w_br, branch=0),
                  lb(w_gate, col_block=d, col=0), lb(b_gate, col_block=d, col=0), lb(wout),
                  lb(lng), lb(lnb), lb(wrg), lb(brg), lb(wre), lb(bre)],
        out_specs=[pl.BlockSpec((tn, d), lambda i: (i, 0)),
                   pl.BlockSpec((8, tn), lambda i: (0, i))],
        out_shape=[jax.ShapeDtypeStruct((n, d), F32),
                   jax.ShapeDtypeStruct((8, n), F32)],
        compiler_params=pltpu.CompilerParams(dimension_semantics=("arbitrary",),
                                             vmem_limit_bytes=VMEM_LIMIT),
        name="mixer_back",
    )(x2, m12, y, wglu_t, bglu, w_br, w_gate, b_gate, wout, lng, lnb, wrg, brg, wre, bre)


MOE_TILE = 256
MAX_TILES_LANES = 256


def _pos_kernel(route_ref, pos_ref, meta_ref, cnt_ref, offs_ref, carry_ref, *, n_exp):
    phase = pl.program_id(0)
    i = pl.program_id(1)
    tp = route_ref.shape[1]
    sub = lax.broadcasted_iota(jnp.int32, (n_exp, tp), 0).astype(F32)
    i1 = route_ref[0:1, :]
    i2 = route_ref[1:2, :]
    sel = jnp.where((sub == i1) | (sub == i2), 1.0, 0.0)
    tile_cnt = jnp.sum(sel, axis=1, keepdims=True)

    @pl.when((phase == 0) & (i == 0))
    def _():
        cnt_ref[...] = jnp.zeros_like(cnt_ref)

    @pl.when(phase == 0)
    def _():
        cnt_ref[...] += jnp.broadcast_to(tile_cnt, cnt_ref.shape)

    @pl.when((phase == 1) & (i == 0))
    def _():
        cnt = cnt_ref[...]
        padded = jnp.ceil(cnt * (1.0 / MOE_TILE)) * float(MOE_TILE)
        sub_e = lax.broadcasted_iota(jnp.int32, cnt.shape, 0)
        lane_e = lax.broadcasted_iota(jnp.int32, cnt.shape, 1)
        row = jnp.sum(jnp.where(sub_e == lane_e, padded, 0.0), axis=0, keepdims=True)
        offs = jnp.sum(jnp.where(lane_e < sub_e, row, 0.0), axis=1, keepdims=True)
        offs_ref[...] = jnp.broadcast_to(offs, offs_ref.shape)
        carry_ref[...] = jnp.zeros_like(carry_ref)
        total = jnp.sum(jnp.where(lane_e < n_exp, row, 0.0), axis=1, keepdims=True)[0:1]
        t = lax.broadcasted_iota(jnp.int32, (n_exp, MAX_TILES_LANES), 1).astype(F32) * float(MOE_TILE)
        texp = jnp.sum(jnp.where(offs <= t, 1.0, 0.0), axis=0, keepdims=True) - 1.0
        valid = jnp.where(t[0:1] < total, 1.0, 0.0)
        meta = jnp.concatenate([texp, valid, jnp.zeros((6, MAX_TILES_LANES), F32)], axis=0)
        meta_ref[...] = meta.astype(jnp.int32)

    @pl.when(phase == 1)
    def _():
        r = lax.broadcasted_iota(jnp.int32, (tp, tp), 0)
        c = lax.broadcasted_iota(jnp.int32, (tp, tp), 1)
        upper = jnp.where(r < c, 1.0, 0.0).astype(BF16)
        rank = _dot(sel.astype(BF16), upper)
        slot = offs_ref[:, 0:1] + carry_ref[:, 0:1] + rank
        pos_a = jnp.sum(jnp.where(sub == i1, slot, 0.0), axis=0, keepdims=True)
        pos_b = jnp.sum(jnp.where(sub == i2, slot, 0.0), axis=0, keepdims=True)
        pos = jnp.concatenate([pos_a, pos_b, jnp.zeros((6, tp), F32)], axis=0)
        pos_ref[...] = pos.astype(jnp.int32)
        carry_ref[...] += jnp.broadcast_to(tile_cnt, carry_ref.shape)


def _positions(route, *, n_exp, tp):
    n = route.shape[1]
    return pl.pallas_call(
        functools.partial(_pos_kernel, n_exp=n_exp),
        grid=(2, n // tp),
        in_specs=[pl.BlockSpec((8, tp), lambda p, i: (0, i))],
        out_specs=[pl.BlockSpec((8, tp), lambda p, i: (0, i * p)),
                   pl.BlockSpec((8, MAX_TILES_LANES), lambda p, i: (0, 0))],
        out_shape=[jax.ShapeDtypeStruct((8, n), jnp.int32),
                   jax.ShapeDtypeStruct((8, MAX_TILES_LANES), jnp.int32)],
        scratch_shapes=[pltpu.VMEM((n_exp, LANES), F32)] * 3,
        compiler_params=pltpu.CompilerParams(dimension_semantics=("arbitrary", "arbitrary")),
        name="moe_positions",
    )(route)


SC_CORES = 2
SC_SUBCORES = 16
SC_LANES = 16
SC_GATHER_ROWS = 32


def _sc_mesh():
    return plsc.VectorSubcoreMesh(core_axis_name="c", subcore_axis_name="s",
                                  num_cores=SC_CORES, num_subcores=SC_SUBCORES)


def _sc_invert(pos_a, pos_b, n_slots):
    n = pos_a.shape[0]

    def body(pa_hbm, pb_hbm, src_hbm, pa_v, pb_v, src_v):
        wid = lax.axis_index("s") * SC_CORES + lax.axis_index("c")

        @pl.when(wid == 0)
        def _():
            pltpu.sync_copy(pa_hbm, pa_v)
            pltpu.sync_copy(pb_hbm, pb_v)
            lane = lax.iota(jnp.int32, SC_LANES)

            @pl.loop(0, n_slots // SC_LANES)
            def _(j):
                off = pl.multiple_of(j * SC_LANES, SC_LANES)
                src_v[pl.ds(off, SC_LANES)] = lax.rem(lane + off, n)

            @pl.loop(0, n // SC_LANES)
            def _(j):
                off = pl.multiple_of(j * SC_LANES, SC_LANES)
                tok = lane + off
                plsc.store_scatter(src_v, [pa_v[pl.ds(off, SC_LANES)]], tok)
                plsc.store_scatter(src_v, [pb_v[pl.ds(off, SC_LANES)]], tok)

            pltpu.sync_copy(src_v, src_hbm)

    return pl.kernel(
        body, out_type=jax.ShapeDtypeStruct((n_slots,), jnp.int32), mesh=_sc_mesh(),
        scratch_types=[pltpu.VMEM((n,), jnp.int32), pltpu.VMEM((n,), jnp.int32),
                       pltpu.VMEM((n_slots,), jnp.int32)],
        compiler_params=pltpu.CompilerParams(needs_layout_passes=False),
        name="moe_invert",
    )(pos_a, pos_b)


def _sc_gather(table, idx):
    m = idx.shape[0]
    d = table.shape[1]
    n_workers = SC_CORES * SC_SUBCORES
    per_w = m // n_workers
    rows = SC_GATHER_ROWS

    def body(table_hbm, idx_hbm, out_hbm, idx_v, rows_v, sem):
        wid = lax.axis_index("s") * SC_CORES + lax.axis_index("c")
        base = pl.multiple_of(wid * per_w, rows)
        pltpu.sync_copy(idx_hbm.at[pl.ds(base, per_w)], idx_v)

        @pl.loop(0, per_w // rows)
        def _(j):
            off = pl.multiple_of(j * rows, rows)
            pltpu.async_copy(table_hbm.at[idx_v.at[pl.ds(off, rows)]], rows_v, sem).wait()
            pltpu.sync_copy(rows_v, out_hbm.at[pl.ds(base + off, rows)])

    return pl.kernel(
        body, out_type=jax.ShapeDtypeStruct((m, d), table.dtype), mesh=_sc_mesh(),
        scratch_types=[pltpu.VMEM((per_w,), jnp.int32), pltpu.VMEM((rows, d), table.dtype),
                       pltpu.SemaphoreType.DMA],
        name="moe_gather",
    )(table, idx)


def _expert_kernel(meta_ref, xs_ref, wg_ref, wu_ref, wd_ref, ys_ref):
    t = pl.program_id(0)

    @pl.when(meta_ref[1, t] == 1)
    def _():
        xb = xs_ref[...].astype(BF16)
        h = jax.nn.silu(_dot(xb, wg_ref[0, 0].astype(BF16))) * _dot(xb, wu_ref[0, 0].astype(BF16))
        ys_ref[...] = _dot(h.astype(BF16), wd_ref[0, 0].astype(BF16))

    @pl.when(meta_ref[1, t] != 1)
    def _():
        ys_ref[...] = jnp.zeros_like(ys_ref)


def _experts(meta, xs, wg, wu, wd, layer):
    n_slots, d = xs.shape
    f = wg.shape[3]
    grid_spec = pltpu.PrefetchScalarGridSpec(
        num_scalar_prefetch=1,
        grid=(n_slots // MOE_TILE,),
        in_specs=[pl.BlockSpec((MOE_TILE, d), lambda t, meta: (t, 0)),
                  pl.BlockSpec((1, 1, d, f), lambda t, meta: (layer, meta[0, t], 0, 0)),
                  pl.BlockSpec((1, 1, d, f), lambda t, meta: (layer, meta[0, t], 0, 0)),
                  pl.BlockSpec((1, 1, f, d), lambda t, meta: (layer, meta[0, t], 0, 0))],
        out_specs=pl.BlockSpec((MOE_TILE, d), lambda t, meta: (t, 0)),
    )
    return pl.pallas_call(
        _expert_kernel, grid_spec=grid_spec,
        out_shape=jax.ShapeDtypeStruct((n_slots, d), F32),
        compiler_params=pltpu.CompilerParams(dimension_semantics=("arbitrary",),
                                             vmem_limit_bytes=VMEM_LIMIT),
        name="moe_experts",
    )(meta, xs, wg, wu, wd)


def _combine_kernel(x_ref, ya_ref, yb_ref, route_ref, lng_ref, lnb_ref, o_ref, *, alpha):
    w = route_ref[...].T
    moe = w[:, 2:3] * ya_ref[...] + w[:, 3:4] * yb_ref[...]
    o_ref[...] = _layer_norm(alpha * x_ref[...] + moe, lng_ref[0], lnb_ref[0])


def _combine(x1, ya, yb, route, lng, lnb, *, layer, tn, alpha):
    n, d = x1.shape
    row = pl.BlockSpec((tn, d), lambda i: (i, 0))
    return pl.pallas_call(
        functools.partial(_combine_kernel, alpha=alpha),
        grid=(n // tn,),
        in_specs=[row, row, row, pl.BlockSpec((8, tn), lambda i: (0, i)),
                  _layer_block(lng, layer), _layer_block(lnb, layer)],
        out_specs=row,
        out_shape=jax.ShapeDtypeStruct((n, d), F32),
        compiler_params=pltpu.CompilerParams(dimension_semantics=("arbitrary",)),
        name="moe_combine",
    )(x1, ya, yb, route, lng, lnb)


def _moe(x1, route, wg, wu, wd, lng, lnb, *, layer, alpha):
    n, d = x1.shape
    n_exp = wg.shape[1]
    n_slots = (2 * n // MOE_TILE + n_exp) * MOE_TILE
    assert n_slots // MOE_TILE <= MAX_TILES_LANES
    pos, meta = _positions(route, n_exp=n_exp, tp=min(512, n))
    src = _sc_invert(pos[0], pos[1], n_slots)
    xs = _sc_gather(x1, src)
    ys = _experts(meta, xs, wg, wu, wd, layer)
    ya = _sc_gather(ys, pos[0])
    yb = _sc_gather(ys, pos[1])
    return _combine(x1, ya, yb, route, lng, lnb, layer=layer, tn=min(512, n), alpha=alpha)


def _dup(v):
    return jnp.concatenate([v, v], axis=-1)


def kernel(x, mem, w_in, w_gate, b_gate, ssm_lam_re, ssm_lam_im, ssm_log_step, ssm_b_re, ssm_b_im, ssm_c_re, ssm_c_im, ssm_d, w_glu, b_glu, gmlp_ln_g, gmlp_ln_b, w_spatial, b_spatial, w_kv, w_br, w_out, ln1_g, ln1_b, w_router_g, b_router_g, w_router_e, b_router_e, w_exp_gate, w_exp_up, w_exp_down, ln2_g, ln2_b):
    bsz, seq, d = x.shape
    depth = w_in.shape[0]
    n = bsz * seq
    n_grp = ssm_lam_re.shape[2]
    ssm_w = n_grp * GROUP_CH
    gmlp_w = gmlp_ln_g.shape[1]
    n_heads = 4
    n_moe_grp, exp_per_grp = b_router_e.shape[1], b_router_e.shape[2]
    alpha = (2.0 * depth) ** 0.25
    tn = 256

    ls = jnp.broadcast_to(ssm_log_step[..., None], ssm_lam_re.shape)
    six = [ssm_lam_re[:, 0], ssm_lam_im[:, 0], ssm_lam_re[:, 1], ssm_lam_im[:, 1], ls[:, 0], ls[:, 1]]
    prow = jnp.stack([_dup(v) for v in six] + [jnp.zeros_like(_dup(six[0]))] * 2, axis=2)
    pcol = jnp.stack(six + [jnp.zeros_like(six[0])] * 2, axis=-1)
    bt2 = jnp.concatenate([jnp.swapaxes(ssm_b_re, 2, 3), jnp.swapaxes(ssm_b_im, 2, 3)], axis=-1)
    c2 = jnp.concatenate([ssm_c_re, ssm_c_im], axis=-1)
    ct2 = jnp.concatenate([jnp.swapaxes(ssm_c_re, 2, 3), jnp.swapaxes(ssm_c_im, 2, 3)], axis=-1)
    wtab, pst, gmat, dec = _ssm_tables(prow, pcol, bt2, c2, ct2)
    dvec = jnp.repeat(ssm_d.reshape(depth, n_grp, 1, GROUP_CH), SSM_T, axis=-1)

    assert w_in.shape[2] == 4 * ssm_w and gmlp_w == ssm_w
    w_in_b = w_in.astype(BF16)
    wssm_t = jnp.swapaxes(w_in[:, :, :ssm_w], 1, 2).astype(BF16)
    w_gate_b = w_gate.astype(BF16)
    b_gate3 = b_gate.reshape(depth, 1, -1)
    w_br_b = w_br.astype(BF16)
    w_out_b = w_out.astype(BF16)
    w_kv_b = w_kv.astype(BF16)
    w_glu_t = jnp.swapaxes(w_glu, 1, 2).astype(BF16)
    b_glu_c = b_glu.reshape(depth, -1, 1)
    w_sp_b = w_spatial.astype(BF16)
    b_sp_t = jnp.swapaxes(b_spatial, 1, 2)
    row3 = lambda v: v.reshape(depth, 1, -1)
    wrg_t = jnp.repeat(jnp.swapaxes(w_router_g, 1, 2), exp_per_grp, axis=1)
    brg_t = jnp.repeat(b_router_g, exp_per_grp, axis=1).reshape(depth, -1, 1)
    wre_t = jnp.transpose(w_router_e, (0, 1, 3, 2)).reshape(depth, n_moe_grp * exp_per_grp, d)
    bre_t = b_router_e.reshape(depth, -1, 1)

    x2 = x.reshape(n, d)
    for l in range(depth):
        kv = _kv_proj(mem, w_kv_b, l)
        z, m12 = _front(
            x2, wssm_t, w_in_b, w_gate_b, b_gate3, row3(gmlp_ln_g), row3(gmlp_ln_b), w_sp_b, b_sp_t, kv,
            w_br_b, layer=l, tn=tn, n_heads=n_heads, tokens_per_batch=seq)
        y = _ssm_apply(z, wtab, pst, gmat, dec, dvec, l, seq // LANES)
        x1, route = _back(
            x2, m12, y, w_glu_t, b_glu_c, w_br_b, w_gate_b, b_gate3, w_out_b,
            row3(ln1_g), row3(ln1_b), wrg_t, brg_t, wre_t, bre_t, layer=l, tn=tn, alpha=alpha)
        x2 = _moe(x1, route, w_exp_gate, w_exp_up, w_exp_down, row3(ln2_g), row3(ln2_b),
                  layer=l, alpha=alpha)
    return x2.reshape(bsz, seq, d)
```

```python
import functools
import math

import jax
import jax.numpy as jnp
from jax import lax
from jax.experimental import pallas as pl
from jax.experimental.pallas import tpu as pltpu
from jax.experimental.pallas import tpu_sc as plsc

F32 = jnp.float32
BF16 = jnp.bfloat16

LANES = 128
HALF = LANES // 2
SSM_T = HALF
GROUP_CH = 16
N_STATE = 64
LN_EPS = 1e-5
VMEM_LIMIT = 56 * 1024 * 1024

NT_DIMS = (((1,), (1,)), ((), ()))
TN_DIMS = (((0,), (0,)), ((), ()))


def _dot(a, b):
    return jnp.dot(a, b, preferred_element_type=F32)


def _dot_hi(a, b):
    return jnp.dot(a, b, preferred_element_type=F32, precision=lax.Precision.HIGHEST)


def _layer_norm(x, g, b):
    mu = jnp.mean(x, axis=-1, keepdims=True)
    xc = x - mu
    var = jnp.mean(xc * xc, axis=-1, keepdims=True)
    return xc * lax.rsqrt(var + LN_EPS) * g + b


def _lane_iota(shape):
    return lax.broadcasted_iota(jnp.int32, shape, len(shape) - 1)


def _swap_halves(x):
    return pltpu.roll(x, HALF, 1)


def _ssm_tables_kernel(prow_ref, pcol_ref, bt_ref, c_ref, ct_ref,
                       wtab_ref, pst_ref, g_ref, dec_ref):
    T = SSM_T
    prow = prow_ref[0, 0]
    pcol = pcol_ref[0, 0]
    bt2 = bt_ref[0, 0]
    c2 = c_ref[0, 0]
    ct = ct_ref[0, 0]
    lane1 = _lane_iota((1, LANES))
    lo1 = lane1 < HALF
    sgn1 = jnp.where(lo1, -1.0, 1.0).astype(F32)

    lane_bc = _lane_iota((GROUP_CH, LANES))
    lo_bc = lane_bc < HALF
    c2s = jnp.where(lo_bc, -1.0, 1.0) * _swap_halves(c2)

    bbar = []
    a_row = []
    th_row = []
    for d in range(2):
        lre = prow[2 * d:2 * d + 1]
        lim = prow[2 * d + 1:2 * d + 2]
        dt = jnp.exp(prow[4 + d:5 + d])
        a = lre * dt
        th = lim * dt
        er = jnp.exp(a)
        lb_re = er * jnp.cos(th)
        lb_im = er * jnp.sin(th)
        num_re = lb_re - 1.0
        den = lre * lre + lim * lim
        coef_re = (num_re * lre + lb_im * lim) / den
        coef_im = (lb_im * lre - num_re * lim) / den
        bbar.append(coef_re * bt2 + (sgn1 * coef_im) * _swap_halves(bt2))
        a_row.append(a)
        th_row.append(th)

    s_col = lax.broadcasted_iota(jnp.int32, (T, 1), 0).astype(F32)
    for d in range(2):
        k = (T - 1.0) - s_col if d == 0 else s_col
        e = jnp.exp(k * a_row[d])
        ang = k * th_row[d]
        pr = e * jnp.cos(ang)
        pi = e * jnp.sin(ang)
        bsw = sgn1 * _swap_halves(bbar[d])
        for hp in range(GROUP_CH):
            blk = pr * bbar[d][hp:hp + 1] + pi * bsw[hp:hp + 1]
            pst_ref[0, 0, hp * T:(hp + 1) * T, d * LANES:(d + 1) * LANES] = blk.astype(pst_ref.dtype)

    for d in range(2):
        e = jnp.exp(float(T) * a_row[d])
        ang = float(T) * th_row[d]
        dec_ref[0, 0, :, d * LANES:(d + 1) * LANES] = e * jnp.where(lo1, jnp.cos(ang), jnp.sin(ang))

    lane_p = _lane_iota((N_STATE, LANES))
    lo_p = lane_p < HALF
    t_lane = jnp.where(lo_p, lane_p, lane_p - HALF).astype(F32)
    pw = []
    for d in range(2):
        dtc = jnp.exp(pcol[:, 4 + d:5 + d])
        a_c = pcol[:, 2 * d:2 * d + 1] * dtc
        th_c = pcol[:, 2 * d + 1:2 * d + 2] * dtc
        pw.append((a_c, th_c))
    for d in range(2):
        a_c, th_c = pw[d]
        k = t_lane + 1.0 if d == 0 else float(T) - t_lane
        e = jnp.exp(k * a_c)
        pr = e * jnp.cos(k * th_c)
        pi = e * jnp.sin(k * th_c)
        for j in range(GROUP_CH // 2):
            cre = jnp.where(lo_p, ct[:, 2 * j:2 * j + 1], ct[:, 2 * j + 1:2 * j + 2])
            cim = jnp.where(lo_p, ct[:, GROUP_CH + 2 * j:GROUP_CH + 2 * j + 1],
                            ct[:, GROUP_CH + 2 * j + 1:GROUP_CH + 2 * j + 2])
            g_re = cre * pr - cim * pi
            g_im = -(cre * pi + cim * pr)
            g_ref[0, 0, d * LANES:d * LANES + N_STATE, j * LANES:(j + 1) * LANES] = g_re.astype(g_ref.dtype)
            g_ref[0, 0, d * LANES + N_STATE:(d + 1) * LANES, j * LANES:(j + 1) * LANES] = g_im.astype(g_ref.dtype)

    res = []
    kb0 = None
    for d in range(2):
        a_c, th_c = pw[d]
        if d == 0:
            k = jnp.maximum(lane_p - HALF, 0).astype(F32)
        else:
            k = jnp.maximum(HALF - lane_p, 0).astype(F32)
        e = jnp.exp(k * a_c)
        rhs = jnp.concatenate([e * jnp.cos(k * th_c), -(e * jnp.sin(k * th_c))], axis=0)
        bsw = _swap_halves(bbar[d])
        b_re = jnp.where(lo_bc, bbar[d], bsw)
        b_im = jnp.where(lo_bc, bsw, bbar[d])
        bc = (b_re[:, None, :] * c2[None, :, :] + b_im[:, None, :] * c2s[None, :, :])
        bc = bc.reshape(GROUP_CH * GROUP_CH, LANES)
        res.append(_dot_hi(bc, rhs))
        if d == 1:
            lane_bcf = _lane_iota(bc.shape)
            kb0 = jnp.sum(jnp.where(lane_bcf < HALF, bc, 0.0), axis=1, keepdims=True)
    lane_w = _lane_iota(res[0].shape)
    wtab = jnp.where(lane_w >= HALF, res[0], res[1])
    wtab_ref[0, 0] = wtab + jnp.where(lane_w == HALF, kb0, 0.0)


def _ssm_tables(prow, pcol, bt2, c2, ct2):
    L, G = prow.shape[0], prow.shape[1]
    T = SSM_T
    blk = lambda shape: pl.BlockSpec((1, 1) + shape, lambda l, g: (l, g, 0, 0))
    return pl.pallas_call(
        _ssm_tables_kernel,
        grid=(L, G),
        in_specs=[blk((8, LANES)), blk((N_STATE, 8)), blk((GROUP_CH, LANES)),
                  blk((GROUP_CH, LANES)), blk((N_STATE, 2 * GROUP_CH))],
        out_specs=[blk((GROUP_CH * GROUP_CH, LANES)), blk((GROUP_CH * T, 2 * LANES)),
                   blk((2 * LANES, GROUP_CH * T)), blk((1, 2 * LANES))],
        out_shape=[jax.ShapeDtypeStruct((L, G, GROUP_CH * GROUP_CH, LANES), F32),
                   jax.ShapeDtypeStruct((L, G, GROUP_CH * T, 2 * LANES), BF16),
                   jax.ShapeDtypeStruct((L, G, 2 * LANES, GROUP_CH * T), BF16),
                   jax.ShapeDtypeStruct((L, G, 1, 2 * LANES), F32)],
        name="ssm_tables",
    )(prow, pcol, bt2, c2, ct2)


def _cmul_packed(x, d_re, d_sw):
    return x * d_re + _swap_halves(x) * d_sw


def _ssm_kernel(z_ref, wtab_ref, pst_ref, g_ref, dec_ref, dvec_ref, o_ref, m_ref, *, rows_per_batch):
    T = SSM_T
    n_rows = z_ref.shape[1] // GROUP_CH
    lane_t = _lane_iota((T, LANES))
    lo_t = lane_t < HALF

    def build(hp, carry):
        r0 = pl.multiple_of(hp * T, T)
        for j in range(GROUP_CH // 2):
            wa = jnp.broadcast_to(wtab_ref[0, 0, pl.ds(hp * GROUP_CH + 2 * j, 1), :], (T, LANES))
            wb = jnp.broadcast_to(wtab_ref[0, 0, pl.ds(hp * GROUP_CH + 2 * j + 1, 1), :], (T, LANES))
            ra = pltpu.roll(wa, HALF, 1, stride=1, stride_axis=0)
            rb = pltpu.roll(wb, 0, 1, stride=1, stride_axis=0)
            m_ref[pl.ds(r0, T), j * LANES:(j + 1) * LANES] = jnp.where(lo_t, ra, rb).astype(m_ref.dtype)
        return carry

    lax.fori_loop(0, GROUP_CH, build, 0)

    lane = _lane_iota((n_rows, LANES))
    lo = lane < HALF
    a0, a1 = [], []
    for j in range(GROUP_CH // 2):
        pe = z_ref[0, pl.ds(2 * j, n_rows, stride=GROUP_CH), :]
        po = z_ref[0, pl.ds(2 * j + 1, n_rows, stride=GROUP_CH), :]
        a0.append(jnp.where(lo, pe, _swap_halves(po)))
        a1.append(jnp.where(lo, _swap_halves(pe), po))
    a0 = jnp.concatenate(a0, axis=1)
    a1 = jnp.concatenate(a1, axis=1)
    a_f32 = jnp.concatenate([a0, a1], axis=0)
    a_bf = a_f32.astype(BF16)

    out = _dot(a_bf, m_ref[...]) + dvec_ref[0, 0] * a_f32
    s = _dot(a_bf, pst_ref[0, 0])
    sf0, sb0 = s[:n_rows, :LANES], s[:n_rows, LANES:]
    sf1, sb1 = s[n_rows:, :LANES], s[n_rows:, LANES:]

    sgn = jnp.where(_lane_iota((1, LANES)) < HALF, -1.0, 1.0).astype(F32)
    lo1 = _lane_iota((1, LANES)) < HALF

    def split(p):
        sw = _swap_halves(p)
        return jnp.where(lo1, p, sw), sgn * jnp.where(lo1, sw, p)

    def square(p):
        d_re, d_sw = split(p)
        return _cmul_packed(p, d_re, d_sw)

    dec = dec_ref[0, 0]
    dec_f, dec_b = dec[:, :LANES], dec[:, LANES:]
    df_re, df_sw = split(dec_f)
    db_re, db_sw = split(dec_b)

    row = lax.broadcasted_iota(jnp.int32, (n_rows, LANES), 0)
    rib = row % rows_per_batch

    ef = _cmul_packed(sf0, df_re, df_sw) + sf1
    eb = sb0 + _cmul_packed(sb1, db_re, db_sw)
    pf, pb = square(dec_f), square(dec_b)
    step = 1
    while step < rows_per_batch:
        pf_re, pf_sw = split(pf)
        pb_re, pb_sw = split(pb)
        shf = jnp.where(rib >= step, pltpu.roll(ef, step, 0), 0.0)
        ef = ef + _cmul_packed(shf, pf_re, pf_sw)
        shb = jnp.where(rib < rows_per_batch - step, pltpu.roll(eb, n_rows - step, 0), 0.0)
        eb = eb + _cmul_packed(shb, pb_re, pb_sw)
        pf, pb = square(pf), square(pb)
        step *= 2
    hf0 = jnp.where(rib >= 1, pltpu.roll(ef, 1, 0), 0.0)
    hf1 = _cmul_packed(hf0, df_re, df_sw) + sf0
    hb1 = jnp.where(rib < rows_per_batch - 1, pltpu.roll(eb, n_rows - 1, 0), 0.0)
    hb0 = sb1 + _cmul_packed(hb1, db_re, db_sw)
    h_in = jnp.concatenate([jnp.concatenate([hf0, hb0], axis=1),
                            jnp.concatenate([hf1, hb1], axis=1)], axis=0)
    out = out + _dot(h_in.astype(BF16), g_ref[0, 0])

    o0, o1 = out[:n_rows], out[n_rows:]
    for j in range(GROUP_CH // 2):
        t0 = o0[:, j * LANES:(j + 1) * LANES]
        t1 = o1[:, j * LANES:(j + 1) * LANES]
        o_ref[0, pl.ds(2 * j, n_rows, stride=GROUP_CH), :] = jnp.where(lo, t0, _swap_halves(t1))
        o_ref[0, pl.ds(2 * j + 1, n_rows, stride=GROUP_CH), :] = jnp.where(lo, _swap_halves(t0), t1)


def _ssm_apply(z, wtab, pst, gmat, dec, dvec, layer, rows_per_batch):
    G, R, _ = z.shape
    T = SSM_T
    tab = lambda shape: pl.BlockSpec((1, 1) + shape, lambda g: (layer, g, 0, 0))
    return pl.pallas_call(
        functools.partial(_ssm_kernel, rows_per_batch=rows_per_batch),
        grid=(G,),
        in_specs=[pl.BlockSpec((1, R, LANES), lambda g: (g, 0, 0)),
                  tab((GROUP_CH * GROUP_CH, LANES)), tab((GROUP_CH * T, 2 * LANES)),
                  tab((2 * LANES, GROUP_CH * T)), tab((1, 2 * LANES)), tab((1, GROUP_CH * T))],
        out_specs=pl.BlockSpec((1, R, LANES), lambda g: (g, 0, 0)),
        out_shape=jax.ShapeDtypeStruct(z.shape, F32),
        scratch_shapes=[pltpu.VMEM((GROUP_CH * T, GROUP_CH * T), BF16)],
        compiler_params=pltpu.CompilerParams(dimension_semantics=("arbitrary",),
                                             vmem_limit_bytes=VMEM_LIMIT),
        name="ssm_seq",
    )(z, wtab, pst, gmat, dec, dvec)


def _kv_kernel(mem_ref, w_ref, o_ref):
    o_ref[0] = _dot(mem_ref[0].astype(BF16), w_ref[0]).astype(o_ref.dtype)


def _kv_proj(mem, w_kv, layer):
    B, M, D = mem.shape
    width = w_kv.shape[2]
    return pl.pallas_call(
        _kv_kernel,
        grid=(B,),
        in_specs=[pl.BlockSpec((1, M, D), lambda b: (b, 0, 0)),
                  pl.BlockSpec((1, D, width), lambda b: (layer, 0, 0))],
        out_specs=pl.BlockSpec((1, M, width), lambda b: (b, 0, 0)),
        out_shape=jax.ShapeDtypeStruct((B, M, width), BF16),
        name="kv_proj",
    )(mem, w_kv)


def _front_kernel(x_ref, wssm_t_ref, wu_ref, wv_ref, wq_ref, wg1_ref, wg2_ref, bg1_ref, bg2_ref,
                  lng_ref, lnb_ref, ws_ref, bs_t_ref, kv_ref, wbr1_ref, wbr2_ref, z_ref, m_ref, *, n_heads):
    tn, d_model = x_ref.shape
    xb = x_ref[...].astype(BF16)

    u_t = lax.dot_general(wssm_t_ref[0], xb, NT_DIMS, preferred_element_type=F32)
    n_grp = u_t.shape[0] // GROUP_CH
    for a in range(tn // LANES):
        z_ref[:, a * GROUP_CH:(a + 1) * GROUP_CH, :] = (
            u_t[:, a * LANES:(a + 1) * LANES].reshape(n_grp, GROUP_CH, LANES))

    u = jax.nn.gelu(_dot(xb, wu_ref[0]))
    v = jax.nn.gelu(_dot(xb, wv_ref[0]))
    half = v.shape[1]
    vb = _layer_norm(v, lng_ref[0], lnb_ref[0]).astype(BF16)
    lane = _lane_iota((LANES, LANES))
    lo = lane < HALF
    bs_t = bs_t_ref[0]
    s_rows = []
    for a in range(tn // LANES):
        va = vb[a * LANES:(a + 1) * LANES]
        tiles = []
        for j in range(half // LANES):
            rhs = va[:, j * LANES:(j + 1) * LANES]
            ev = _dot(ws_ref[0, 2 * j], rhs)
            od = _dot(ws_ref[0, 2 * j + 1], rhs)
            bias = jnp.where(lo, bs_t[:, 2 * j:2 * j + 1], bs_t[:, 2 * j + 1:2 * j + 2])
            tiles.append(jnp.where(lo, ev, od) + bias)
        s_rows.append(jnp.concatenate(tiles, axis=1))
    y_gmlp = u * jnp.concatenate(s_rows, axis=0)

    q = _dot(xb, wq_ref[0])
    hd = q.shape[1] // n_heads
    kv = kv_ref[0]
    outs = []
    for h in range(n_heads):
        qh = q[:, h * hd:(h + 1) * hd].astype(BF16)
        kh = kv[:, h * hd:(h + 1) * hd]
        vh = kv[:, q.shape[1] + h * hd:q.shape[1] + (h + 1) * hd]
        sc = lax.dot_general(qh, kh, NT_DIMS, preferred_element_type=F32) * (hd ** -0.5)
        sc = sc - jnp.max(sc, axis=-1, keepdims=True)
        p = jnp.exp(sc)
        p = p / jnp.sum(p, axis=-1, keepdims=True)
        outs.append(_dot(p.astype(BF16), vh))
    y_mem = jnp.concatenate(outs, axis=1)

    g1 = jax.nn.sigmoid(_dot(xb, wg1_ref[0]) + bg1_ref[0])
    g2 = jax.nn.sigmoid(_dot(xb, wg2_ref[0]) + bg2_ref[0])
    m_ref[...] = (g1 * _dot(y_gmlp.astype(BF16), wbr1_ref[0, 0])
                  + g2 * _dot(y_mem.astype(BF16), wbr2_ref[0, 0]))


def _layer_block(arr, layer, col_block=None, col=0, branch=None):
    if branch is not None:
        return pl.BlockSpec((1, 1) + arr.shape[2:], lambda *_: (layer, branch, 0, 0))
    shape = arr.shape[1:]
    if col_block is not None:
        shape = shape[:-1] + (col_block,)
    nd = len(shape)
    return pl.BlockSpec((1,) + shape, lambda *_: (layer,) + (0,) * (nd - 1) + (col,))


def _front(x2, wssm_t, w_in, w_gate, b_gate, lng, lnb, ws, bs_t, kv, w_br, *, layer, tn, n_heads,
           tokens_per_batch):
    n, d = x2.shape
    n_grp = wssm_t.shape[1] // GROUP_CH
    width = wssm_t.shape[1]
    rows = tn // LANES * GROUP_CH
    tiles_per_batch = tokens_per_batch // tn
    lb = functools.partial(_layer_block, layer=layer)
    return pl.pallas_call(
        functools.partial(_front_kernel, n_heads=n_heads),
        grid=(n // tn,),
        in_specs=[pl.BlockSpec((tn, d), lambda i: (i, 0)),
                  lb(wssm_t),
                  lb(w_in, col_block=width, col=1), lb(w_in, col_block=width, col=2),
                  lb(w_in, col_block=width, col=3),
                  lb(w_gate, col_block=d, col=1), lb(w_gate, col_block=d, col=2),
                  lb(b_gate, col_block=d, col=1), lb(b_gate, col_block=d, col=2),
                  lb(lng), lb(lnb), lb(ws), lb(bs_t),
                  pl.BlockSpec((1,) + kv.shape[1:], lambda i: (i // tiles_per_batch, 0, 0)),
                  lb(w_br, branch=1), lb(w_br, branch=2)],
        out_specs=[pl.BlockSpec((n_grp, rows, LANES), lambda i: (0, i, 0)),
                   pl.BlockSpec((tn, d), lambda i: (i, 0))],
        out_shape=[jax.ShapeDtypeStruct((n_grp, n // LANES * GROUP_CH, LANES), F32),
                   jax.ShapeDtypeStruct((n, d), F32)],
        compiler_params=pltpu.CompilerParams(dimension_semantics=("arbitrary",),
                                             vmem_limit_bytes=VMEM_LIMIT),
        name="mixer_front",
    )(x2, wssm_t, w_in, w_in, w_in, w_gate, w_gate, b_gate, b_gate, lng, lnb, ws, bs_t, kv, w_br, w_br)


def _route_t(lg, le):
    n_exp = float(le.shape[0])
    sub = lax.broadcasted_iota(jnp.int32, le.shape, 0).astype(F32)
    gmax = jnp.max(lg, axis=0, keepdims=True)
    denom = jnp.sum(jnp.exp(lg - gmax), axis=0, keepdims=True) * (1.0 / 8.0)
    pg_sel = 1.0 / denom
    first = jnp.min(jnp.where(lg == gmax, sub, n_exp), axis=0, keepdims=True)
    in_grp = jnp.floor(sub * 0.125) == jnp.floor(first * 0.125)
    neg = -jnp.inf
    l1 = jnp.where(in_grp, le, neg)
    m1 = jnp.max(l1, axis=0, keepdims=True)
    i1 = jnp.min(jnp.where(l1 == m1, sub, n_exp), axis=0, keepdims=True)
    l2 = jnp.where(sub == i1, neg, l1)
    m2 = jnp.max(l2, axis=0, keepdims=True)
    i2 = jnp.min(jnp.where(l2 == m2, sub, n_exp), axis=0, keepdims=True)
    e2 = jnp.exp(m2 - m1)
    p1 = 1.0 / (1.0 + e2)
    p2 = e2 / (1.0 + e2)
    return i1, i2, pg_sel * p1, pg_sel * p2


def _pack_bf16_pairs(x):
    c = x.shape[1] // 2
    rounded = x.astype(BF16).astype(F32)
    bits = lax.bitcast_convert_type(rounded, jnp.uint32)
    word = (bits[:, :c] >> 16) | bits[:, c:]
    return lax.bitcast_convert_type(word, jnp.int32)


def _unpack_bf16_pairs(w):
    bits = lax.bitcast_convert_type(w, jnp.uint32)
    lo = lax.bitcast_convert_type(bits << 16, F32)
    hi = lax.bitcast_convert_type(bits & jnp.uint32(0xFFFF0000), F32)
    return jnp.concatenate([lo, hi], axis=1)


def _back_kernel(x_ref, m12_ref, y_ref, wglu_t_ref, bglu_ref, wbr0_ref, wg0_ref, bg0_ref, wout_ref,
                 lng_ref, lnb_ref, wr_ref, br_ref, x1_ref, xpk_ref, route_ref, *, alpha):
    tn = x_ref.shape[0]
    x = x_ref[...]
    xb = x.astype(BF16)
    n_grp = y_ref.shape[0]
    cols = []
    for a in range(tn // LANES):
        y_t = y_ref[:, a * GROUP_CH:(a + 1) * GROUP_CH, :].reshape(n_grp * GROUP_CH, LANES)
        y_t = jax.nn.gelu(y_t)
        gate = jax.nn.sigmoid(_dot(wglu_t_ref[0], y_t.astype(BF16)) + bglu_ref[0])
        cols.append((y_t * gate).astype(BF16))
    ys_t = jnp.concatenate(cols, axis=1)
    br0 = lax.dot_general(ys_t, wbr0_ref[0, 0], TN_DIMS, preferred_element_type=F32)
    g0 = jax.nn.sigmoid(_dot(xb, wg0_ref[0]) + bg0_ref[0])
    merged = m12_ref[...] + g0 * br0
    h = _dot(merged.astype(BF16), wout_ref[0])
    x1 = _layer_norm(alpha * x + h, lng_ref[0], lnb_ref[0])
    x1_ref[...] = x1
    xpk_ref[...] = _pack_bf16_pairs(x1)
    x_hi = x1.astype(BF16)
    x_lo = (x1 - x_hi.astype(F32)).astype(BF16)
    w_hl = wr_ref[0]
    n_r = w_hl.shape[0] // 2
    part = lax.dot_general(w_hl, x_hi, NT_DIMS, preferred_element_type=F32)
    logits = (part[:n_r] + part[n_r:]
              + lax.dot_general(w_hl[:n_r], x_lo, NT_DIMS, preferred_element_type=F32) + br_ref[0])
    n_exp = n_r // 2
    i1, i2, w1, w2 = _route_t(logits[:n_exp], logits[n_exp:])
    route_ref[...] = jnp.concatenate([i1, i2, w1, w2, jnp.zeros((4, tn), F32)], axis=0)


def _back(x2, m12, y, wglu_t, bglu, w_br, w_gate, b_gate, wout, lng, lnb, wr, br, *, layer, tn, alpha):
    n, d = x2.shape
    n_grp = y.shape[0]
    rows = tn // LANES * GROUP_CH
    lb = functools.partial(_layer_block, layer=layer)
    return pl.pallas_call(
        functools.partial(_back_kernel, alpha=alpha),
        grid=(n // tn,),
        in_specs=[pl.BlockSpec((tn, d), lambda i: (i, 0)),
                  pl.BlockSpec((tn, d), lambda i: (i, 0)),
                  pl.BlockSpec((n_grp, rows, LANES), lambda i: (0, i, 0)),
                  lb(wglu_t), lb(bglu), lb(w_br, branch=0),
                  lb(w_gate, col_block=d, col=0), lb(b_gate, col_block=d, col=0), lb(wout),
                  lb(lng), lb(lnb), lb(wr), lb(br)],
        out_specs=[pl.BlockSpec((tn, d), lambda i: (i, 0)),
                   pl.BlockSpec((tn, d // 2), lambda i: (i, 0)),
                   pl.BlockSpec((8, tn), lambda i: (0, i))],
        out_shape=[jax.ShapeDtypeStruct((n, d), F32),
                   jax.ShapeDtypeStruct((n, d // 2), jnp.int32),
                   jax.ShapeDtypeStruct((8, n), F32)],
        compiler_params=pltpu.CompilerParams(dimension_semantics=("arbitrary",),
                                             vmem_limit_bytes=VMEM_LIMIT),
        name="mixer_back",
    )(x2, m12, y, wglu_t, bglu, w_br, w_gate, b_gate, wout, lng, lnb, wr, br)


MOE_TILE = 256
MAX_TILES_LANES = 256


def _pos_kernel(route_ref, pos_ref, meta_ref, cnt_ref, offs_ref, carry_ref, *, n_exp):
    phase = pl.program_id(0)
    i = pl.program_id(1)
    tp = route_ref.shape[1]
    sub = lax.broadcasted_iota(jnp.int32, (n_exp, tp), 0).astype(F32)
    i1 = route_ref[0:1, :]
    i2 = route_ref[1:2, :]
    sel = jnp.where((sub == i1) | (sub == i2), 1.0, 0.0)
    tile_cnt = jnp.sum(sel, axis=1, keepdims=True)

    @pl.when((phase == 0) & (i == 0))
    def _():
        cnt_ref[...] = jnp.zeros_like(cnt_ref)

    @pl.when(phase == 0)
    def _():
        cnt_ref[...] += jnp.broadcast_to(tile_cnt, cnt_ref.shape)

    @pl.when((phase == 1) & (i == 0))
    def _():
        cnt = cnt_ref[...]
        padded = jnp.ceil(cnt * (1.0 / MOE_TILE)) * float(MOE_TILE)
        sub_e = lax.broadcasted_iota(jnp.int32, cnt.shape, 0)
        lane_e = lax.broadcasted_iota(jnp.int32, cnt.shape, 1)
        row = jnp.sum(jnp.where(sub_e == lane_e, padded, 0.0), axis=0, keepdims=True)
        offs = jnp.sum(jnp.where(lane_e < sub_e, row, 0.0), axis=1, keepdims=True)
        offs_ref[...] = jnp.broadcast_to(offs, offs_ref.shape)
        carry_ref[...] = jnp.zeros_like(carry_ref)
        total = jnp.sum(jnp.where(lane_e < n_exp, row, 0.0), axis=1, keepdims=True)[0:1]
        t = lax.broadcasted_iota(jnp.int32, (n_exp, MAX_TILES_LANES), 1).astype(F32) * float(MOE_TILE)
        texp = jnp.sum(jnp.where(offs <= t, 1.0, 0.0), axis=0, keepdims=True) - 1.0
        valid = jnp.where(t[0:1] < total, 1.0, 0.0)
        meta = jnp.concatenate([texp, valid, jnp.zeros((6, MAX_TILES_LANES), F32)], axis=0)
        meta_ref[...] = meta.astype(jnp.int32)

    @pl.when(phase == 1)
    def _():
        r = lax.broadcasted_iota(jnp.int32, (tp, tp), 0)
        c = lax.broadcasted_iota(jnp.int32, (tp, tp), 1)
        upper = jnp.where(r < c, 1.0, 0.0).astype(BF16)
        rank = _dot(sel.astype(BF16), upper)
        slot = offs_ref[:, 0:1] + carry_ref[:, 0:1] + rank
        pos_a = jnp.sum(jnp.where(sub == i1, slot, 0.0), axis=0, keepdims=True)
        pos_b = jnp.sum(jnp.where(sub == i2, slot, 0.0), axis=0, keepdims=True)
        pos = jnp.concatenate([pos_a, pos_b, jnp.zeros((6, tp), F32)], axis=0)
        pos_ref[...] = pos.astype(jnp.int32)
        carry_ref[...] += jnp.broadcast_to(tile_cnt, carry_ref.shape)


def _positions(route, *, n_exp, tp):
    n = route.shape[1]
    return pl.pallas_call(
        functools.partial(_pos_kernel, n_exp=n_exp),
        grid=(2, n // tp),
        in_specs=[pl.BlockSpec((8, tp), lambda p, i: (0, i))],
        out_specs=[pl.BlockSpec((8, tp), lambda p, i: (0, i * p)),
                   pl.BlockSpec((8, MAX_TILES_LANES), lambda p, i: (0, 0))],
        out_shape=[jax.ShapeDtypeStruct((8, n), jnp.int32),
                   jax.ShapeDtypeStruct((8, MAX_TILES_LANES), jnp.int32)],
        scratch_shapes=[pltpu.VMEM((n_exp, LANES), F32)] * 3,
        compiler_params=pltpu.CompilerParams(dimension_semantics=("arbitrary", "arbitrary")),
        name="moe_positions",
    )(route)


SC_CORES = 2
SC_SUBCORES = 16
SC_LANES = 16
SC_GATHER_ROWS = 64


def _sc_mesh():
    return plsc.VectorSubcoreMesh(core_axis_name="c", subcore_axis_name="s",
                                  num_cores=SC_CORES, num_subcores=SC_SUBCORES)


def _sc_invert(pos_a, pos_b, n_slots):
    n = pos_a.shape[0]

    def body(pa_hbm, pb_hbm, src_hbm, pa_v, pb_v, src_v):
        wid = lax.axis_index("s") * SC_CORES + lax.axis_index("c")

        @pl.when(wid == 0)
        def _():
            pltpu.sync_copy(pa_hbm, pa_v)
            pltpu.sync_copy(pb_hbm, pb_v)
            lane = lax.iota(jnp.int32, SC_LANES)

            @pl.loop(0, n_slots // SC_LANES)
            def _(j):
                off = pl.multiple_of(j * SC_LANES, SC_LANES)
                src_v[pl.ds(off, SC_LANES)] = lax.rem(lane + off, n)

            @pl.loop(0, n // SC_LANES)
            def _(j):
                off = pl.multiple_of(j * SC_LANES, SC_LANES)
                tok = lane + off
                plsc.store_scatter(src_v, [pa_v[pl.ds(off, SC_LANES)]], tok)
                plsc.store_scatter(src_v, [pb_v[pl.ds(off, SC_LANES)]], tok)

            pltpu.sync_copy(src_v, src_hbm)

    return pl.kernel(
        body, out_type=jax.ShapeDtypeStruct((n_slots,), jnp.int32), mesh=_sc_mesh(),
        scratch_types=[pltpu.VMEM((n,), jnp.int32), pltpu.VMEM((n,), jnp.int32),
                       pltpu.VMEM((n_slots,), jnp.int32)],
        compiler_params=pltpu.CompilerParams(needs_layout_passes=False),
        name="moe_invert",
    )(pos_a, pos_b)


def _sc_gather(table, idx):
    m = idx.shape[0]
    d = table.shape[1]
    n_workers = SC_CORES * SC_SUBCORES
    per_w = m // n_workers
    rows = SC_GATHER_ROWS

    n_pairs = per_w // (2 * rows)
    assert per_w == n_pairs * 2 * rows

    def body(table_hbm, idx_hbm, out_hbm, idx_v, buf_v, sems):
        wid = lax.axis_index("s") * SC_CORES + lax.axis_index("c")
        base = pl.multiple_of(wid * per_w, rows)
        pltpu.sync_copy(idx_hbm.at[pl.ds(base, per_w)], idx_v)

        def gather(chunk, slot):
            off = pl.multiple_of(chunk * rows, rows)
            return pltpu.make_async_copy(table_hbm.at[idx_v.at[pl.ds(off, rows)]], buf_v.at[slot],
                                         sems.at[slot])

        def write(chunk, slot):
            off = pl.multiple_of(chunk * rows, rows)
            pltpu.sync_copy(buf_v.at[slot], out_hbm.at[pl.ds(base + off, rows)])

        gather(0, 0).start()

        @pl.loop(0, n_pairs)
        def _(p):
            c0 = 2 * p
            gather(c0, 0).wait()
            gather(c0 + 1, 1).start()
            write(c0, 0)
            gather(c0 + 1, 1).wait()

            @pl.when(p + 1 < n_pairs)
            def _():
                gather(c0 + 2, 0).start()

            write(c0 + 1, 1)

    return pl.kernel(
        body, out_type=jax.ShapeDtypeStruct((m, d), table.dtype), mesh=_sc_mesh(),
        scratch_types=[pltpu.VMEM((per_w,), jnp.int32), pltpu.VMEM((2, rows, d), table.dtype),
                       pltpu.SemaphoreType.DMA((2,))],
        name="moe_gather",
    )(table, idx)


def _expert_kernel(meta_ref, xs_ref, wg_ref, wu_ref, wd_ref, ys_ref):
    t = pl.program_id(0)

    @pl.when(meta_ref[1, t] == 1)
    def _():
        xb = _unpack_bf16_pairs(xs_ref[...]).astype(BF16)
        h = jax.nn.silu(_dot(xb, wg_ref[0, 0].astype(BF16))) * _dot(xb, wu_ref[0, 0].astype(BF16))
        ys_ref[...] = _pack_bf16_pairs(_dot(h.astype(BF16), wd_ref[0, 0].astype(BF16)))

    @pl.when(meta_ref[1, t] != 1)
    def _():
        ys_ref[...] = jnp.zeros_like(ys_ref)


def _experts(meta, xs, wg, wu, wd, layer):
    n_slots, dp = xs.shape
    d, f = wg.shape[2], wg.shape[3]
    grid_spec = pltpu.PrefetchScalarGridSpec(
        num_scalar_prefetch=1,
        grid=(n_slots // MOE_TILE,),
        in_specs=[pl.BlockSpec((MOE_TILE, dp), lambda t, meta: (t, 0)),
                  pl.BlockSpec((1, 1, d, f), lambda t, meta: (layer, meta[0, t], 0, 0)),
                  pl.BlockSpec((1, 1, d, f), lambda t, meta: (layer, meta[0, t], 0, 0)),
                  pl.BlockSpec((1, 1, f, d), lambda t, meta: (layer, meta[0, t], 0, 0))],
        out_specs=pl.BlockSpec((MOE_TILE, dp), lambda t, meta: (t, 0)),
    )
    return pl.pallas_call(
        _expert_kernel, grid_spec=grid_spec,
        out_shape=jax.ShapeDtypeStruct((n_slots, dp), jnp.int32),
        compiler_params=pltpu.CompilerParams(dimension_semantics=("arbitrary",),
                                             vmem_limit_bytes=VMEM_LIMIT),
        name="moe_experts",
    )(meta, xs, wg, wu, wd)


def _combine_kernel(x_ref, ya_ref, yb_ref, route_ref, lng_ref, lnb_ref, o_ref, *, alpha):
    w = route_ref[...].T
    moe = w[:, 2:3] * _unpack_bf16_pairs(ya_ref[...]) + w[:, 3:4] * _unpack_bf16_pairs(yb_ref[...])
    o_ref[...] = _layer_norm(alpha * x_ref[...] + moe, lng_ref[0], lnb_ref[0])


def _combine(x1, yab, route, lng, lnb, *, layer, tn, alpha):
    n, d = x1.shape
    row = pl.BlockSpec((tn, d), lambda i: (i, 0))
    tiles = n // tn
    return pl.pallas_call(
        functools.partial(_combine_kernel, alpha=alpha),
        grid=(tiles,),
        in_specs=[row,
                  pl.BlockSpec((tn, d // 2), lambda i: (i, 0)),
                  pl.BlockSpec((tn, d // 2), lambda i: (i + tiles, 0)),
                  pl.BlockSpec((8, tn), lambda i: (0, i)),
                  _layer_block(lng, layer), _layer_block(lnb, layer)],
        out_specs=row,
        out_shape=jax.ShapeDtypeStruct((n, d), F32),
        compiler_params=pltpu.CompilerParams(dimension_semantics=("arbitrary",)),
        name="moe_combine",
    )(x1, yab, yab, route, lng, lnb)


def _moe(x1, xpk, route, wg, wu, wd, lng, lnb, *, layer, alpha):
    n, d = x1.shape
    n_exp = wg.shape[1]
    n_slots = (2 * n // MOE_TILE + n_exp) * MOE_TILE
    assert n_slots // MOE_TILE <= MAX_TILES_LANES
    pos, meta = _positions(route, n_exp=n_exp, tp=min(512, n))
    src = _sc_invert(pos[0], pos[1], n_slots)
    xs = _sc_gather(xpk, src)
    ys = _experts(meta, xs, wg, wu, wd, layer)
    yab = _sc_gather(ys, pos[:2].reshape(2 * n))
    return _combine(x1, yab, route, lng, lnb, layer=layer, tn=min(512, n), alpha=alpha)


def _dup(v):
    return jnp.concatenate([v, v], axis=-1)


def kernel(x, mem, w_in, w_gate, b_gate, ssm_lam_re, ssm_lam_im, ssm_log_step, ssm_b_re, ssm_b_im, ssm_c_re, ssm_c_im, ssm_d, w_glu, b_glu, gmlp_ln_g, gmlp_ln_b, w_spatial, b_spatial, w_kv, w_br, w_out, ln1_g, ln1_b, w_router_g, b_router_g, w_router_e, b_router_e, w_exp_gate, w_exp_up, w_exp_down, ln2_g, ln2_b):
    bsz, seq, d = x.shape
    depth = w_in.shape[0]
    n = bsz * seq
    n_grp = ssm_lam_re.shape[2]
    ssm_w = n_grp * GROUP_CH
    gmlp_w = gmlp_ln_g.shape[1]
    n_heads = 4
    n_moe_grp, exp_per_grp = b_router_e.shape[1], b_router_e.shape[2]
    alpha = (2.0 * depth) ** 0.25
    tn = 512

    ls = jnp.broadcast_to(ssm_log_step[..., None], ssm_lam_re.shape)
    six = [ssm_lam_re[:, 0], ssm_lam_im[:, 0], ssm_lam_re[:, 1], ssm_lam_im[:, 1], ls[:, 0], ls[:, 1]]
    prow = jnp.stack([_dup(v) for v in six] + [jnp.zeros_like(_dup(six[0]))] * 2, axis=2)
    pcol = jnp.stack(six + [jnp.zeros_like(six[0])] * 2, axis=-1)
    bt2 = jnp.concatenate([jnp.swapaxes(ssm_b_re, 2, 3), jnp.swapaxes(ssm_b_im, 2, 3)], axis=-1)
    c2 = jnp.concatenate([ssm_c_re, ssm_c_im], axis=-1)
    ct2 = jnp.concatenate([jnp.swapaxes(ssm_c_re, 2, 3), jnp.swapaxes(ssm_c_im, 2, 3)], axis=-1)
    wtab, pst, gmat, dec = _ssm_tables(prow, pcol, bt2, c2, ct2)
    dvec = jnp.repeat(ssm_d.reshape(depth, n_grp, 1, GROUP_CH), SSM_T, axis=-1)

    assert w_in.shape[2] == 4 * ssm_w and gmlp_w == ssm_w
    w_in_b = w_in.astype(BF16)
    wssm_t = jnp.swapaxes(w_in[:, :, :ssm_w], 1, 2).astype(BF16)
    w_gate_b = w_gate.astype(BF16)
    b_gate3 = b_gate.reshape(depth, 1, -1)
    w_br_b = w_br.astype(BF16)
    w_out_b = w_out.astype(BF16)
    w_kv_b = w_kv.astype(BF16)
    w_glu_t = jnp.swapaxes(w_glu, 1, 2).astype(BF16)
    b_glu_c = b_glu.reshape(depth, -1, 1)
    w_sp_b = w_spatial.astype(BF16)
    b_sp_t = jnp.swapaxes(b_spatial, 1, 2)
    row3 = lambda v: v.reshape(depth, 1, -1)
    wrg_t = jnp.repeat(jnp.swapaxes(w_router_g, 1, 2), exp_per_grp, axis=1)
    wre_t = jnp.transpose(w_router_e, (0, 1, 3, 2)).reshape(depth, n_moe_grp * exp_per_grp, d)
    wr = jnp.concatenate([wrg_t, wre_t], axis=1)
    wr_hi = wr.astype(BF16)
    wr_hl = jnp.concatenate([wr_hi, (wr - wr_hi.astype(F32)).astype(BF16)], axis=1)
    br = jnp.concatenate([jnp.repeat(b_router_g, exp_per_grp, axis=1),
                          b_router_e.reshape(depth, -1)], axis=1).reshape(depth, -1, 1)

    x2 = x.reshape(n, d)
    for l in range(depth):
        kv = _kv_proj(mem, w_kv_b, l)
        z, m12 = _front(
            x2, wssm_t, w_in_b, w_gate_b, b_gate3, row3(gmlp_ln_g), row3(gmlp_ln_b), w_sp_b, b_sp_t, kv,
            w_br_b, layer=l, tn=tn, n_heads=n_heads, tokens_per_batch=seq)
        y = _ssm_apply(z, wtab, pst, gmat, dec, dvec, l, seq // LANES)
        x1, xpk, route = _back(
            x2, m12, y, w_glu_t, b_glu_c, w_br_b, w_gate_b, b_gate3, w_out_b,
            row3(ln1_g), row3(ln1_b), wr_hl, br, layer=l, tn=tn, alpha=alpha)
        x2 = _moe(x1, xpk, route, w_exp_gate, w_exp_up, w_exp_down, row3(ln2_g), row3(ln2_b),
                  layer=l, alpha=alpha)
    return x2.reshape(bsz, seq, d)
```

```python
import functools
import math

import jax
import jax.numpy as jnp
from jax import lax
from jax.experimental import pallas as pl
from jax.experimental.pallas import tpu as pltpu
from jax.experimental.pallas import tpu_sc as plsc

F32 = jnp.float32
BF16 = jnp.bfloat16

LANES = 128
HALF = LANES // 2
SSM_T = HALF
GROUP_CH = 16
N_STATE = 64
LN_EPS = 1e-5
VMEM_LIMIT = 56 * 1024 * 1024

NT_DIMS = (((1,), (1,)), ((), ()))
TN_DIMS = (((0,), (0,)), ((), ()))


def _dot(a, b):
    return jnp.dot(a, b, preferred_element_type=F32)


def _dot_hi(a, b):
    return jnp.dot(a, b, preferred_element_type=F32, precision=lax.Precision.HIGHEST)


def _layer_norm(x, g, b):
    mu = jnp.mean(x, axis=-1, keepdims=True)
    xc = x - mu
    var = jnp.mean(xc * xc, axis=-1, keepdims=True)
    return xc * lax.rsqrt(var + LN_EPS) * g + b


def _lane_iota(shape):
    return lax.broadcasted_iota(jnp.int32, shape, len(shape) - 1)


def _swap_halves(x):
    return pltpu.roll(x, HALF, 1)


def _ssm_tables_kernel(prow_ref, pcol_ref, bt_ref, c_ref, ct_ref,
                       wtab_ref, pst_ref, g_ref, dec_ref):
    T = SSM_T
    prow = prow_ref[0, 0]
    pcol = pcol_ref[0, 0]
    bt2 = bt_ref[0, 0]
    c2 = c_ref[0, 0]
    ct = ct_ref[0, 0]
    lane1 = _lane_iota((1, LANES))
    lo1 = lane1 < HALF
    sgn1 = jnp.where(lo1, -1.0, 1.0).astype(F32)

    lane_bc = _lane_iota((GROUP_CH, LANES))
    lo_bc = lane_bc < HALF
    c2s = jnp.where(lo_bc, -1.0, 1.0) * _swap_halves(c2)

    bbar = []
    a_row = []
    th_row = []
    for d in range(2):
        lre = prow[2 * d:2 * d + 1]
        lim = prow[2 * d + 1:2 * d + 2]
        dt = jnp.exp(prow[4 + d:5 + d])
        a = lre * dt
        th = lim * dt
        er = jnp.exp(a)
        lb_re = er * jnp.cos(th)
        lb_im = er * jnp.sin(th)
        num_re = lb_re - 1.0
        den = lre * lre + lim * lim
        coef_re = (num_re * lre + lb_im * lim) / den
        coef_im = (lb_im * lre - num_re * lim) / den
        bbar.append(coef_re * bt2 + (sgn1 * coef_im) * _swap_halves(bt2))
        a_row.append(a)
        th_row.append(th)

    s_col = lax.broadcasted_iota(jnp.int32, (T, 1), 0).astype(F32)
    for d in range(2):
        k = (T - 1.0) - s_col if d == 0 else s_col
        e = jnp.exp(k * a_row[d])
        ang = k * th_row[d]
        pr = e * jnp.cos(ang)
        pi = e * jnp.sin(ang)
        bsw = sgn1 * _swap_halves(bbar[d])
        for hp in range(GROUP_CH):
            blk = pr * bbar[d][hp:hp + 1] + pi * bsw[hp:hp + 1]
            pst_ref[0, 0, hp * T:(hp + 1) * T, d * LANES:(d + 1) * LANES] = blk.astype(pst_ref.dtype)

    for d in range(2):
        e = jnp.exp(float(T) * a_row[d])
        ang = float(T) * th_row[d]
        dec_ref[0, 0, :, d * LANES:(d + 1) * LANES] = e * jnp.where(lo1, jnp.cos(ang), jnp.sin(ang))

    lane_p = _lane_iota((N_STATE, LANES))
    lo_p = lane_p < HALF
    t_lane = jnp.where(lo_p, lane_p, lane_p - HALF).astype(F32)
    pw = []
    for d in range(2):
        dtc = jnp.exp(pcol[:, 4 + d:5 + d])
        a_c = pcol[:, 2 * d:2 * d + 1] * dtc
        th_c = pcol[:, 2 * d + 1:2 * d + 2] * dtc
        pw.append((a_c, th_c))
    for d in range(2):
        a_c, th_c = pw[d]
        k = t_lane + 1.0 if d == 0 else float(T) - t_lane
        e = jnp.exp(k * a_c)
        pr = e * jnp.cos(k * th_c)
        pi = e * jnp.sin(k * th_c)
        for j in range(GROUP_CH // 2):
            cre = jnp.where(lo_p, ct[:, 2 * j:2 * j + 1], ct[:, 2 * j + 1:2 * j + 2])
            cim = jnp.where(lo_p, ct[:, GROUP_CH + 2 * j:GROUP_CH + 2 * j + 1],
                            ct[:, GROUP_CH + 2 * j + 1:GROUP_CH + 2 * j + 2])
            g_re = cre * pr - cim * pi
            g_im = -(cre * pi + cim * pr)
            g_ref[0, 0, d * LANES:d * LANES + N_STATE, j * LANES:(j + 1) * LANES] = g_re.astype(g_ref.dtype)
            g_ref[0, 0, d * LANES + N_STATE:(d + 1) * LANES, j * LANES:(j + 1) * LANES] = g_im.astype(g_ref.dtype)

    res = []
    kb0 = None
    for d in range(2):
        a_c, th_c = pw[d]
        if d == 0:
            k = jnp.maximum(lane_p - HALF, 0).astype(F32)
        else:
            k = jnp.maximum(HALF - lane_p, 0).astype(F32)
        e = jnp.exp(k * a_c)
        rhs = jnp.concatenate([e * jnp.cos(k * th_c), -(e * jnp.sin(k * th_c))], axis=0)
        bsw = _swap_halves(bbar[d])
        b_re = jnp.where(lo_bc, bbar[d], bsw)
        b_im = jnp.where(lo_bc, bsw, bbar[d])
        bc = (b_re[:, None, :] * c2[None, :, :] + b_im[:, None, :] * c2s[None, :, :])
        bc = bc.reshape(GROUP_CH * GROUP_CH, LANES)
        res.append(_dot_hi(bc, rhs))
        if d == 1:
            lane_bcf = _lane_iota(bc.shape)
            kb0 = jnp.sum(jnp.where(lane_bcf < HALF, bc, 0.0), axis=1, keepdims=True)
    lane_w = _lane_iota(res[0].shape)
    wtab = jnp.where(lane_w >= HALF, res[0], res[1])
    wtab_ref[0, 0] = wtab + jnp.where(lane_w == HALF, kb0, 0.0)


def _ssm_tables(prow, pcol, bt2, c2, ct2):
    L, G = prow.shape[0], prow.shape[1]
    T = SSM_T
    blk = lambda shape: pl.BlockSpec((1, 1) + shape, lambda l, g: (l, g, 0, 0))
    return pl.pallas_call(
        _ssm_tables_kernel,
        grid=(L, G),
        in_specs=[blk((8, LANES)), blk((N_STATE, 8)), blk((GROUP_CH, LANES)),
                  blk((GROUP_CH, LANES)), blk((N_STATE, 2 * GROUP_CH))],
        out_specs=[blk((GROUP_CH * GROUP_CH, LANES)), blk((GROUP_CH * T, 2 * LANES)),
                   blk((2 * LANES, GROUP_CH * T)), blk((1, 2 * LANES))],
        out_shape=[jax.ShapeDtypeStruct((L, G, GROUP_CH * GROUP_CH, LANES), F32),
                   jax.ShapeDtypeStruct((L, G, GROUP_CH * T, 2 * LANES), BF16),
                   jax.ShapeDtypeStruct((L, G, 2 * LANES, GROUP_CH * T), BF16),
                   jax.ShapeDtypeStruct((L, G, 1, 2 * LANES), F32)],
        name="ssm_tables",
    )(prow, pcol, bt2, c2, ct2)


def _cmul_packed(x, d_re, d_sw):
    return x * d_re + _swap_halves(x) * d_sw


def _ssm_kernel(*refs, n_streams, rows_per_batch):
    z_refs = refs[:n_streams]
    wtab_ref, pst_ref, g_ref, dec_ref, dvec_ref = refs[n_streams:n_streams + 5]
    o_refs = refs[n_streams + 5:2 * n_streams + 5]
    m_ref = refs[2 * n_streams + 5]
    T = SSM_T
    rows_s = z_refs[0].shape[1] // GROUP_CH
    n_rows = n_streams * rows_s
    lane_t = _lane_iota((T, LANES))
    lo_t = lane_t < HALF

    def build(hp, carry):
        r0 = pl.multiple_of(hp * T, T)
        for j in range(GROUP_CH // 2):
            wa = jnp.broadcast_to(wtab_ref[0, 0, pl.ds(hp * GROUP_CH + 2 * j, 1), :], (T, LANES))
            wb = jnp.broadcast_to(wtab_ref[0, 0, pl.ds(hp * GROUP_CH + 2 * j + 1, 1), :], (T, LANES))
            ra = pltpu.roll(wa, HALF, 1, stride=1, stride_axis=0)
            rb = pltpu.roll(wb, 0, 1, stride=1, stride_axis=0)
            m_ref[pl.ds(r0, T), j * LANES:(j + 1) * LANES] = jnp.where(lo_t, ra, rb).astype(m_ref.dtype)
        return carry

    lax.fori_loop(0, GROUP_CH, build, 0)

    lane = _lane_iota((n_rows, LANES))
    lo = lane < HALF
    a0, a1 = [], []
    for j in range(GROUP_CH // 2):
        pe = jnp.concatenate([z[0, pl.ds(2 * j, rows_s, stride=GROUP_CH), :] for z in z_refs], axis=0)
        po = jnp.concatenate([z[0, pl.ds(2 * j + 1, rows_s, stride=GROUP_CH), :] for z in z_refs], axis=0)
        a0.append(jnp.where(lo, pe, _swap_halves(po)))
        a1.append(jnp.where(lo, _swap_halves(pe), po))
    a0 = jnp.concatenate(a0, axis=1)
    a1 = jnp.concatenate(a1, axis=1)
    a_f32 = jnp.concatenate([a0, a1], axis=0)
    a_bf = a_f32.astype(BF16)

    out = _dot(a_bf, m_ref[...]) + dvec_ref[0, 0] * a_f32
    s = _dot(a_bf, pst_ref[0, 0])
    sf0, sb0 = s[:n_rows, :LANES], s[:n_rows, LANES:]
    sf1, sb1 = s[n_rows:, :LANES], s[n_rows:, LANES:]

    sgn = jnp.where(_lane_iota((1, LANES)) < HALF, -1.0, 1.0).astype(F32)
    lo1 = _lane_iota((1, LANES)) < HALF

    def split(p):
        sw = _swap_halves(p)
        return jnp.where(lo1, p, sw), sgn * jnp.where(lo1, sw, p)

    def square(p):
        d_re, d_sw = split(p)
        return _cmul_packed(p, d_re, d_sw)

    dec = dec_ref[0, 0]
    dec_f, dec_b = dec[:, :LANES], dec[:, LANES:]
    df_re, df_sw = split(dec_f)
    db_re, db_sw = split(dec_b)

    row = lax.broadcasted_iota(jnp.int32, (n_rows, LANES), 0)
    rib = row % rows_per_batch

    ef = _cmul_packed(sf0, df_re, df_sw) + sf1
    eb = sb0 + _cmul_packed(sb1, db_re, db_sw)
    pf, pb = square(dec_f), square(dec_b)
    step = 1
    while step < rows_per_batch:
        pf_re, pf_sw = split(pf)
        pb_re, pb_sw = split(pb)
        shf = jnp.where(rib >= step, pltpu.roll(ef, step, 0), 0.0)
        ef = ef + _cmul_packed(shf, pf_re, pf_sw)
        shb = jnp.where(rib < rows_per_batch - step, pltpu.roll(eb, n_rows - step, 0), 0.0)
        eb = eb + _cmul_packed(shb, pb_re, pb_sw)
        pf, pb = square(pf), square(pb)
        step *= 2
    hf0 = jnp.where(rib >= 1, pltpu.roll(ef, 1, 0), 0.0)
    hf1 = _cmul_packed(hf0, df_re, df_sw) + sf0
    hb1 = jnp.where(rib < rows_per_batch - 1, pltpu.roll(eb, n_rows - 1, 0), 0.0)
    hb0 = sb1 + _cmul_packed(hb1, db_re, db_sw)
    h_in = jnp.concatenate([jnp.concatenate([hf0, hb0], axis=1),
                            jnp.concatenate([hf1, hb1], axis=1)], axis=0)
    out = out + _dot(h_in.astype(BF16), g_ref[0, 0])

    o0, o1 = out[:n_rows], out[n_rows:]
    for j in range(GROUP_CH // 2):
        t0 = o0[:, j * LANES:(j + 1) * LANES]
        t1 = o1[:, j * LANES:(j + 1) * LANES]
        even = jnp.where(lo, t0, _swap_halves(t1))
        odd = jnp.where(lo, _swap_halves(t0), t1)
        for s, o_ref in enumerate(o_refs):
            o_ref[0, pl.ds(2 * j, rows_s, stride=GROUP_CH), :] = even[s * rows_s:(s + 1) * rows_s]
            o_ref[0, pl.ds(2 * j + 1, rows_s, stride=GROUP_CH), :] = odd[s * rows_s:(s + 1) * rows_s]


def _ssm_apply(zs, wtab, pst, gmat, dec, dvec, layer, rows_per_batch):
    G, R, _ = zs[0].shape
    T = SSM_T
    n_streams = len(zs)
    tab = lambda shape: pl.BlockSpec((1, 1) + shape, lambda g: (layer, g, 0, 0))
    seq_block = pl.BlockSpec((1, R, LANES), lambda g: (g, 0, 0))
    return pl.pallas_call(
        functools.partial(_ssm_kernel, n_streams=n_streams, rows_per_batch=rows_per_batch),
        grid=(G,),
        in_specs=[seq_block] * n_streams + [
            tab((GROUP_CH * GROUP_CH, LANES)), tab((GROUP_CH * T, 2 * LANES)),
            tab((2 * LANES, GROUP_CH * T)), tab((1, 2 * LANES)), tab((1, GROUP_CH * T))],
        out_specs=[seq_block] * n_streams,
        out_shape=[jax.ShapeDtypeStruct(zs[0].shape, F32)] * n_streams,
        scratch_shapes=[pltpu.VMEM((GROUP_CH * T, GROUP_CH * T), BF16)],
        compiler_params=pltpu.CompilerParams(dimension_semantics=("arbitrary",),
                                             vmem_limit_bytes=VMEM_LIMIT),
        name="ssm_seq",
    )(*zs, wtab, pst, gmat, dec, dvec)


def _kv_kernel(mem_ref, w_ref, o_ref):
    o_ref[0] = _dot(mem_ref[0].astype(BF16), w_ref[0]).astype(o_ref.dtype)


def _kv_proj(mem, w_kv, layer):
    B, M, D = mem.shape
    width = w_kv.shape[2]
    return pl.pallas_call(
        _kv_kernel,
        grid=(B,),
        in_specs=[pl.BlockSpec((1, M, D), lambda b: (b, 0, 0)),
                  pl.BlockSpec((1, D, width), lambda b: (layer, 0, 0))],
        out_specs=pl.BlockSpec((1, M, width), lambda b: (b, 0, 0)),
        out_shape=jax.ShapeDtypeStruct((B, M, width), BF16),
        name="kv_proj",
    )(mem, w_kv)


def _front_kernel(x_ref, wssm_t_ref, wu_ref, wv_ref, wq_ref, wg1_ref, wg2_ref, bg1_ref, bg2_ref,
                  lng_ref, lnb_ref, ws_ref, bs_t_ref, kv_ref, wbr1_ref, wbr2_ref, z_ref, m_ref, *, n_heads):
    tn, d_model = x_ref.shape
    xb = x_ref[...].astype(BF16)

    u_t = lax.dot_general(wssm_t_ref[0], xb, NT_DIMS, preferred_element_type=F32)
    n_grp = u_t.shape[0] // GROUP_CH
    for a in range(tn // LANES):
        z_ref[:, a * GROUP_CH:(a + 1) * GROUP_CH, :] = (
            u_t[:, a * LANES:(a + 1) * LANES].reshape(n_grp, GROUP_CH, LANES))

    u = jax.nn.gelu(_dot(xb, wu_ref[0]))
    v = jax.nn.gelu(_dot(xb, wv_ref[0]))
    half = v.shape[1]
    vb = _layer_norm(v, lng_ref[0], lnb_ref[0]).astype(BF16)
    lane = _lane_iota((LANES, LANES))
    lo = lane < HALF
    bs_t = bs_t_ref[0]
    s_rows = []
    for a in range(tn // LANES):
        va = vb[a * LANES:(a + 1) * LANES]
        tiles = []
        for j in range(half // LANES):
            rhs = va[:, j * LANES:(j + 1) * LANES]
            ev = _dot(ws_ref[0, 2 * j], rhs)
            od = _dot(ws_ref[0, 2 * j + 1], rhs)
            bias = jnp.where(lo, bs_t[:, 2 * j:2 * j + 1], bs_t[:, 2 * j + 1:2 * j + 2])
            tiles.append(jnp.where(lo, ev, od) + bias)
        s_rows.append(jnp.concatenate(tiles, axis=1))
    y_gmlp = u * jnp.concatenate(s_rows, axis=0)

    q = _dot(xb, wq_ref[0])
    hd = q.shape[1] // n_heads
    kv = kv_ref[0]
    outs = []
    for h in range(n_heads):
        qh = q[:, h * hd:(h + 1) * hd].astype(BF16)
        kh = kv[:, h * hd:(h + 1) * hd]
        vh = kv[:, q.shape[1] + h * hd:q.shape[1] + (h + 1) * hd]
        sc = lax.dot_general(qh, kh, NT_DIMS, preferred_element_type=F32) * (hd ** -0.5)
        sc = sc - jnp.max(sc, axis=-1, keepdims=True)
        p = jnp.exp(sc)
        p = p / jnp.sum(p, axis=-1, keepdims=True)
        outs.append(_dot(p.astype(BF16), vh))
    y_mem = jnp.concatenate(outs, axis=1)

    g1 = jax.nn.sigmoid(_dot(xb, wg1_ref[0]) + bg1_ref[0])
    g2 = jax.nn.sigmoid(_dot(xb, wg2_ref[0]) + bg2_ref[0])
    m_ref[...] = (g1 * _dot(y_gmlp.astype(BF16), wbr1_ref[0, 0])
                  + g2 * _dot(y_mem.astype(BF16), wbr2_ref[0, 0]))


def _layer_block(arr, layer, col_block=None, col=0, branch=None):
    if branch is not None:
        return pl.BlockSpec((1, 1) + arr.shape[2:], lambda *_: (layer, branch, 0, 0))
    shape = arr.shape[1:]
    if col_block is not None:
        shape = shape[:-1] + (col_block,)
    nd = len(shape)
    return pl.BlockSpec((1,) + shape, lambda *_: (layer,) + (0,) * (nd - 1) + (col,))


def _front(x2, wssm_t, w_in, w_gate, b_gate, lng, lnb, ws, bs_t, kv, w_br, *, layer, tn, n_heads,
           tokens_per_batch, n, row_off, batch_off):
    d = x2.shape[1]
    n_grp = wssm_t.shape[1] // GROUP_CH
    width = wssm_t.shape[1]
    rows = tn // LANES * GROUP_CH
    tiles_per_batch = tokens_per_batch // tn
    tile_off = row_off // tn
    lb = functools.partial(_layer_block, layer=layer)
    return pl.pallas_call(
        functools.partial(_front_kernel, n_heads=n_heads),
        grid=(n // tn,),
        in_specs=[pl.BlockSpec((tn, d), lambda i: (i + tile_off, 0)),
                  lb(wssm_t),
                  lb(w_in, col_block=width, col=1), lb(w_in, col_block=width, col=2),
                  lb(w_in, col_block=width, col=3),
                  lb(w_gate, col_block=d, col=1), lb(w_gate, col_block=d, col=2),
                  lb(b_gate, col_block=d, col=1), lb(b_gate, col_block=d, col=2),
                  lb(lng), lb(lnb), lb(ws), lb(bs_t),
                  pl.BlockSpec((1,) + kv.shape[1:], lambda i: (i // tiles_per_batch + batch_off, 0, 0)),
                  lb(w_br, branch=1), lb(w_br, branch=2)],
        out_specs=[pl.BlockSpec((n_grp, rows, LANES), lambda i: (0, i, 0)),
                   pl.BlockSpec((tn, d), lambda i: (i, 0))],
        out_shape=[jax.ShapeDtypeStruct((n_grp, n // LANES * GROUP_CH, LANES), F32),
                   jax.ShapeDtypeStruct((n, d), F32)],
        compiler_params=pltpu.CompilerParams(dimension_semantics=("arbitrary",),
                                             vmem_limit_bytes=VMEM_LIMIT),
        name="mixer_front",
    )(x2, wssm_t, w_in, w_in, w_in, w_gate, w_gate, b_gate, b_gate, lng, lnb, ws, bs_t, kv, w_br, w_br)


def _route_t(lg, le):
    n_exp = float(le.shape[0])
    sub = lax.broadcasted_iota(jnp.int32, le.shape, 0).astype(F32)
    gmax = jnp.max(lg, axis=0, keepdims=True)
    denom = jnp.sum(jnp.exp(lg - gmax), axis=0, keepdims=True) * (1.0 / 8.0)
    pg_sel = 1.0 / denom
    first = jnp.min(jnp.where(lg == gmax, sub, n_exp), axis=0, keepdims=True)
    in_grp = jnp.floor(sub * 0.125) == jnp.floor(first * 0.125)
    neg = -jnp.inf
    l1 = jnp.where(in_grp, le, neg)
    m1 = jnp.max(l1, axis=0, keepdims=True)
    i1 = jnp.min(jnp.where(l1 == m1, sub, n_exp), axis=0, keepdims=True)
    l2 = jnp.where(sub == i1, neg, l1)
    m2 = jnp.max(l2, axis=0, keepdims=True)
    i2 = jnp.min(jnp.where(l2 == m2, sub, n_exp), axis=0, keepdims=True)
    e2 = jnp.exp(m2 - m1)
    p1 = 1.0 / (1.0 + e2)
    p2 = e2 / (1.0 + e2)
    return i1, i2, pg_sel * p1, pg_sel * p2


def _pack_bf16_pairs(x):
    c = x.shape[1] // 2
    rounded = x.astype(BF16).astype(F32)
    bits = lax.bitcast_convert_type(rounded, jnp.uint32)
    word = (bits[:, :c] >> 16) | bits[:, c:]
    return lax.bitcast_convert_type(word, jnp.int32)


def _unpack_bf16_pairs(w):
    bits = lax.bitcast_convert_type(w, jnp.uint32)
    lo = lax.bitcast_convert_type(bits << 16, F32)
    hi = lax.bitcast_convert_type(bits & jnp.uint32(0xFFFF0000), F32)
    return jnp.concatenate([lo, hi], axis=1)


def _back_kernel(x_ref, m12_ref, y_ref, wglu_t_ref, bglu_ref, wbr0_ref, wg0_ref, bg0_ref, wout_ref,
                 lng_ref, lnb_ref, wr_ref, br_ref, x1_ref, xpk_ref, route_ref, *, alpha):
    tn = x_ref.shape[0]
    x = x_ref[...]
    xb = x.astype(BF16)
    n_grp = y_ref.shape[0]
    cols = []
    for a in range(tn // LANES):
        y_t = y_ref[:, a * GROUP_CH:(a + 1) * GROUP_CH, :].reshape(n_grp * GROUP_CH, LANES)
        y_t = jax.nn.gelu(y_t)
        gate = jax.nn.sigmoid(_dot(wglu_t_ref[0], y_t.astype(BF16)) + bglu_ref[0])
        cols.append((y_t * gate).astype(BF16))
    ys_t = jnp.concatenate(cols, axis=1)
    br0 = lax.dot_general(ys_t, wbr0_ref[0, 0], TN_DIMS, preferred_element_type=F32)
    g0 = jax.nn.sigmoid(_dot(xb, wg0_ref[0]) + bg0_ref[0])
    merged = m12_ref[...] + g0 * br0
    h = _dot(merged.astype(BF16), wout_ref[0])
    x1 = _layer_norm(alpha * x + h, lng_ref[0], lnb_ref[0])
    x1_ref[...] = x1
    xpk_ref[...] = _pack_bf16_pairs(x1)
    x_hi = x1.astype(BF16)
    x_lo = (x1 - x_hi.astype(F32)).astype(BF16)
    w_hl = wr_ref[0]
    n_r = w_hl.shape[0] // 2
    part = lax.dot_general(w_hl, x_hi, NT_DIMS, preferred_element_type=F32)
    logits = (part[:n_r] + part[n_r:]
              + lax.dot_general(w_hl[:n_r], x_lo, NT_DIMS, preferred_element_type=F32) + br_ref[0])
    n_exp = n_r // 2
    i1, i2, w1, w2 = _route_t(logits[:n_exp], logits[n_exp:])
    route_ref[...] = jnp.concatenate([i1, i2, w1, w2, jnp.zeros((4, tn), F32)], axis=0)


def _back(x2, m12, y, wglu_t, bglu, w_br, w_gate, b_gate, wout, lng, lnb, wr, br, *, layer, tn, alpha,
          row_off):
    n, d = m12.shape
    n_grp = y.shape[0]
    rows = tn // LANES * GROUP_CH
    tile_off = row_off // tn
    lb = functools.partial(_layer_block, layer=layer)
    return pl.pallas_call(
        functools.partial(_back_kernel, alpha=alpha),
        grid=(n // tn,),
        in_specs=[pl.BlockSpec((tn, d), lambda i: (i + tile_off, 0)),
                  pl.BlockSpec((tn, d), lambda i: (i, 0)),
                  pl.BlockSpec((n_grp, rows, LANES), lambda i: (0, i, 0)),
                  lb(wglu_t), lb(bglu), lb(w_br, branch=0),
                  lb(w_gate, col_block=d, col=0), lb(b_gate, col_block=d, col=0), lb(wout),
                  lb(lng), lb(lnb), lb(wr), lb(br)],
        out_specs=[pl.BlockSpec((tn, d), lambda i: (i, 0)),
                   pl.BlockSpec((tn, d // 2), lambda i: (i, 0)),
                   pl.BlockSpec((8, tn), lambda i: (0, i))],
        out_shape=[jax.ShapeDtypeStruct((n, d), F32),
                   jax.ShapeDtypeStruct((n, d // 2), jnp.int32),
                   jax.ShapeDtypeStruct((8, n), F32)],
        compiler_params=pltpu.CompilerParams(dimension_semantics=("arbitrary",),
                                             vmem_limit_bytes=VMEM_LIMIT),
        name="mixer_back",
    )(x2, m12, y, wglu_t, bglu, w_br, w_gate, b_gate, wout, lng, lnb, wr, br)


MOE_TILE = 256
MAX_TILES_LANES = 256


def _pos_kernel(route_ref, pos_ref, meta_ref, cnt_ref, offs_ref, carry_ref, *, n_exp):
    phase = pl.program_id(0)
    i = pl.program_id(1)
    tp = route_ref.shape[1]
    sub = lax.broadcasted_iota(jnp.int32, (n_exp, tp), 0).astype(F32)
    i1 = route_ref[0:1, :]
    i2 = route_ref[1:2, :]
    sel = jnp.where((sub == i1) | (sub == i2), 1.0, 0.0)
    tile_cnt = jnp.sum(sel, axis=1, keepdims=True)

    @pl.when((phase == 0) & (i == 0))
    def _():
        cnt_ref[...] = jnp.zeros_like(cnt_ref)

    @pl.when(phase == 0)
    def _():
        cnt_ref[...] += jnp.broadcast_to(tile_cnt, cnt_ref.shape)

    @pl.when((phase == 1) & (i == 0))
    def _():
        cnt = cnt_ref[...]
        padded = jnp.ceil(cnt * (1.0 / MOE_TILE)) * float(MOE_TILE)
        sub_e = lax.broadcasted_iota(jnp.int32, cnt.shape, 0)
        lane_e = lax.broadcasted_iota(jnp.int32, cnt.shape, 1)
        row = jnp.sum(jnp.where(sub_e == lane_e, padded, 0.0), axis=0, keepdims=True)
        offs = jnp.sum(jnp.where(lane_e < sub_e, row, 0.0), axis=1, keepdims=True)
        offs_ref[...] = jnp.broadcast_to(offs, offs_ref.shape)
        carry_ref[...] = jnp.zeros_like(carry_ref)
        total = jnp.sum(jnp.where(lane_e < n_exp, row, 0.0), axis=1, keepdims=True)[0:1]
        t = lax.broadcasted_iota(jnp.int32, (n_exp, MAX_TILES_LANES), 1).astype(F32) * float(MOE_TILE)
        texp = jnp.sum(jnp.where(offs <= t, 1.0, 0.0), axis=0, keepdims=True) - 1.0
        valid = jnp.where(t[0:1] < total, 1.0, 0.0)
        meta = jnp.concatenate([texp, valid, jnp.zeros((6, MAX_TILES_LANES), F32)], axis=0)
        meta_ref[...] = meta.astype(jnp.int32)

    @pl.when(phase == 1)
    def _():
        r = lax.broadcasted_iota(jnp.int32, (tp, tp), 0)
        c = lax.broadcasted_iota(jnp.int32, (tp, tp), 1)
        upper = jnp.where(r < c, 1.0, 0.0).astype(BF16)
        rank = _dot(sel.astype(BF16), upper)
        slot = offs_ref[:, 0:1] + carry_ref[:, 0:1] + rank
        pos_a = jnp.sum(jnp.where(sub == i1, slot, 0.0), axis=0, keepdims=True)
        pos_b = jnp.sum(jnp.where(sub == i2, slot, 0.0), axis=0, keepdims=True)
        pos = jnp.concatenate([pos_a, pos_b, jnp.zeros((6, tp), F32)], axis=0)
        pos_ref[...] = pos.astype(jnp.int32)
        carry_ref[...] += jnp.broadcast_to(tile_cnt, carry_ref.shape)


def _positions(route, *, n_exp, tp):
    n = route.shape[1]
    return pl.pallas_call(
        functools.partial(_pos_kernel, n_exp=n_exp),
        grid=(2, n // tp),
        in_specs=[pl.BlockSpec((8, tp), lambda p, i: (0, i))],
        out_specs=[pl.BlockSpec((8, tp), lambda p, i: (0, i * p)),
                   pl.BlockSpec((8, MAX_TILES_LANES), lambda p, i: (0, 0))],
        out_shape=[jax.ShapeDtypeStruct((8, n), jnp.int32),
                   jax.ShapeDtypeStruct((8, MAX_TILES_LANES), jnp.int32)],
        scratch_shapes=[pltpu.VMEM((n_exp, LANES), F32)] * 3,
        compiler_params=pltpu.CompilerParams(dimension_semantics=("arbitrary", "arbitrary")),
        name="moe_positions",
    )(route)


SC_CORES = 2
SC_SUBCORES = 16
SC_LANES = 16
SC_GATHER_ROWS = 64


def _sc_mesh():
    return plsc.VectorSubcoreMesh(core_axis_name="c", subcore_axis_name="s",
                                  num_cores=SC_CORES, num_subcores=SC_SUBCORES)


def _sc_invert(pos_a, pos_b, n_slots):
    n = pos_a.shape[0]

    def body(pa_hbm, pb_hbm, src_hbm, pa_v, pb_v, src_v):
        wid = lax.axis_index("s") * SC_CORES + lax.axis_index("c")

        @pl.when(wid == 0)
        def _():
            pltpu.sync_copy(pa_hbm, pa_v)
            pltpu.sync_copy(pb_hbm, pb_v)
            lane = lax.iota(jnp.int32, SC_LANES)

            @pl.loop(0, n_slots // SC_LANES)
            def _(j):
                off = pl.multiple_of(j * SC_LANES, SC_LANES)
                src_v[pl.ds(off, SC_LANES)] = lax.rem(lane + off, n)

            @pl.loop(0, n // SC_LANES)
            def _(j):
                off = pl.multiple_of(j * SC_LANES, SC_LANES)
                tok = lane + off
                plsc.store_scatter(src_v, [pa_v[pl.ds(off, SC_LANES)]], tok)
                plsc.store_scatter(src_v, [pb_v[pl.ds(off, SC_LANES)]], tok)

            pltpu.sync_copy(src_v, src_hbm)

    return pl.kernel(
        body, out_type=jax.ShapeDtypeStruct((n_slots,), jnp.int32), mesh=_sc_mesh(),
        scratch_types=[pltpu.VMEM((n,), jnp.int32), pltpu.VMEM((n,), jnp.int32),
                       pltpu.VMEM((n_slots,), jnp.int32)],
        compiler_params=pltpu.CompilerParams(needs_layout_passes=False),
        name="moe_invert",
    )(pos_a, pos_b)


def _sc_gather(table, idx):
    m = idx.shape[0]
    d = table.shape[1]
    n_workers = SC_CORES * SC_SUBCORES
    per_w = m // n_workers
    rows = SC_GATHER_ROWS

    n_pairs = per_w // (2 * rows)
    assert per_w == n_pairs * 2 * rows

    def body(table_hbm, idx_hbm, out_hbm, idx_v, buf_v, sems):
        wid = lax.axis_index("s") * SC_CORES + lax.axis_index("c")
        base = pl.multiple_of(wid * per_w, rows)
        pltpu.sync_copy(idx_hbm.at[pl.ds(base, per_w)], idx_v)

        def gather(chunk, slot):
            off = pl.multiple_of(chunk * rows, rows)
            return pltpu.make_async_copy(table_hbm.at[idx_v.at[pl.ds(off, rows)]], buf_v.at[slot],
                                         sems.at[slot])

        def write(chunk, slot):
            off = pl.multiple_of(chunk * rows, rows)
            pltpu.sync_copy(buf_v.at[slot], out_hbm.at[pl.ds(base + off, rows)])

        gather(0, 0).start()

        @pl.loop(0, n_pairs)
        def _(p):
            c0 = 2 * p
            gather(c0, 0).wait()
            gather(c0 + 1, 1).start()
            write(c0, 0)
            gather(c0 + 1, 1).wait()

            @pl.when(p + 1 < n_pairs)
            def _():
                gather(c0 + 2, 0).start()

            write(c0 + 1, 1)

    return pl.kernel(
        body, out_type=jax.ShapeDtypeStruct((m, d), table.dtype), mesh=_sc_mesh(),
        scratch_types=[pltpu.VMEM((per_w,), jnp.int32), pltpu.VMEM((2, rows, d), table.dtype),
                       pltpu.SemaphoreType.DMA((2,))],
        name="moe_gather",
    )(table, idx)


def _expert_kernel(meta_ref, xs_ref, wg_ref, wu_ref, wd_ref, ys_ref):
    t = pl.program_id(0)

    @pl.when(meta_ref[1, t] == 1)
    def _():
        xb = _unpack_bf16_pairs(xs_ref[...]).astype(BF16)
        h = jax.nn.silu(_dot(xb, wg_ref[0, 0].astype(BF16))) * _dot(xb, wu_ref[0, 0].astype(BF16))
        ys_ref[...] = _pack_bf16_pairs(_dot(h.astype(BF16), wd_ref[0, 0].astype(BF16)))

    @pl.when(meta_ref[1, t] != 1)
    def _():
        ys_ref[...] = jnp.zeros_like(ys_ref)


def _experts(meta, xs, wg, wu, wd, layer):
    n_slots, dp = xs.shape
    d, f = wg.shape[2], wg.shape[3]
    grid_spec = pltpu.PrefetchScalarGridSpec(
        num_scalar_prefetch=1,
        grid=(n_slots // MOE_TILE,),
        in_specs=[pl.BlockSpec((MOE_TILE, dp), lambda t, meta: (t, 0)),
                  pl.BlockSpec((1, 1, d, f), lambda t, meta: (layer, meta[0, t], 0, 0)),
                  pl.BlockSpec((1, 1, d, f), lambda t, meta: (layer, meta[0, t], 0, 0)),
                  pl.BlockSpec((1, 1, f, d), lambda t, meta: (layer, meta[0, t], 0, 0))],
        out_specs=pl.BlockSpec((MOE_TILE, dp), lambda t, meta: (t, 0)),
    )
    return pl.pallas_call(
        _expert_kernel, grid_spec=grid_spec,
        out_shape=jax.ShapeDtypeStruct((n_slots, dp), jnp.int32),
        compiler_params=pltpu.CompilerParams(dimension_semantics=("arbitrary",),
                                             vmem_limit_bytes=VMEM_LIMIT),
        name="moe_experts",
    )(meta, xs, wg, wu, wd)


def _combine_kernel(x_ref, ya_ref, yb_ref, route_ref, lng_ref, lnb_ref, o_ref, *, alpha):
    w = route_ref[...].T
    moe = w[:, 2:3] * _unpack_bf16_pairs(ya_ref[...]) + w[:, 3:4] * _unpack_bf16_pairs(yb_ref[...])
    o_ref[...] = _layer_norm(alpha * x_ref[...] + moe, lng_ref[0], lnb_ref[0])


def _combine(x1, yab, route, lng, lnb, *, layer, tn, alpha):
    n, d = x1.shape
    row = pl.BlockSpec((tn, d), lambda i: (i, 0))
    tiles = n // tn
    return pl.pallas_call(
        functools.partial(_combine_kernel, alpha=alpha),
        grid=(tiles,),
        in_specs=[row,
                  pl.BlockSpec((tn, d // 2), lambda i: (i, 0)),
                  pl.BlockSpec((tn, d // 2), lambda i: (i + tiles, 0)),
                  pl.BlockSpec((8, tn), lambda i: (0, i)),
                  _layer_block(lng, layer), _layer_block(lnb, layer)],
        out_specs=row,
        out_shape=jax.ShapeDtypeStruct((n, d), F32),
        compiler_params=pltpu.CompilerParams(dimension_semantics=("arbitrary",)),
        name="moe_combine",
    )(x1, yab, yab, route, lng, lnb)


def _moe(x1, xpk, route, wg, wu, wd, lng, lnb, *, layer, alpha):
    n, d = x1.shape
    n_exp = wg.shape[1]
    n_slots = (2 * n // MOE_TILE + n_exp) * MOE_TILE
    assert n_slots // MOE_TILE <= MAX_TILES_LANES
    pos, meta = _positions(route, n_exp=n_exp, tp=min(512, n))
    src = _sc_invert(pos[0], pos[1], n_slots)
    xs = _sc_gather(xpk, src)
    ys = _experts(meta, xs, wg, wu, wd, layer)
    yab = _sc_gather(ys, pos[:2].reshape(2 * n))
    return _combine(x1, yab, route, lng, lnb, layer=layer, tn=min(512, n), alpha=alpha)


def _dup(v):
    return jnp.concatenate([v, v], axis=-1)


def kernel(x, mem, w_in, w_gate, b_gate, ssm_lam_re, ssm_lam_im, ssm_log_step, ssm_b_re, ssm_b_im, ssm_c_re, ssm_c_im, ssm_d, w_glu, b_glu, gmlp_ln_g, gmlp_ln_b, w_spatial, b_spatial, w_kv, w_br, w_out, ln1_g, ln1_b, w_router_g, b_router_g, w_router_e, b_router_e, w_exp_gate, w_exp_up, w_exp_down, ln2_g, ln2_b):
    bsz, seq, d = x.shape
    depth = w_in.shape[0]
    n = bsz * seq
    n_grp = ssm_lam_re.shape[2]
    ssm_w = n_grp * GROUP_CH
    gmlp_w = gmlp_ln_g.shape[1]
    n_heads = 4
    n_moe_grp, exp_per_grp = b_router_e.shape[1], b_router_e.shape[2]
    alpha = (2.0 * depth) ** 0.25
    tn = 512

    ls = jnp.broadcast_to(ssm_log_step[..., None], ssm_lam_re.shape)
    six = [ssm_lam_re[:, 0], ssm_lam_im[:, 0], ssm_lam_re[:, 1], ssm_lam_im[:, 1], ls[:, 0], ls[:, 1]]
    prow = jnp.stack([_dup(v) for v in six] + [jnp.zeros_like(_dup(six[0]))] * 2, axis=2)
    pcol = jnp.stack(six + [jnp.zeros_like(six[0])] * 2, axis=-1)
    bt2 = jnp.concatenate([jnp.swapaxes(ssm_b_re, 2, 3), jnp.swapaxes(ssm_b_im, 2, 3)], axis=-1)
    c2 = jnp.concatenate([ssm_c_re, ssm_c_im], axis=-1)
    ct2 = jnp.concatenate([jnp.swapaxes(ssm_c_re, 2, 3), jnp.swapaxes(ssm_c_im, 2, 3)], axis=-1)
    wtab, pst, gmat, dec = _ssm_tables(prow, pcol, bt2, c2, ct2)
    dvec = jnp.repeat(ssm_d.reshape(depth, n_grp, 1, GROUP_CH), SSM_T, axis=-1)

    assert w_in.shape[2] == 4 * ssm_w and gmlp_w == ssm_w
    w_in_b = w_in.astype(BF16)
    wssm_t = jnp.swapaxes(w_in[:, :, :ssm_w], 1, 2).astype(BF16)
    w_gate_b = w_gate.astype(BF16)
    b_gate3 = b_gate.reshape(depth, 1, -1)
    w_br_b = w_br.astype(BF16)
    w_out_b = w_out.astype(BF16)
    w_kv_b = w_kv.astype(BF16)
    w_glu_t = jnp.swapaxes(w_glu, 1, 2).astype(BF16)
    b_glu_c = b_glu.reshape(depth, -1, 1)
    w_sp_b = w_spatial.astype(BF16)
    b_sp_t = jnp.swapaxes(b_spatial, 1, 2)
    row3 = lambda v: v.reshape(depth, 1, -1)
    wrg_t = jnp.repeat(jnp.swapaxes(w_router_g, 1, 2), exp_per_grp, axis=1)
    wre_t = jnp.transpose(w_router_e, (0, 1, 3, 2)).reshape(depth, n_moe_grp * exp_per_grp, d)
    wr = jnp.concatenate([wrg_t, wre_t], axis=1)
    wr_hi = wr.astype(BF16)
    wr_hl = jnp.concatenate([wr_hi, (wr - wr_hi.astype(F32)).astype(BF16)], axis=1)
    br = jnp.concatenate([jnp.repeat(b_router_g, exp_per_grp, axis=1),
                          b_router_e.reshape(depth, -1)], axis=1).reshape(depth, -1, 1)

    n_streams = 2 if bsz % 2 == 0 else 1
    ns = n // n_streams
    xin = [x.reshape(n, d)] * n_streams
    row_off = [s * ns for s in range(n_streams)]
    for l in range(depth):
        kv = _kv_proj(mem, w_kv_b, l)
        zs, m12s = [], []
        for s in range(n_streams):
            z, m12 = _front(
                xin[s], wssm_t, w_in_b, w_gate_b, b_gate3, row3(gmlp_ln_g), row3(gmlp_ln_b), w_sp_b,
                b_sp_t, kv, w_br_b, layer=l, tn=tn, n_heads=n_heads, tokens_per_batch=seq,
                n=ns, row_off=row_off[s], batch_off=s * (bsz // n_streams))
            zs.append(z)
            m12s.append(m12)
        ys = _ssm_apply(zs, wtab, pst, gmat, dec, dvec, l, seq // LANES)
        nxt = []
        for s in range(n_streams):
            x1, xpk, route = _back(
                xin[s], m12s[s], ys[s], w_glu_t, b_glu_c, w_br_b, w_gate_b, b_gate3, w_out_b,
                row3(ln1_g), row3(ln1_b), wr_hl, br, layer=l, tn=tn, alpha=alpha, row_off=row_off[s])
            nxt.append(_moe(x1, xpk, route, w_exp_gate, w_exp_up, w_exp_down, row3(ln2_g), row3(ln2_b),
                            layer=l, alpha=alpha))
        xin = nxt
        row_off = [0] * n_streams
    return jnp.concatenate(xin, axis=0).reshape(bsz, seq, d)
```

```python
import functools
import math

import jax
import jax.numpy as jnp
from jax import lax
from jax.experimental import pallas as pl
from jax.experimental.pallas import tpu as pltpu
from jax.experimental.pallas import tpu_sc as plsc

F32 = jnp.float32
BF16 = jnp.bfloat16

LANES = 128
HALF = LANES // 2
SSM_T = HALF
GROUP_CH = 16
N_STATE = 64
LN_EPS = 1e-5
VMEM_LIMIT = 56 * 1024 * 1024

NT_DIMS = (((1,), (1,)), ((), ()))
TN_DIMS = (((0,), (0,)), ((), ()))


def _dot(a, b):
    return jnp.dot(a, b, preferred_element_type=F32)


def _dot_hi(a, b):
    return jnp.dot(a, b, preferred_element_type=F32, precision=lax.Precision.HIGHEST)


def _layer_norm(x, g, b):
    mu = jnp.mean(x, axis=-1, keepdims=True)
    xc = x - mu
    var = jnp.mean(xc * xc, axis=-1, keepdims=True)
    return xc * lax.rsqrt(var + LN_EPS) * g + b


def _lane_iota(shape):
    return lax.broadcasted_iota(jnp.int32, shape, len(shape) - 1)


def _swap_halves(x):
    return pltpu.roll(x, HALF, 1)


def _ssm_tables_kernel(prow_ref, pcol_ref, bt_ref, c_ref, ct_ref,
                       wtab_ref, pst_ref, g_ref, dec_ref):
    T = SSM_T
    prow = prow_ref[0, 0]
    pcol = pcol_ref[0, 0]
    bt2 = bt_ref[0, 0]
    c2 = c_ref[0, 0]
    ct = ct_ref[0, 0]
    lane1 = _lane_iota((1, LANES))
    lo1 = lane1 < HALF
    sgn1 = jnp.where(lo1, -1.0, 1.0).astype(F32)

    lane_bc = _lane_iota((GROUP_CH, LANES))
    lo_bc = lane_bc < HALF
    c2s = jnp.where(lo_bc, -1.0, 1.0) * _swap_halves(c2)

    bbar = []
    a_row = []
    th_row = []
    for d in range(2):
        lre = prow[2 * d:2 * d + 1]
        lim = prow[2 * d + 1:2 * d + 2]
        dt = jnp.exp(prow[4 + d:5 + d])
        a = lre * dt
        th = lim * dt
        er = jnp.exp(a)
        lb_re = er * jnp.cos(th)
        lb_im = er * jnp.sin(th)
        num_re = lb_re - 1.0
        den = lre * lre + lim * lim
        coef_re = (num_re * lre + lb_im * lim) / den
        coef_im = (lb_im * lre - num_re * lim) / den
        bbar.append(coef_re * bt2 + (sgn1 * coef_im) * _swap_halves(bt2))
        a_row.append(a)
        th_row.append(th)

    s_col = lax.broadcasted_iota(jnp.int32, (T, 1), 0).astype(F32)
    for d in range(2):
        k = (T - 1.0) - s_col if d == 0 else s_col
        e = jnp.exp(k * a_row[d])
        ang = k * th_row[d]
        pr = e * jnp.cos(ang)
        pi = e * jnp.sin(ang)
        bsw = sgn1 * _swap_halves(bbar[d])
        for hp in range(GROUP_CH):
            blk = pr * bbar[d][hp:hp + 1] + pi * bsw[hp:hp + 1]
            pst_ref[0, 0, hp * T:(hp + 1) * T, d * LANES:(d + 1) * LANES] = blk.astype(pst_ref.dtype)

    for d in range(2):
        e = jnp.exp(float(T) * a_row[d])
        ang = float(T) * th_row[d]
        dec_ref[0, 0, :, d * LANES:(d + 1) * LANES] = e * jnp.where(lo1, jnp.cos(ang), jnp.sin(ang))

    lane_p = _lane_iota((N_STATE, LANES))
    lo_p = lane_p < HALF
    t_lane = jnp.where(lo_p, lane_p, lane_p - HALF).astype(F32)
    pw = []
    for d in range(2):
        dtc = jnp.exp(pcol[:, 4 + d:5 + d])
        a_c = pcol[:, 2 * d:2 * d + 1] * dtc
        th_c = pcol[:, 2 * d + 1:2 * d + 2] * dtc
        pw.append((a_c, th_c))
    for d in range(2):
        a_c, th_c = pw[d]
        k = t_lane + 1.0 if d == 0 else float(T) - t_lane
        e = jnp.exp(k * a_c)
        pr = e * jnp.cos(k * th_c)
        pi = e * jnp.sin(k * th_c)
        for j in range(GROUP_CH // 2):
            cre = jnp.where(lo_p, ct[:, 2 * j:2 * j + 1], ct[:, 2 * j + 1:2 * j + 2])
            cim = jnp.where(lo_p, ct[:, GROUP_CH + 2 * j:GROUP_CH + 2 * j + 1],
                            ct[:, GROUP_CH + 2 * j + 1:GROUP_CH + 2 * j + 2])
            g_re = cre * pr - cim * pi
            g_im = -(cre * pi + cim * pr)
            g_ref[0, 0, d * LANES:d * LANES + N_STATE, j * LANES:(j + 1) * LANES] = g_re.astype(g_ref.dtype)
            g_ref[0, 0, d * LANES + N_STATE:(d + 1) * LANES, j * LANES:(j + 1) * LANES] = g_im.astype(g_ref.dtype)

    res = []
    kb0 = None
    for d in range(2):
        a_c, th_c = pw[d]
        if d == 0:
            k = jnp.maximum(lane_p - HALF, 0).astype(F32)
        else:
            k = jnp.maximum(HALF - lane_p, 0).astype(F32)
        e = jnp.exp(k * a_c)
        rhs = jnp.concatenate([e * jnp.cos(k * th_c), -(e * jnp.sin(k * th_c))], axis=0)
        bsw = _swap_halves(bbar[d])
        b_re = jnp.where(lo_bc, bbar[d], bsw)
        b_im = jnp.where(lo_bc, bsw, bbar[d])
        bc = (b_re[:, None, :] * c2[None, :, :] + b_im[:, None, :] * c2s[None, :, :])
        bc = bc.reshape(GROUP_CH * GROUP_CH, LANES)
        res.append(_dot_hi(bc, rhs))
        if d == 1:
            lane_bcf = _lane_iota(bc.shape)
            kb0 = jnp.sum(jnp.where(lane_bcf < HALF, bc, 0.0), axis=1, keepdims=True)
    lane_w = _lane_iota(res[0].shape)
    wtab = jnp.where(lane_w >= HALF, res[0], res[1])
    wtab_ref[0, 0] = wtab + jnp.where(lane_w == HALF, kb0, 0.0)


def _ssm_tables(prow, pcol, bt2, c2, ct2):
    L, G = prow.shape[0], prow.shape[1]
    T = SSM_T
    blk = lambda shape: pl.BlockSpec((1, 1) + shape, lambda l, g: (l, g, 0, 0))
    return pl.pallas_call(
        _ssm_tables_kernel,
        grid=(L, G),
        in_specs=[blk((8, LANES)), blk((N_STATE, 8)), blk((GROUP_CH, LANES)),
                  blk((GROUP_CH, LANES)), blk((N_STATE, 2 * GROUP_CH))],
        out_specs=[blk((GROUP_CH * GROUP_CH, LANES)), blk((GROUP_CH * T, 2 * LANES)),
                   blk((2 * LANES, GROUP_CH * T)), blk((1, 2 * LANES))],
        out_shape=[jax.ShapeDtypeStruct((L, G, GROUP_CH * GROUP_CH, LANES), F32),
                   jax.ShapeDtypeStruct((L, G, GROUP_CH * T, 2 * LANES), BF16),
                   jax.ShapeDtypeStruct((L, G, 2 * LANES, GROUP_CH * T), BF16),
                   jax.ShapeDtypeStruct((L, G, 1, 2 * LANES), F32)],
        name="ssm_tables",
    )(prow, pcol, bt2, c2, ct2)


def _cmul_packed(x, d_re, d_sw):
    return x * d_re + _swap_halves(x) * d_sw


def _ssm_kernel(*refs, n_streams, rows_per_batch):
    z_refs = refs[:n_streams]
    wtab_ref, pst_ref, g_ref, dec_ref, dvec_ref = refs[n_streams:n_streams + 5]
    o_refs = refs[n_streams + 5:2 * n_streams + 5]
    m_ref = refs[2 * n_streams + 5]
    T = SSM_T
    rows_s = z_refs[0].shape[1] // GROUP_CH
    n_rows = n_streams * rows_s
    lane_t = _lane_iota((T, LANES))
    lo_t = lane_t < HALF

    def build(hp, carry):
        r0 = pl.multiple_of(hp * T, T)
        for j in range(GROUP_CH // 2):
            wa = jnp.broadcast_to(wtab_ref[0, 0, pl.ds(hp * GROUP_CH + 2 * j, 1), :], (T, LANES))
            wb = jnp.broadcast_to(wtab_ref[0, 0, pl.ds(hp * GROUP_CH + 2 * j + 1, 1), :], (T, LANES))
            ra = pltpu.roll(wa, HALF, 1, stride=1, stride_axis=0)
            rb = pltpu.roll(wb, 0, 1, stride=1, stride_axis=0)
            m_ref[pl.ds(r0, T), j * LANES:(j + 1) * LANES] = jnp.where(lo_t, ra, rb).astype(m_ref.dtype)
        return carry

    lax.fori_loop(0, GROUP_CH, build, 0)

    lane = _lane_iota((n_rows, LANES))
    lo = lane < HALF
    a0, a1 = [], []
    for j in range(GROUP_CH // 2):
        pe = jnp.concatenate([z[0, pl.ds(2 * j, rows_s, stride=GROUP_CH), :] for z in z_refs], axis=0)
        po = jnp.concatenate([z[0, pl.ds(2 * j + 1, rows_s, stride=GROUP_CH), :] for z in z_refs], axis=0)
        a0.append(jnp.where(lo, pe, _swap_halves(po)))
        a1.append(jnp.where(lo, _swap_halves(pe), po))
    a0 = jnp.concatenate(a0, axis=1)
    a1 = jnp.concatenate(a1, axis=1)
    a_f32 = jnp.concatenate([a0, a1], axis=0)
    a_bf = a_f32.astype(BF16)

    out = _dot(a_bf, m_ref[...]) + dvec_ref[0, 0] * a_f32
    s = _dot(a_bf, pst_ref[0, 0])
    sf0, sb0 = s[:n_rows, :LANES], s[:n_rows, LANES:]
    sf1, sb1 = s[n_rows:, :LANES], s[n_rows:, LANES:]

    sgn = jnp.where(_lane_iota((1, LANES)) < HALF, -1.0, 1.0).astype(F32)
    lo1 = _lane_iota((1, LANES)) < HALF

    def split(p):
        sw = _swap_halves(p)
        return jnp.where(lo1, p, sw), sgn * jnp.where(lo1, sw, p)

    def square(p):
        d_re, d_sw = split(p)
        return _cmul_packed(p, d_re, d_sw)

    dec = dec_ref[0, 0]
    dec_f, dec_b = dec[:, :LANES], dec[:, LANES:]
    df_re, df_sw = split(dec_f)
    db_re, db_sw = split(dec_b)

    row = lax.broadcasted_iota(jnp.int32, (n_rows, LANES), 0)
    rib = row % rows_per_batch

    ef = _cmul_packed(sf0, df_re, df_sw) + sf1
    eb = sb0 + _cmul_packed(sb1, db_re, db_sw)
    pf, pb = square(dec_f), square(dec_b)
    step = 1
    while step < rows_per_batch:
        pf_re, pf_sw = split(pf)
        pb_re, pb_sw = split(pb)
        shf = jnp.where(rib >= step, pltpu.roll(ef, step, 0), 0.0)
        ef = ef + _cmul_packed(shf, pf_re, pf_sw)
        shb = jnp.where(rib < rows_per_batch - step, pltpu.roll(eb, n_rows - step, 0), 0.0)
        eb = eb + _cmul_packed(shb, pb_re, pb_sw)
        pf, pb = square(pf), square(pb)
        step *= 2
    hf0 = jnp.where(rib >= 1, pltpu.roll(ef, 1, 0), 0.0)
    hf1 = _cmul_packed(hf0, df_re, df_sw) + sf0
    hb1 = jnp.where(rib < rows_per_batch - 1, pltpu.roll(eb, n_rows - 1, 0), 0.0)
    hb0 = sb1 + _cmul_packed(hb1, db_re, db_sw)
    h_in = jnp.concatenate([jnp.concatenate([hf0, hb0], axis=1),
                            jnp.concatenate([hf1, hb1], axis=1)], axis=0)
    out = out + _dot(h_in.astype(BF16), g_ref[0, 0])

    o0, o1 = out[:n_rows], out[n_rows:]
    for j in range(GROUP_CH // 2):
        t0 = o0[:, j * LANES:(j + 1) * LANES]
        t1 = o1[:, j * LANES:(j + 1) * LANES]
        even = jnp.where(lo, t0, _swap_halves(t1))
        odd = jnp.where(lo, _swap_halves(t0), t1)
        for s, o_ref in enumerate(o_refs):
            o_ref[0, pl.ds(2 * j, rows_s, stride=GROUP_CH), :] = even[s * rows_s:(s + 1) * rows_s]
            o_ref[0, pl.ds(2 * j + 1, rows_s, stride=GROUP_CH), :] = odd[s * rows_s:(s + 1) * rows_s]


def _ssm_apply(zs, wtab, pst, gmat, dec, dvec, layer, rows_per_batch):
    G, R, _ = zs[0].shape
    T = SSM_T
    n_streams = len(zs)
    tab = lambda shape: pl.BlockSpec((1, 1) + shape, lambda g: (layer, g, 0, 0))
    seq_block = pl.BlockSpec((1, R, LANES), lambda g: (g, 0, 0))
    return pl.pallas_call(
        functools.partial(_ssm_kernel, n_streams=n_streams, rows_per_batch=rows_per_batch),
        grid=(G,),
        in_specs=[seq_block] * n_streams + [
            tab((GROUP_CH * GROUP_CH, LANES)), tab((GROUP_CH * T, 2 * LANES)),
            tab((2 * LANES, GROUP_CH * T)), tab((1, 2 * LANES)), tab((1, GROUP_CH * T))],
        out_specs=[seq_block] * n_streams,
        out_shape=[jax.ShapeDtypeStruct(zs[0].shape, F32)] * n_streams,
        scratch_shapes=[pltpu.VMEM((GROUP_CH * T, GROUP_CH * T), BF16)],
        compiler_params=pltpu.CompilerParams(dimension_semantics=("arbitrary",),
                                             vmem_limit_bytes=VMEM_LIMIT),
        name="ssm_seq",
    )(*zs, wtab, pst, gmat, dec, dvec)


SSM_GROUPS_PER_STEP = 2


def _toeplitz_build(m_ref, wtab_ref, k):
    T = SSM_T
    lo_t = _lane_iota((T, LANES)) < HALF
    for hp in range(GROUP_CH):
        for j in range(GROUP_CH // 2):
            r = hp * GROUP_CH + 2 * j
            wa = jnp.broadcast_to(wtab_ref[0, k, r:r + 1, :], (T, LANES))
            wb = jnp.broadcast_to(wtab_ref[0, k, r + 1:r + 2, :], (T, LANES))
            ra = pltpu.roll(wa, HALF, 1, stride=1, stride_axis=0)
            rb = pltpu.roll(wb, 0, 1, stride=1, stride_axis=0)
            m_ref[hp * T:(hp + 1) * T, j * LANES:(j + 1) * LANES] = (
                jnp.where(lo_t, ra, rb).astype(m_ref.dtype))


def _ssm_group(z_refs, o_refs, k, m_ref, pst, gmat, dec, dvec, rows_per_batch):
    rows_s = z_refs[0].shape[1] // GROUP_CH
    n_rows = len(z_refs) * rows_s
    lane = _lane_iota((n_rows, LANES))
    lo = lane < HALF
    a0, a1 = [], []
    for j in range(GROUP_CH // 2):
        pe = jnp.concatenate([z[k, pl.ds(2 * j, rows_s, stride=GROUP_CH), :] for z in z_refs], axis=0)
        po = jnp.concatenate([z[k, pl.ds(2 * j + 1, rows_s, stride=GROUP_CH), :] for z in z_refs], axis=0)
        a0.append(jnp.where(lo, pe, _swap_halves(po)))
        a1.append(jnp.where(lo, _swap_halves(pe), po))
    a_f32 = jnp.concatenate([jnp.concatenate(a0, axis=1), jnp.concatenate(a1, axis=1)], axis=0)
    a_bf = a_f32.astype(BF16)

    out = _dot(a_bf, m_ref[...]) + dvec * a_f32
    s = _dot(a_bf, pst)
    sf0, sb0 = s[:n_rows, :LANES], s[:n_rows, LANES:]
    sf1, sb1 = s[n_rows:, :LANES], s[n_rows:, LANES:]

    lo1 = _lane_iota((1, LANES)) < HALF
    sgn = jnp.where(lo1, -1.0, 1.0).astype(F32)

    def split(p):
        sw = _swap_halves(p)
        return jnp.where(lo1, p, sw), sgn * jnp.where(lo1, sw, p)

    def square(p):
        d_re, d_sw = split(p)
        return _cmul_packed(p, d_re, d_sw)

    dec_f, dec_b = dec[:, :LANES], dec[:, LANES:]
    df_re, df_sw = split(dec_f)
    db_re, db_sw = split(dec_b)

    row = lax.broadcasted_iota(jnp.int32, (n_rows, LANES), 0)
    rib = row % rows_per_batch

    ef = _cmul_packed(sf0, df_re, df_sw) + sf1
    eb = sb0 + _cmul_packed(sb1, db_re, db_sw)
    pf, pb = square(dec_f), square(dec_b)
    step = 1
    while step < rows_per_batch:
        pf_re, pf_sw = split(pf)
        pb_re, pb_sw = split(pb)
        shf = jnp.where(rib >= step, pltpu.roll(ef, step, 0), 0.0)
        ef = ef + _cmul_packed(shf, pf_re, pf_sw)
        shb = jnp.where(rib < rows_per_batch - step, pltpu.roll(eb, n_rows - step, 0), 0.0)
        eb = eb + _cmul_packed(shb, pb_re, pb_sw)
        pf, pb = square(pf), square(pb)
        step *= 2
    hf0 = jnp.where(rib >= 1, pltpu.roll(ef, 1, 0), 0.0)
    hf1 = _cmul_packed(hf0, df_re, df_sw) + sf0
    hb1 = jnp.where(rib < rows_per_batch - 1, pltpu.roll(eb, n_rows - 1, 0), 0.0)
    hb0 = sb1 + _cmul_packed(hb1, db_re, db_sw)
    h_in = jnp.concatenate([jnp.concatenate([hf0, hb0], axis=1),
                            jnp.concatenate([hf1, hb1], axis=1)], axis=0)
    out = out + _dot(h_in.astype(BF16), gmat)

    o0, o1 = out[:n_rows], out[n_rows:]
    for j in range(GROUP_CH // 2):
        t0 = o0[:, j * LANES:(j + 1) * LANES]
        t1 = o1[:, j * LANES:(j + 1) * LANES]
        even = jnp.where(lo, t0, _swap_halves(t1))
        odd = jnp.where(lo, _swap_halves(t0), t1)
        for si, o_ref in enumerate(o_refs):
            o_ref[k, pl.ds(2 * j, rows_s, stride=GROUP_CH), :] = even[si * rows_s:(si + 1) * rows_s]
            o_ref[k, pl.ds(2 * j + 1, rows_s, stride=GROUP_CH), :] = odd[si * rows_s:(si + 1) * rows_s]


def _ssm_pair_kernel(*refs, n_streams, rows_per_batch):
    z_refs = refs[:n_streams]
    wtab_ref, wnext_ref, pst_ref, g_ref, dec_ref, dvec_ref = refs[n_streams:n_streams + 6]
    o_refs = refs[n_streams + 6:2 * n_streams + 6]
    m_a, m_b = refs[2 * n_streams + 6:2 * n_streams + 8]

    @pl.when(pl.program_id(0) == 0)
    def _():
        _toeplitz_build(m_a, wtab_ref, 0)

    def group(k, m_ref):
        _ssm_group(z_refs, o_refs, k, m_ref, pst_ref[0, k], g_ref[0, k], dec_ref[0, k], dvec_ref[0, k],
                   rows_per_batch)

    _toeplitz_build(m_b, wtab_ref, 1)
    group(0, m_a)
    _toeplitz_build(m_a, wnext_ref, 0)
    group(1, m_b)


def _ssm_apply_pairs(zs, wtab, pst, gmat, dec, dvec, layer, rows_per_batch):
    G, R, _ = zs[0].shape
    T = SSM_T
    gp = SSM_GROUPS_PER_STEP
    n_streams = len(zs)
    tab = lambda shape: pl.BlockSpec((1, gp) + shape, lambda g: (layer, g, 0, 0))
    nxt = pl.BlockSpec((1, 1, GROUP_CH * GROUP_CH, LANES),
                       lambda g: (layer, jnp.minimum(gp * g + gp, G - 1), 0, 0))
    seq_block = pl.BlockSpec((gp, R, LANES), lambda g: (g, 0, 0))
    return pl.pallas_call(
        functools.partial(_ssm_pair_kernel, n_streams=n_streams, rows_per_batch=rows_per_batch),
        grid=(G // gp,),
        in_specs=[seq_block] * n_streams + [
            tab((GROUP_CH * GROUP_CH, LANES)), nxt, tab((GROUP_CH * T, 2 * LANES)),
            tab((2 * LANES, GROUP_CH * T)), tab((1, 2 * LANES)), tab((1, GROUP_CH * T))],
        out_specs=[seq_block] * n_streams,
        out_shape=[jax.ShapeDtypeStruct(zs[0].shape, F32)] * n_streams,
        scratch_shapes=[pltpu.VMEM((GROUP_CH * T, GROUP_CH * T), BF16)] * 2,
        compiler_params=pltpu.CompilerParams(dimension_semantics=("arbitrary",),
                                             vmem_limit_bytes=VMEM_LIMIT),
        name="ssm_seq",
    )(*zs, wtab, wtab, pst, gmat, dec, dvec)


def _kv_kernel(mem_ref, w_ref, o_ref):
    o_ref[0] = _dot(mem_ref[0].astype(BF16), w_ref[0]).astype(o_ref.dtype)


def _kv_proj(mem, w_kv, layer):
    B, M, D = mem.shape
    width = w_kv.shape[2]
    return pl.pallas_call(
        _kv_kernel,
        grid=(B,),
        in_specs=[pl.BlockSpec((1, M, D), lambda b: (b, 0, 0)),
                  pl.BlockSpec((1, D, width), lambda b: (layer, 0, 0))],
        out_specs=pl.BlockSpec((1, M, width), lambda b: (b, 0, 0)),
        out_shape=jax.ShapeDtypeStruct((B, M, width), BF16),
        name="kv_proj",
    )(mem, w_kv)


def _front_kernel(x_ref, wssm_t_ref, wu_ref, wv_ref, wq_ref, wg1_ref, wg2_ref, bg1_ref, bg2_ref,
                  lng_ref, lnb_ref, ws_ref, bs_t_ref, kv_ref, wbr1_ref, wbr2_ref, z_ref, m_ref, *, n_heads):
    tn, d_model = x_ref.shape
    xb = x_ref[...].astype(BF16)

    u_t = lax.dot_general(wssm_t_ref[0], xb, NT_DIMS, preferred_element_type=F32)
    n_grp = u_t.shape[0] // GROUP_CH
    for a in range(tn // LANES):
        z_ref[:, a * GROUP_CH:(a + 1) * GROUP_CH, :] = (
            u_t[:, a * LANES:(a + 1) * LANES].reshape(n_grp, GROUP_CH, LANES))

    u = jax.nn.gelu(_dot(xb, wu_ref[0]))
    v = jax.nn.gelu(_dot(xb, wv_ref[0]))
    half = v.shape[1]
    vb = _layer_norm(v, lng_ref[0], lnb_ref[0]).astype(BF16)
    lane = _lane_iota((LANES, LANES))
    lo = lane < HALF
    bs_t = bs_t_ref[0]
    s_rows = []
    for a in range(tn // LANES):
        va = vb[a * LANES:(a + 1) * LANES]
        tiles = []
        for j in range(half // LANES):
            rhs = va[:, j * LANES:(j + 1) * LANES]
            ev = _dot(ws_ref[0, 2 * j], rhs)
            od = _dot(ws_ref[0, 2 * j + 1], rhs)
            bias = jnp.where(lo, bs_t[:, 2 * j:2 * j + 1], bs_t[:, 2 * j + 1:2 * j + 2])
            tiles.append(jnp.where(lo, ev, od) + bias)
        s_rows.append(jnp.concatenate(tiles, axis=1))
    y_gmlp = u * jnp.concatenate(s_rows, axis=0)

    q = _dot(xb, wq_ref[0])
    hd = q.shape[1] // n_heads
    kv = kv_ref[0]
    outs = []
    for h in range(n_heads):
        qh = q[:, h * hd:(h + 1) * hd].astype(BF16)
        kh = kv[:, h * hd:(h + 1) * hd]
        vh = kv[:, q.shape[1] + h * hd:q.shape[1] + (h + 1) * hd]
        sc = lax.dot_general(qh, kh, NT_DIMS, preferred_element_type=F32) * (hd ** -0.5)
        sc = sc - jnp.max(sc, axis=-1, keepdims=True)
        p = jnp.exp(sc)
        p = p / jnp.sum(p, axis=-1, keepdims=True)
        outs.append(_dot(p.astype(BF16), vh))
    y_mem = jnp.concatenate(outs, axis=1)

    g1 = jax.nn.sigmoid(_dot(xb, wg1_ref[0]) + bg1_ref[0])
    g2 = jax.nn.sigmoid(_dot(xb, wg2_ref[0]) + bg2_ref[0])
    m_ref[...] = (g1 * _dot(y_gmlp.astype(BF16), wbr1_ref[0, 0])
                  + g2 * _dot(y_mem.astype(BF16), wbr2_ref[0, 0]))


def _layer_block(arr, layer, col_block=None, col=0, branch=None):
    if branch is not None:
        return pl.BlockSpec((1, 1) + arr.shape[2:], lambda *_: (layer, branch, 0, 0))
    shape = arr.shape[1:]
    if col_block is not None:
        shape = shape[:-1] + (col_block,)
    nd = len(shape)
    return pl.BlockSpec((1,) + shape, lambda *_: (layer,) + (0,) * (nd - 1) + (col,))


def _front(x2, wssm_t, w_in, w_gate, b_gate, lng, lnb, ws, bs_t, kv, w_br, *, layer, tn, n_heads,
           tokens_per_batch, n, row_off, batch_off):
    d = x2.shape[1]
    n_grp = wssm_t.shape[1] // GROUP_CH
    width = wssm_t.shape[1]
    rows = tn // LANES * GROUP_CH
    tiles_per_batch = tokens_per_batch // tn
    tile_off = row_off // tn
    lb = functools.partial(_layer_block, layer=layer)
    return pl.pallas_call(
        functools.partial(_front_kernel, n_heads=n_heads),
        grid=(n // tn,),
        in_specs=[pl.BlockSpec((tn, d), lambda i: (i + tile_off, 0)),
                  lb(wssm_t),
                  lb(w_in, col_block=width, col=1), lb(w_in, col_block=width, col=2),
                  lb(w_in, col_block=width, col=3),
                  lb(w_gate, col_block=d, col=1), lb(w_gate, col_block=d, col=2),
                  lb(b_gate, col_block=d, col=1), lb(b_gate, col_block=d, col=2),
                  lb(lng), lb(lnb), lb(ws), lb(bs_t),
                  pl.BlockSpec((1,) + kv.shape[1:], lambda i: (i // tiles_per_batch + batch_off, 0, 0)),
                  lb(w_br, branch=1), lb(w_br, branch=2)],
        out_specs=[pl.BlockSpec((n_grp, rows, LANES), lambda i: (0, i, 0)),
                   pl.BlockSpec((tn, d), lambda i: (i, 0))],
        out_shape=[jax.ShapeDtypeStruct((n_grp, n // LANES * GROUP_CH, LANES), F32),
                   jax.ShapeDtypeStruct((n, d), F32)],
        compiler_params=pltpu.CompilerParams(dimension_semantics=("arbitrary",),
                                             vmem_limit_bytes=VMEM_LIMIT),
        name="mixer_front",
    )(x2, wssm_t, w_in, w_in, w_in, w_gate, w_gate, b_gate, b_gate, lng, lnb, ws, bs_t, kv, w_br, w_br)


def _route_t(lg, le):
    n_exp = float(le.shape[0])
    sub = lax.broadcasted_iota(jnp.int32, le.shape, 0).astype(F32)
    gmax = jnp.max(lg, axis=0, keepdims=True)
    denom = jnp.sum(jnp.exp(lg - gmax), axis=0, keepdims=True) * (1.0 / 8.0)
    pg_sel = 1.0 / denom
    first = jnp.min(jnp.where(lg == gmax, sub, n_exp), axis=0, keepdims=True)
    in_grp = jnp.floor(sub * 0.125) == jnp.floor(first * 0.125)
    neg = -jnp.inf
    l1 = jnp.where(in_grp, le, neg)
    m1 = jnp.max(l1, axis=0, keepdims=True)
    i1 = jnp.min(jnp.where(l1 == m1, sub, n_exp), axis=0, keepdims=True)
    l2 = jnp.where(sub == i1, neg, l1)
    m2 = jnp.max(l2, axis=0, keepdims=True)
    i2 = jnp.min(jnp.where(l2 == m2, sub, n_exp), axis=0, keepdims=True)
    e2 = jnp.exp(m2 - m1)
    p1 = 1.0 / (1.0 + e2)
    p2 = e2 / (1.0 + e2)
    return i1, i2, pg_sel * p1, pg_sel * p2


def _pack_bf16_pairs(x):
    c = x.shape[1] // 2
    rounded = x.astype(BF16).astype(F32)
    bits = lax.bitcast_convert_type(rounded, jnp.uint32)
    word = (bits[:, :c] >> 16) | bits[:, c:]
    return lax.bitcast_convert_type(word, jnp.int32)


def _unpack_bf16_pairs(w):
    bits = lax.bitcast_convert_type(w, jnp.uint32)
    lo = lax.bitcast_convert_type(bits << 16, F32)
    hi = lax.bitcast_convert_type(bits & jnp.uint32(0xFFFF0000), F32)
    return jnp.concatenate([lo, hi], axis=1)


def _back_kernel(x_ref, m12_ref, y_ref, wglu_t_ref, bglu_ref, wbr0_ref, wg0_ref, bg0_ref, wout_ref,
                 lng_ref, lnb_ref, wr_ref, br_ref, x1_ref, xpk_ref, route_ref, *, alpha):
    tn = x_ref.shape[0]
    x = x_ref[...]
    xb = x.astype(BF16)
    n_grp = y_ref.shape[0]
    cols = []
    for a in range(tn // LANES):
        y_t = y_ref[:, a * GROUP_CH:(a + 1) * GROUP_CH, :].reshape(n_grp * GROUP_CH, LANES)
        y_t = jax.nn.gelu(y_t)
        gate = jax.nn.sigmoid(_dot(wglu_t_ref[0], y_t.astype(BF16)) + bglu_ref[0])
        cols.append((y_t * gate).astype(BF16))
    ys_t = jnp.concatenate(cols, axis=1)
    br0 = lax.dot_general(ys_t, wbr0_ref[0, 0], TN_DIMS, preferred_element_type=F32)
    g0 = jax.nn.sigmoid(_dot(xb, wg0_ref[0]) + bg0_ref[0])
    merged = m12_ref[...] + g0 * br0
    h = _dot(merged.astype(BF16), wout_ref[0])
    x1 = _layer_norm(alpha * x + h, lng_ref[0], lnb_ref[0])
    x1_ref[...] = x1
    xpk_ref[...] = _pack_bf16_pairs(x1)
    x_hi = x1.astype(BF16)
    x_lo = (x1 - x_hi.astype(F32)).astype(BF16)
    w_hl = wr_ref[0]
    n_r = w_hl.shape[0] // 2
    part = lax.dot_general(w_hl, x_hi, NT_DIMS, preferred_element_type=F32)
    logits = (part[:n_r] + part[n_r:]
              + lax.dot_general(w_hl[:n_r], x_lo, NT_DIMS, preferred_element_type=F32) + br_ref[0])
    n_exp = n_r // 2
    i1, i2, w1, w2 = _route_t(logits[:n_exp], logits[n_exp:])
    route_ref[...] = jnp.concatenate([i1, i2, w1, w2, jnp.zeros((4, tn), F32)], axis=0)


def _back(x2, m12, y, wglu_t, bglu, w_br, w_gate, b_gate, wout, lng, lnb, wr, br, *, layer, tn, alpha,
          row_off):
    n, d = m12.shape
    n_grp = y.shape[0]
    rows = tn // LANES * GROUP_CH
    tile_off = row_off // tn
    lb = functools.partial(_layer_block, layer=layer)
    return pl.pallas_call(
        functools.partial(_back_kernel, alpha=alpha),
        grid=(n // tn,),
        in_specs=[pl.BlockSpec((tn, d), lambda i: (i + tile_off, 0)),
                  pl.BlockSpec((tn, d), lambda i: (i, 0)),
                  pl.BlockSpec((n_grp, rows, LANES), lambda i: (0, i, 0)),
                  lb(wglu_t), lb(bglu), lb(w_br, branch=0),
                  lb(w_gate, col_block=d, col=0), lb(b_gate, col_block=d, col=0), lb(wout),
                  lb(lng), lb(lnb), lb(wr), lb(br)],
        out_specs=[pl.BlockSpec((tn, d), lambda i: (i, 0)),
                   pl.BlockSpec((tn, d // 2), lambda i: (i, 0)),
                   pl.BlockSpec((8, tn), lambda i: (0, i))],
        out_shape=[jax.ShapeDtypeStruct((n, d), F32),
                   jax.ShapeDtypeStruct((n, d // 2), jnp.int32),
                   jax.ShapeDtypeStruct((8, n), F32)],
        compiler_params=pltpu.CompilerParams(dimension_semantics=("arbitrary",),
                                             vmem_limit_bytes=VMEM_LIMIT),
        name="mixer_back",
    )(x2, m12, y, wglu_t, bglu, w_br, w_gate, b_gate, wout, lng, lnb, wr, br)


MOE_TILE = 256
MAX_TILES_LANES = 256


def _pos_kernel(route_ref, pos_ref, meta_ref, cnt_ref, offs_ref, carry_ref, *, n_exp):
    phase = pl.program_id(0)
    i = pl.program_id(1)
    tp = route_ref.shape[1]
    sub = lax.broadcasted_iota(jnp.int32, (n_exp, tp), 0).astype(F32)
    i1 = route_ref[0:1, :]
    i2 = route_ref[1:2, :]
    sel = jnp.where((sub == i1) | (sub == i2), 1.0, 0.0)
    tile_cnt = jnp.sum(sel, axis=1, keepdims=True)

    @pl.when((phase == 0) & (i == 0))
    def _():
        cnt_ref[...] = jnp.zeros_like(cnt_ref)

    @pl.when(phase == 0)
    def _():
        cnt_ref[...] += jnp.broadcast_to(tile_cnt, cnt_ref.shape)

    @pl.when((phase == 1) & (i == 0))
    def _():
        cnt = cnt_ref[...]
        padded = jnp.ceil(cnt * (1.0 / MOE_TILE)) * float(MOE_TILE)
        sub_e = lax.broadcasted_iota(jnp.int32, cnt.shape, 0)
        lane_e = lax.broadcasted_iota(jnp.int32, cnt.shape, 1)
        row = jnp.sum(jnp.where(sub_e == lane_e, padded, 0.0), axis=0, keepdims=True)
        offs = jnp.sum(jnp.where(lane_e < sub_e, row, 0.0), axis=1, keepdims=True)
        offs_ref[...] = jnp.broadcast_to(offs, offs_ref.shape)
        carry_ref[...] = jnp.zeros_like(carry_ref)
        total = jnp.sum(jnp.where(lane_e < n_exp, row, 0.0), axis=1, keepdims=True)[0:1]
        t = lax.broadcasted_iota(jnp.int32, (n_exp, MAX_TILES_LANES), 1).astype(F32) * float(MOE_TILE)
        texp = jnp.sum(jnp.where(offs <= t, 1.0, 0.0), axis=0, keepdims=True) - 1.0
        valid = jnp.where(t[0:1] < total, 1.0, 0.0)
        lane_t = _lane_iota(texp.shape)
        first = jnp.where((lane_t == 0) | (texp != pltpu.roll(texp, 1, 1)), 1.0, 0.0)
        meta = jnp.concatenate([texp, valid, first, jnp.zeros((5, MAX_TILES_LANES), F32)], axis=0)
        meta_ref[...] = meta.astype(jnp.int32)

    @pl.when(phase == 1)
    def _():
        r = lax.broadcasted_iota(jnp.int32, (tp, tp), 0)
        c = lax.broadcasted_iota(jnp.int32, (tp, tp), 1)
        upper = jnp.where(r < c, 1.0, 0.0).astype(BF16)
        rank = _dot(sel.astype(BF16), upper)
        slot = offs_ref[:, 0:1] + carry_ref[:, 0:1] + rank
        pos_a = jnp.sum(jnp.where(sub == i1, slot, 0.0), axis=0, keepdims=True)
        pos_b = jnp.sum(jnp.where(sub == i2, slot, 0.0), axis=0, keepdims=True)
        pos = jnp.concatenate([pos_a, pos_b, jnp.zeros((6, tp), F32)], axis=0)
        pos_ref[...] = pos.astype(jnp.int32)
        carry_ref[...] += jnp.broadcast_to(tile_cnt, carry_ref.shape)


def _positions(route, *, n_exp, tp):
    n = route.shape[1]
    return pl.pallas_call(
        functools.partial(_pos_kernel, n_exp=n_exp),
        grid=(2, n // tp),
        in_specs=[pl.BlockSpec((8, tp), lambda p, i: (0, i))],
        out_specs=[pl.BlockSpec((8, tp), lambda p, i: (0, i * p)),
                   pl.BlockSpec((8, MAX_TILES_LANES), lambda p, i: (0, 0))],
        out_shape=[jax.ShapeDtypeStruct((8, n), jnp.int32),
                   jax.ShapeDtypeStruct((8, MAX_TILES_LANES), jnp.int32)],
        scratch_shapes=[pltpu.VMEM((n_exp, LANES), F32)] * 3,
        compiler_params=pltpu.CompilerParams(dimension_semantics=("arbitrary", "arbitrary")),
        name="moe_positions",
    )(route)


SC_CORES = 2
SC_SUBCORES = 16
SC_LANES = 16
SC_GATHER_ROWS = 64


def _sc_mesh():
    return plsc.VectorSubcoreMesh(core_axis_name="c", subcore_axis_name="s",
                                  num_cores=SC_CORES, num_subcores=SC_SUBCORES)


def _sc_invert(pos_a, pos_b, n_slots):
    n = pos_a.shape[0]

    def body(pa_hbm, pb_hbm, src_hbm, pa_v, pb_v, src_v):
        wid = lax.axis_index("s") * SC_CORES + lax.axis_index("c")

        @pl.when(wid == 0)
        def _():
            pltpu.sync_copy(pa_hbm, pa_v)
            pltpu.sync_copy(pb_hbm, pb_v)
            lane = lax.iota(jnp.int32, SC_LANES)

            @pl.loop(0, n_slots // SC_LANES)
            def _(j):
                off = pl.multiple_of(j * SC_LANES, SC_LANES)
                src_v[pl.ds(off, SC_LANES)] = lax.rem(lane + off, n)

            @pl.loop(0, n // SC_LANES)
            def _(j):
                off = pl.multiple_of(j * SC_LANES, SC_LANES)
                tok = lane + off
                plsc.store_scatter(src_v, [pa_v[pl.ds(off, SC_LANES)]], tok)
                plsc.store_scatter(src_v, [pb_v[pl.ds(off, SC_LANES)]], tok)

            pltpu.sync_copy(src_v, src_hbm)

    return pl.kernel(
        body, out_type=jax.ShapeDtypeStruct((n_slots,), jnp.int32), mesh=_sc_mesh(),
        scratch_types=[pltpu.VMEM((n,), jnp.int32), pltpu.VMEM((n,), jnp.int32),
                       pltpu.VMEM((n_slots,), jnp.int32)],
        compiler_params=pltpu.CompilerParams(needs_layout_passes=False),
        name="moe_invert",
    )(pos_a, pos_b)


def _sc_gather(table, idx):
    m = idx.shape[0]
    d = table.shape[1]
    n_workers = SC_CORES * SC_SUBCORES
    per_w = m // n_workers
    rows = SC_GATHER_ROWS

    n_pairs = per_w // (2 * rows)
    assert per_w == n_pairs * 2 * rows

    def body(table_hbm, idx_hbm, out_hbm, idx_v, buf_v, sems):
        wid = lax.axis_index("s") * SC_CORES + lax.axis_index("c")
        base = pl.multiple_of(wid * per_w, rows)
        pltpu.sync_copy(idx_hbm.at[pl.ds(base, per_w)], idx_v)

        def gather(chunk, slot):
            off = pl.multiple_of(chunk * rows, rows)
            return pltpu.make_async_copy(table_hbm.at[idx_v.at[pl.ds(off, rows)]], buf_v.at[slot],
                                         sems.at[slot])

        def write(chunk, slot):
            off = pl.multiple_of(chunk * rows, rows)
            pltpu.sync_copy(buf_v.at[slot], out_hbm.at[pl.ds(base + off, rows)])

        gather(0, 0).start()

        @pl.loop(0, n_pairs)
        def _(p):
            c0 = 2 * p
            gather(c0, 0).wait()
            gather(c0 + 1, 1).start()
            write(c0, 0)
            gather(c0 + 1, 1).wait()

            @pl.when(p + 1 < n_pairs)
            def _():
                gather(c0 + 2, 0).start()

            write(c0 + 1, 1)

    return pl.kernel(
        body, out_type=jax.ShapeDtypeStruct((m, d), table.dtype), mesh=_sc_mesh(),
        scratch_types=[pltpu.VMEM((per_w,), jnp.int32), pltpu.VMEM((2, rows, d), table.dtype),
                       pltpu.SemaphoreType.DMA((2,))],
        name="moe_gather",
    )(table, idx)


def _expert_kernel(meta_ref, xs_ref, wg_ref, wu_ref, wd_ref, ys_ref, wg_s, wu_s, wd_s):
    t = pl.program_id(0)
    valid = meta_ref[1, t] == 1

    @pl.when(valid & (meta_ref[2, t] == 1))
    def _():
        wg_s[...] = wg_ref[0, 0].astype(BF16)
        wu_s[...] = wu_ref[0, 0].astype(BF16)
        wd_s[...] = wd_ref[0, 0].astype(BF16)

    @pl.when(valid)
    def _():
        xb = _unpack_bf16_pairs(xs_ref[...]).astype(BF16)
        h = jax.nn.silu(_dot(xb, wg_s[...])) * _dot(xb, wu_s[...])
        ys_ref[...] = _pack_bf16_pairs(_dot(h.astype(BF16), wd_s[...]))

    @pl.when(jnp.logical_not(valid))
    def _():
        ys_ref[...] = jnp.zeros_like(ys_ref)


def _experts(meta, xs, wg, wu, wd, layer):
    n_slots, dp = xs.shape
    d, f = wg.shape[2], wg.shape[3]
    grid_spec = pltpu.PrefetchScalarGridSpec(
        num_scalar_prefetch=1,
        grid=(n_slots // MOE_TILE,),
        in_specs=[pl.BlockSpec((MOE_TILE, dp), lambda t, meta: (t, 0)),
                  pl.BlockSpec((1, 1, d, f), lambda t, meta: (layer, meta[0, t], 0, 0)),
                  pl.BlockSpec((1, 1, d, f), lambda t, meta: (layer, meta[0, t], 0, 0)),
                  pl.BlockSpec((1, 1, f, d), lambda t, meta: (layer, meta[0, t], 0, 0))],
        out_specs=pl.BlockSpec((MOE_TILE, dp), lambda t, meta: (t, 0)),
        scratch_shapes=[pltpu.VMEM((d, f), BF16), pltpu.VMEM((d, f), BF16), pltpu.VMEM((f, d), BF16)],
    )
    return pl.pallas_call(
        _expert_kernel, grid_spec=grid_spec,
        out_shape=jax.ShapeDtypeStruct((n_slots, dp), jnp.int32),
        compiler_params=pltpu.CompilerParams(dimension_semantics=("arbitrary",),
                                             vmem_limit_bytes=VMEM_LIMIT),
        name="moe_experts",
    )(meta, xs, wg, wu, wd)


def _combine_kernel(x_ref, ya_ref, yb_ref, route_ref, lng_ref, lnb_ref, o_ref, *, alpha):
    w = route_ref[...].T
    moe = w[:, 2:3] * _unpack_bf16_pairs(ya_ref[...]) + w[:, 3:4] * _unpack_bf16_pairs(yb_ref[...])
    o_ref[...] = _layer_norm(alpha * x_ref[...] + moe, lng_ref[0], lnb_ref[0])


def _combine(x1, yab, route, lng, lnb, *, layer, tn, alpha):
    n, d = x1.shape
    row = pl.BlockSpec((tn, d), lambda i: (i, 0))
    tiles = n // tn
    return pl.pallas_call(
        functools.partial(_combine_kernel, alpha=alpha),
        grid=(tiles,),
        in_specs=[row,
                  pl.BlockSpec((tn, d // 2), lambda i: (i, 0)),
                  pl.BlockSpec((tn, d // 2), lambda i: (i + tiles, 0)),
                  pl.BlockSpec((8, tn), lambda i: (0, i)),
                  _layer_block(lng, layer), _layer_block(lnb, layer)],
        out_specs=row,
        out_shape=jax.ShapeDtypeStruct((n, d), F32),
        compiler_params=pltpu.CompilerParams(dimension_semantics=("arbitrary",)),
        name="moe_combine",
    )(x1, yab, yab, route, lng, lnb)


def _moe(x1, xpk, route, wg, wu, wd, lng, lnb, *, layer, alpha):
    n, d = x1.shape
    n_exp = wg.shape[1]
    n_slots = (2 * n // MOE_TILE + n_exp) * MOE_TILE
    assert n_slots // MOE_TILE <= MAX_TILES_LANES
    pos, meta = _positions(route, n_exp=n_exp, tp=min(512, n))
    src = _sc_invert(pos[0], pos[1], n_slots)
    xs = _sc_gather(xpk, src)
    ys = _experts(meta, xs, wg, wu, wd, layer)
    yab = _sc_gather(ys, pos[:2].reshape(2 * n))
    return _combine(x1, yab, route, lng, lnb, layer=layer, tn=min(512, n), alpha=alpha)


def _dup(v):
    return jnp.concatenate([v, v], axis=-1)


def kernel(x, mem, w_in, w_gate, b_gate, ssm_lam_re, ssm_lam_im, ssm_log_step, ssm_b_re, ssm_b_im, ssm_c_re, ssm_c_im, ssm_d, w_glu, b_glu, gmlp_ln_g, gmlp_ln_b, w_spatial, b_spatial, w_kv, w_br, w_out, ln1_g, ln1_b, w_router_g, b_router_g, w_router_e, b_router_e, w_exp_gate, w_exp_up, w_exp_down, ln2_g, ln2_b):
    bsz, seq, d = x.shape
    depth = w_in.shape[0]
    n = bsz * seq
    n_grp = ssm_lam_re.shape[2]
    ssm_w = n_grp * GROUP_CH
    gmlp_w = gmlp_ln_g.shape[1]
    n_heads = 4
    n_moe_grp, exp_per_grp = b_router_e.shape[1], b_router_e.shape[2]
    alpha = (2.0 * depth) ** 0.25
    tn = 512

    ls = jnp.broadcast_to(ssm_log_step[..., None], ssm_lam_re.shape)
    six = [ssm_lam_re[:, 0], ssm_lam_im[:, 0], ssm_lam_re[:, 1], ssm_lam_im[:, 1], ls[:, 0], ls[:, 1]]
    prow = jnp.stack([_dup(v) for v in six] + [jnp.zeros_like(_dup(six[0]))] * 2, axis=2)
    pcol = jnp.stack(six + [jnp.zeros_like(six[0])] * 2, axis=-1)
    bt2 = jnp.concatenate([jnp.swapaxes(ssm_b_re, 2, 3), jnp.swapaxes(ssm_b_im, 2, 3)], axis=-1)
    c2 = jnp.concatenate([ssm_c_re, ssm_c_im], axis=-1)
    ct2 = jnp.concatenate([jnp.swapaxes(ssm_c_re, 2, 3), jnp.swapaxes(ssm_c_im, 2, 3)], axis=-1)
    wtab, pst, gmat, dec = _ssm_tables(prow, pcol, bt2, c2, ct2)
    dvec = jnp.repeat(ssm_d.reshape(depth, n_grp, 1, GROUP_CH), SSM_T, axis=-1)

    assert w_in.shape[2] == 4 * ssm_w and gmlp_w == ssm_w
    w_in_b = w_in.astype(BF16)
    wssm_t = jnp.swapaxes(w_in[:, :, :ssm_w], 1, 2).astype(BF16)
    w_gate_b = w_gate.astype(BF16)
    b_gate3 = b_gate.reshape(depth, 1, -1)
    w_br_b = w_br.astype(BF16)
    w_out_b = w_out.astype(BF16)
    w_kv_b = w_kv.astype(BF16)
    w_glu_t = jnp.swapaxes(w_glu, 1, 2).astype(BF16)
    b_glu_c = b_glu.reshape(depth, -1, 1)
    w_sp_b = w_spatial.astype(BF16)
    b_sp_t = jnp.swapaxes(b_spatial, 1, 2)
    row3 = lambda v: v.reshape(depth, 1, -1)
    wrg_t = jnp.repeat(jnp.swapaxes(w_router_g, 1, 2), exp_per_grp, axis=1)
    wre_t = jnp.transpose(w_router_e, (0, 1, 3, 2)).reshape(depth, n_moe_grp * exp_per_grp, d)
    wr = jnp.concatenate([wrg_t, wre_t], axis=1)
    wr_hi = wr.astype(BF16)
    wr_hl = jnp.concatenate([wr_hi, (wr - wr_hi.astype(F32)).astype(BF16)], axis=1)
    br = jnp.concatenate([jnp.repeat(b_router_g, exp_per_grp, axis=1),
                          b_router_e.reshape(depth, -1)], axis=1).reshape(depth, -1, 1)

    n_streams = 1
    ns = n // n_streams
    xin = [x.reshape(n, d)] * n_streams
    row_off = [s * ns for s in range(n_streams)]
    for l in range(depth):
        kv = _kv_proj(mem, w_kv_b, l)
        zs, m12s = [], []
        for s in range(n_streams):
            z, m12 = _front(
                xin[s], wssm_t, w_in_b, w_gate_b, b_gate3, row3(gmlp_ln_g), row3(gmlp_ln_b), w_sp_b,
                b_sp_t, kv, w_br_b, layer=l, tn=tn, n_heads=n_heads, tokens_per_batch=seq,
                n=ns, row_off=row_off[s], batch_off=s * (bsz // n_streams))
            zs.append(z)
            m12s.append(m12)
        ys = _ssm_apply_pairs(zs, wtab, pst, gmat, dec, dvec, l, seq // LANES)
        nxt = []
        for s in range(n_streams):
            x1, xpk, route = _back(
                xin[s], m12s[s], ys[s], w_glu_t, b_glu_c, w_br_b, w_gate_b, b_gate3, w_out_b,
                row3(ln1_g), row3(ln1_b), wr_hl, br, layer=l, tn=tn, alpha=alpha, row_off=row_off[s])
            nxt.append(_moe(x1, xpk, route, w_exp_gate, w_exp_up, w_exp_down, row3(ln2_g), row3(ln2_b),
                            layer=l, alpha=alpha))
        xin = nxt
        row_off = [0] * n_streams
    return jnp.concatenate(xin, axis=0).reshape(bsz, seq, d)
```

```python
import functools
import math

import jax
import jax.numpy as jnp
from jax import lax
from jax.experimental import pallas as pl
from jax.experimental.pallas import tpu as pltpu
from jax.experimental.pallas import tpu_sc as plsc

F32 = jnp.float32
BF16 = jnp.bfloat16

LANES = 128
HALF = LANES // 2
SSM_T = HALF
GROUP_CH = 16
N_STATE = 64
LN_EPS = 1e-5
VMEM_LIMIT = 56 * 1024 * 1024

NT_DIMS = (((1,), (1,)), ((), ()))
TN_DIMS = (((0,), (0,)), ((), ()))


def _dot(a, b):
    return jnp.dot(a, b, preferred_element_type=F32)


def _dot_hi(a, b):
    return jnp.dot(a, b, preferred_element_type=F32, precision=lax.Precision.HIGHEST)


def _layer_norm(x, g, b):
    mu = jnp.mean(x, axis=-1, keepdims=True)
    xc = x - mu
    var = jnp.mean(xc * xc, axis=-1, keepdims=True)
    return xc * lax.rsqrt(var + LN_EPS) * g + b


def _lane_iota(shape):
    return lax.broadcasted_iota(jnp.int32, shape, len(shape) - 1)


def _swap_halves(x):
    return pltpu.roll(x, HALF, 1)


def _ssm_tables_kernel(prow_ref, pcol_ref, bt_ref, c_ref, ct_ref,
                       wtab_ref, pst_ref, g_ref, dec_ref):
    T = SSM_T
    prow = prow_ref[0, 0]
    pcol = pcol_ref[0, 0]
    bt2 = bt_ref[0, 0]
    c2 = c_ref[0, 0]
    ct = ct_ref[0, 0]
    lane1 = _lane_iota((1, LANES))
    lo1 = lane1 < HALF
    sgn1 = jnp.where(lo1, -1.0, 1.0).astype(F32)

    lane_bc = _lane_iota((GROUP_CH, LANES))
    lo_bc = lane_bc < HALF
    c2s = jnp.where(lo_bc, -1.0, 1.0) * _swap_halves(c2)

    bbar = []
    a_row = []
    th_row = []
    for d in range(2):
        lre = prow[2 * d:2 * d + 1]
        lim = prow[2 * d + 1:2 * d + 2]
        dt = jnp.exp(prow[4 + d:5 + d])
        a = lre * dt
        th = lim * dt
        er = jnp.exp(a)
        lb_re = er * jnp.cos(th)
        lb_im = er * jnp.sin(th)
        num_re = lb_re - 1.0
        den = lre * lre + lim * lim
        coef_re = (num_re * lre + lb_im * lim) / den
        coef_im = (lb_im * lre - num_re * lim) / den
        bbar.append(coef_re * bt2 + (sgn1 * coef_im) * _swap_halves(bt2))
        a_row.append(a)
        th_row.append(th)

    s_col = lax.broadcasted_iota(jnp.int32, (T, 1), 0).astype(F32)
    for d in range(2):
        k = (T - 1.0) - s_col if d == 0 else s_col
        e = jnp.exp(k * a_row[d])
        ang = k * th_row[d]
        pr = e * jnp.cos(ang)
        pi = e * jnp.sin(ang)
        bsw = sgn1 * _swap_halves(bbar[d])
        for hp in range(GROUP_CH):
            blk = pr * bbar[d][hp:hp + 1] + pi * bsw[hp:hp + 1]
            pst_ref[0, 0, hp * T:(hp + 1) * T, d * LANES:(d + 1) * LANES] = blk.astype(pst_ref.dtype)

    for d in range(2):
        e = jnp.exp(float(T) * a_row[d])
        ang = float(T) * th_row[d]
        dec_ref[0, 0, :, d * LANES:(d + 1) * LANES] = e * jnp.where(lo1, jnp.cos(ang), jnp.sin(ang))

    lane_p = _lane_iota((N_STATE, LANES))
    lo_p = lane_p < HALF
    t_lane = jnp.where(lo_p, lane_p, lane_p - HALF).astype(F32)
    pw = []
    for d in range(2):
        dtc = jnp.exp(pcol[:, 4 + d:5 + d])
        a_c = pcol[:, 2 * d:2 * d + 1] * dtc
        th_c = pcol[:, 2 * d + 1:2 * d + 2] * dtc
        pw.append((a_c, th_c))
    for d in range(2):
        a_c, th_c = pw[d]
        k = t_lane + 1.0 if d == 0 else float(T) - t_lane
        e = jnp.exp(k * a_c)
        pr = e * jnp.cos(k * th_c)
        pi = e * jnp.sin(k * th_c)
        for j in range(GROUP_CH // 2):
            cre = jnp.where(lo_p, ct[:, 2 * j:2 * j + 1], ct[:, 2 * j + 1:2 * j + 2])
            cim = jnp.where(lo_p, ct[:, GROUP_CH + 2 * j:GROUP_CH + 2 * j + 1],
                            ct[:, GROUP_CH + 2 * j + 1:GROUP_CH + 2 * j + 2])
            g_re = cre * pr - cim * pi
            g_im = -(cre * pi + cim * pr)
            g_ref[0, 0, d * LANES:d * LANES + N_STATE, j * LANES:(j + 1) * LANES] = g_re.astype(g_ref.dtype)
            g_ref[0, 0, d * LANES + N_STATE:(d + 1) * LANES, j * LANES:(j + 1) * LANES] = g_im.astype(g_ref.dtype)

    res = []
    kb0 = None
    for d in range(2):
        a_c, th_c = pw[d]
        if d == 0:
            k = jnp.maximum(lane_p - HALF, 0).astype(F32)
        else:
            k = jnp.maximum(HALF - lane_p, 0).astype(F32)
        e = jnp.exp(k * a_c)
        rhs = jnp.concatenate([e * jnp.cos(k * th_c), -(e * jnp.sin(k * th_c))], axis=0)
        bsw = _swap_halves(bbar[d])
        b_re = jnp.where(lo_bc, bbar[d], bsw)
        b_im = jnp.where(lo_bc, bsw, bbar[d])
        bc = (b_re[:, None, :] * c2[None, :, :] + b_im[:, None, :] * c2s[None, :, :])
        bc = bc.reshape(GROUP_CH * GROUP_CH, LANES)
        res.append(_dot_hi(bc, rhs))
        if d == 1:
            lane_bcf = _lane_iota(bc.shape)
            kb0 = jnp.sum(jnp.where(lane_bcf < HALF, bc, 0.0), axis=1, keepdims=True)
    lane_w = _lane_iota(res[0].shape)
    wtab = jnp.where(lane_w >= HALF, res[0], res[1])
    wtab_ref[0, 0] = wtab + jnp.where(lane_w == HALF, kb0, 0.0)


def _ssm_tables(prow, pcol, bt2, c2, ct2):
    L, G = prow.shape[0], prow.shape[1]
    T = SSM_T
    blk = lambda shape: pl.BlockSpec((1, 1) + shape, lambda l, g: (l, g, 0, 0))
    return pl.pallas_call(
        _ssm_tables_kernel,
        grid=(L, G),
        in_specs=[blk((8, LANES)), blk((N_STATE, 8)), blk((GROUP_CH, LANES)),
                  blk((GROUP_CH, LANES)), blk((N_STATE, 2 * GROUP_CH))],
        out_specs=[blk((GROUP_CH * GROUP_CH, LANES)), blk((GROUP_CH * T, 2 * LANES)),
                   blk((2 * LANES, GROUP_CH * T)), blk((1, 2 * LANES))],
        out_shape=[jax.ShapeDtypeStruct((L, G, GROUP_CH * GROUP_CH, LANES), F32),
                   jax.ShapeDtypeStruct((L, G, GROUP_CH * T, 2 * LANES), BF16),
                   jax.ShapeDtypeStruct((L, G, 2 * LANES, GROUP_CH * T), BF16),
                   jax.ShapeDtypeStruct((L, G, 1, 2 * LANES), F32)],
        name="ssm_tables",
    )(prow, pcol, bt2, c2, ct2)


def _cmul_packed(x, d_re, d_sw):
    return x * d_re + _swap_halves(x) * d_sw


SSM_GROUPS_PER_STEP = 2


def _toeplitz_build(m_ref, wtab_ref, k):
    T = SSM_T
    lo_t = _lane_iota((T, LANES)) < HALF
    for hp in range(GROUP_CH):
        for j in range(GROUP_CH // 2):
            r = hp * GROUP_CH + 2 * j
            wa = jnp.broadcast_to(wtab_ref[0, k, r:r + 1, :], (T, LANES))
            wb = jnp.broadcast_to(wtab_ref[0, k, r + 1:r + 2, :], (T, LANES))
            ra = pltpu.roll(wa, HALF, 1, stride=1, stride_axis=0)
            rb = pltpu.roll(wb, 0, 1, stride=1, stride_axis=0)
            m_ref[hp * T:(hp + 1) * T, j * LANES:(j + 1) * LANES] = (
                jnp.where(lo_t, ra, rb).astype(m_ref.dtype))


def _ssm_group(z_refs, o_refs, k, m_ref, pst, gmat, dec, dvec, rows_per_batch):
    rows_s = z_refs[0].shape[1] // GROUP_CH
    n_rows = len(z_refs) * rows_s
    lane = _lane_iota((n_rows, LANES))
    lo = lane < HALF
    a0, a1 = [], []
    for j in range(GROUP_CH // 2):
        pe = jnp.concatenate([z[k, pl.ds(2 * j, rows_s, stride=GROUP_CH), :] for z in z_refs], axis=0)
        po = jnp.concatenate([z[k, pl.ds(2 * j + 1, rows_s, stride=GROUP_CH), :] for z in z_refs], axis=0)
        a0.append(jnp.where(lo, pe, _swap_halves(po)))
        a1.append(jnp.where(lo, _swap_halves(pe), po))
    a_f32 = jnp.concatenate([jnp.concatenate(a0, axis=1), jnp.concatenate(a1, axis=1)], axis=0)
    a_bf = a_f32.astype(BF16)

    out = _dot(a_bf, m_ref[...]) + dvec * a_f32
    s = _dot(a_bf, pst)
    sf0, sb0 = s[:n_rows, :LANES], s[:n_rows, LANES:]
    sf1, sb1 = s[n_rows:, :LANES], s[n_rows:, LANES:]

    lo1 = _lane_iota((1, LANES)) < HALF
    sgn = jnp.where(lo1, -1.0, 1.0).astype(F32)

    def split(p):
        sw = _swap_halves(p)
        return jnp.where(lo1, p, sw), sgn * jnp.where(lo1, sw, p)

    def square(p):
        d_re, d_sw = split(p)
        return _cmul_packed(p, d_re, d_sw)

    dec_f, dec_b = dec[:, :LANES], dec[:, LANES:]
    df_re, df_sw = split(dec_f)
    db_re, db_sw = split(dec_b)

    row = lax.broadcasted_iota(jnp.int32, (n_rows, LANES), 0)
    rib = row % rows_per_batch

    ef = _cmul_packed(sf0, df_re, df_sw) + sf1
    eb = sb0 + _cmul_packed(sb1, db_re, db_sw)
    pf, pb = square(dec_f), square(dec_b)
    step = 1
    while step < rows_per_batch:
        pf_re, pf_sw = split(pf)
        pb_re, pb_sw = split(pb)
        shf = jnp.where(rib >= step, pltpu.roll(ef, step, 0), 0.0)
        ef = ef + _cmul_packed(shf, pf_re, pf_sw)
        shb = jnp.where(rib < rows_per_batch - step, pltpu.roll(eb, n_rows - step, 0), 0.0)
        eb = eb + _cmul_packed(shb, pb_re, pb_sw)
        pf, pb = square(pf), square(pb)
        step *= 2
    hf0 = jnp.where(rib >= 1, pltpu.roll(ef, 1, 0), 0.0)
    hf1 = _cmul_packed(hf0, df_re, df_sw) + sf0
    hb1 = jnp.where(rib < rows_per_batch - 1, pltpu.roll(eb, n_rows - 1, 0), 0.0)
    hb0 = sb1 + _cmul_packed(hb1, db_re, db_sw)
    h_in = jnp.concatenate([jnp.concatenate([hf0, hb0], axis=1),
                            jnp.concatenate([hf1, hb1], axis=1)], axis=0)
    out = out + _dot(h_in.astype(BF16), gmat)

    o0, o1 = out[:n_rows], out[n_rows:]
    for j in range(GROUP_CH // 2):
        t0 = o0[:, j * LANES:(j + 1) * LANES]
        t1 = o1[:, j * LANES:(j + 1) * LANES]
        even = jnp.where(lo, t0, _swap_halves(t1))
        odd = jnp.where(lo, _swap_halves(t0), t1)
        for si, o_ref in enumerate(o_refs):
            o_ref[k, pl.ds(2 * j, rows_s, stride=GROUP_CH), :] = even[si * rows_s:(si + 1) * rows_s]
            o_ref[k, pl.ds(2 * j + 1, rows_s, stride=GROUP_CH), :] = odd[si * rows_s:(si + 1) * rows_s]


def _ssm_pair_kernel(*refs, n_streams, rows_per_batch):
    z_refs = refs[:n_streams]
    wtab_ref, wnext_ref, pst_ref, g_ref, dec_ref, dvec_ref = refs[n_streams:n_streams + 6]
    o_refs = refs[n_streams + 6:2 * n_streams + 6]
    m_a, m_b = refs[2 * n_streams + 6:2 * n_streams + 8]

    @pl.when(pl.program_id(0) == 0)
    def _():
        _toeplitz_build(m_a, wtab_ref, 0)

    def group(k, m_ref):
        _ssm_group(z_refs, o_refs, k, m_ref, pst_ref[0, k], g_ref[0, k], dec_ref[0, k], dvec_ref[0, k],
                   rows_per_batch)

    _toeplitz_build(m_b, wtab_ref, 1)
    group(0, m_a)
    _toeplitz_build(m_a, wnext_ref, 0)
    group(1, m_b)


def _ssm_apply_pairs(zs, wtab, pst, gmat, dec, dvec, layer, rows_per_batch):
    G, R, _ = zs[0].shape
    T = SSM_T
    gp = SSM_GROUPS_PER_STEP
    n_streams = len(zs)
    tab = lambda shape: pl.BlockSpec((1, gp) + shape, lambda g: (layer, g, 0, 0))
    nxt = pl.BlockSpec((1, 1, GROUP_CH * GROUP_CH, LANES),
                       lambda g: (layer, jnp.minimum(gp * g + gp, G - 1), 0, 0))
    seq_block = pl.BlockSpec((gp, R, LANES), lambda g: (g, 0, 0))
    return pl.pallas_call(
        functools.partial(_ssm_pair_kernel, n_streams=n_streams, rows_per_batch=rows_per_batch),
        grid=(G // gp,),
        in_specs=[seq_block] * n_streams + [
            tab((GROUP_CH * GROUP_CH, LANES)), nxt, tab((GROUP_CH * T, 2 * LANES)),
            tab((2 * LANES, GROUP_CH * T)), tab((1, 2 * LANES)), tab((1, GROUP_CH * T))],
        out_specs=[seq_block] * n_streams,
        out_shape=[jax.ShapeDtypeStruct(zs[0].shape, F32)] * n_streams,
        scratch_shapes=[pltpu.VMEM((GROUP_CH * T, GROUP_CH * T), BF16)] * 2,
        compiler_params=pltpu.CompilerParams(dimension_semantics=("arbitrary",),
                                             vmem_limit_bytes=VMEM_LIMIT),
        name="ssm_seq",
    )(*zs, wtab, wtab, pst, gmat, dec, dvec)


def _kv_kernel(mem_ref, w_ref, o_ref):
    o_ref[0] = _dot(mem_ref[0].astype(BF16), w_ref[0]).astype(o_ref.dtype)


def _kv_proj(mem, w_kv, layer):
    B, M, D = mem.shape
    width = w_kv.shape[2]
    return pl.pallas_call(
        _kv_kernel,
        grid=(B,),
        in_specs=[pl.BlockSpec((1, M, D), lambda b: (b, 0, 0)),
                  pl.BlockSpec((1, D, width), lambda b: (layer, 0, 0))],
        out_specs=pl.BlockSpec((1, M, width), lambda b: (b, 0, 0)),
        out_shape=jax.ShapeDtypeStruct((B, M, width), BF16),
        name="kv_proj",
    )(mem, w_kv)


def _front_kernel(x_ref, wssm_t_ref, wu_ref, wv_ref, wq_ref, wg1_ref, wg2_ref, bg1_ref, bg2_ref,
                  lng_ref, lnb_ref, ws_ref, bs_t_ref, kv_ref, wbr1_ref, wbr2_ref, z_ref, m_ref, *, n_heads):
    tn, d_model = x_ref.shape
    xb = x_ref[...].astype(BF16)

    u_t = lax.dot_general(wssm_t_ref[0], xb, NT_DIMS, preferred_element_type=F32)
    n_grp = u_t.shape[0] // GROUP_CH
    for a in range(tn // LANES):
        z_ref[:, a * GROUP_CH:(a + 1) * GROUP_CH, :] = (
            u_t[:, a * LANES:(a + 1) * LANES].reshape(n_grp, GROUP_CH, LANES))

    u = jax.nn.gelu(_dot(xb, wu_ref[0]))
    v = jax.nn.gelu(_dot(xb, wv_ref[0]))
    half = v.shape[1]
    vb = _layer_norm(v, lng_ref[0], lnb_ref[0]).astype(BF16)
    lane = _lane_iota((LANES, LANES))
    lo = lane < HALF
    bs_t = bs_t_ref[0]
    s_rows = []
    for a in range(tn // LANES):
        va = vb[a * LANES:(a + 1) * LANES]
        tiles = []
        for j in range(half // LANES):
            rhs = va[:, j * LANES:(j + 1) * LANES]
            ev = _dot(ws_ref[0, 2 * j], rhs)
            od = _dot(ws_ref[0, 2 * j + 1], rhs)
            bias = jnp.where(lo, bs_t[:, 2 * j:2 * j + 1], bs_t[:, 2 * j + 1:2 * j + 2])
            tiles.append(jnp.where(lo, ev, od) + bias)
        s_rows.append(jnp.concatenate(tiles, axis=1))
    y_gmlp = u * jnp.concatenate(s_rows, axis=0)

    q = _dot(xb, wq_ref[0])
    hd = q.shape[1] // n_heads
    kv = kv_ref[0]
    outs = []
    for h in range(n_heads):
        qh = q[:, h * hd:(h + 1) * hd].astype(BF16)
        kh = kv[:, h * hd:(h + 1) * hd]
        vh = kv[:, q.shape[1] + h * hd:q.shape[1] + (h + 1) * hd]
        sc = lax.dot_general(qh, kh, NT_DIMS, preferred_element_type=F32) * (hd ** -0.5)
        sc = sc - jnp.max(sc, axis=-1, keepdims=True)
        p = jnp.exp(sc)
        p = p / jnp.sum(p, axis=-1, keepdims=True)
        outs.append(_dot(p.astype(BF16), vh))
    y_mem = jnp.concatenate(outs, axis=1)

    g1 = jax.nn.sigmoid(_dot(xb, wg1_ref[0]) + bg1_ref[0])
    g2 = jax.nn.sigmoid(_dot(xb, wg2_ref[0]) + bg2_ref[0])
    m_ref[...] = (g1 * _dot(y_gmlp.astype(BF16), wbr1_ref[0, 0])
                  + g2 * _dot(y_mem.astype(BF16), wbr2_ref[0, 0]))


def _layer_block(arr, layer, col_block=None, col=0, branch=None):
    if branch is not None:
        return pl.BlockSpec((1, 1) + arr.shape[2:], lambda *_: (layer, branch, 0, 0))
    shape = arr.shape[1:]
    if col_block is not None:
        shape = shape[:-1] + (col_block,)
    nd = len(shape)
    return pl.BlockSpec((1,) + shape, lambda *_: (layer,) + (0,) * (nd - 1) + (col,))


def _front(x2, wssm_t, w_in, w_gate, b_gate, lng, lnb, ws, bs_t, kv, w_br, *, layer, tn, n_heads,
           tokens_per_batch, n, row_off, batch_off):
    d = x2.shape[1]
    n_grp = wssm_t.shape[1] // GROUP_CH
    width = wssm_t.shape[1]
    rows = tn // LANES * GROUP_CH
    tiles_per_batch = tokens_per_batch // tn
    tile_off = row_off // tn
    lb = functools.partial(_layer_block, layer=layer)
    return pl.pallas_call(
        functools.partial(_front_kernel, n_heads=n_heads),
        grid=(n // tn,),
        in_specs=[pl.BlockSpec((tn, d), lambda i: (i + tile_off, 0)),
                  lb(wssm_t),
                  lb(w_in, col_block=width, col=1), lb(w_in, col_block=width, col=2),
                  lb(w_in, col_block=width, col=3),
                  lb(w_gate, col_block=d, col=1), lb(w_gate, col_block=d, col=2),
                  lb(b_gate, col_block=d, col=1), lb(b_gate, col_block=d, col=2),
                  lb(lng), lb(lnb), lb(ws), lb(bs_t),
                  pl.BlockSpec((1,) + kv.shape[1:], lambda i: (i // tiles_per_batch + batch_off, 0, 0)),
                  lb(w_br, branch=1), lb(w_br, branch=2)],
        out_specs=[pl.BlockSpec((n_grp, rows, LANES), lambda i: (0, i, 0)),
                   pl.BlockSpec((tn, d), lambda i: (i, 0))],
        out_shape=[jax.ShapeDtypeStruct((n_grp, n // LANES * GROUP_CH, LANES), F32),
                   jax.ShapeDtypeStruct((n, d), F32)],
        compiler_params=pltpu.CompilerParams(dimension_semantics=("arbitrary",),
                                             vmem_limit_bytes=VMEM_LIMIT),
        name="mixer_front",
    )(x2, wssm_t, w_in, w_in, w_in, w_gate, w_gate, b_gate, b_gate, lng, lnb, ws, bs_t, kv, w_br, w_br)


def _route_t(lg, le):
    n_exp = float(le.shape[0])
    sub = lax.broadcasted_iota(jnp.int32, le.shape, 0).astype(F32)
    gmax = jnp.max(lg, axis=0, keepdims=True)
    denom = jnp.sum(jnp.exp(lg - gmax), axis=0, keepdims=True) * (1.0 / 8.0)
    pg_sel = 1.0 / denom
    first = jnp.min(jnp.where(lg == gmax, sub, n_exp), axis=0, keepdims=True)
    in_grp = jnp.floor(sub * 0.125) == jnp.floor(first * 0.125)
    neg = -jnp.inf
    l1 = jnp.where(in_grp, le, neg)
    m1 = jnp.max(l1, axis=0, keepdims=True)
    i1 = jnp.min(jnp.where(l1 == m1, sub, n_exp), axis=0, keepdims=True)
    l2 = jnp.where(sub == i1, neg, l1)
    m2 = jnp.max(l2, axis=0, keepdims=True)
    i2 = jnp.min(jnp.where(l2 == m2, sub, n_exp), axis=0, keepdims=True)
    e2 = jnp.exp(m2 - m1)
    p1 = 1.0 / (1.0 + e2)
    p2 = e2 / (1.0 + e2)
    return i1, i2, pg_sel * p1, pg_sel * p2


def _pack_bf16_pairs(x):
    c = x.shape[1] // 2
    rounded = x.astype(BF16).astype(F32)
    bits = lax.bitcast_convert_type(rounded, jnp.uint32)
    word = (bits[:, :c] >> 16) | bits[:, c:]
    return lax.bitcast_convert_type(word, jnp.int32)


def _unpack_bf16_pairs(w):
    bits = lax.bitcast_convert_type(w, jnp.uint32)
    lo = lax.bitcast_convert_type(bits << 16, F32)
    hi = lax.bitcast_convert_type(bits & jnp.uint32(0xFFFF0000), F32)
    return jnp.concatenate([lo, hi], axis=1)


def _back_kernel(x_ref, m12_ref, y_ref, wglu_t_ref, bglu_ref, wbr0_ref, wg0_ref, bg0_ref, wout_ref,
                 lng_ref, lnb_ref, wr_ref, br_ref, x1_ref, xpk_ref, route_ref, *, alpha):
    tn = x_ref.shape[0]
    x = x_ref[...]
    xb = x.astype(BF16)
    n_grp = y_ref.shape[0]
    cols = []
    for a in range(tn // LANES):
        y_t = y_ref[:, a * GROUP_CH:(a + 1) * GROUP_CH, :].reshape(n_grp * GROUP_CH, LANES)
        y_t = jax.nn.gelu(y_t)
        gate = jax.nn.sigmoid(_dot(wglu_t_ref[0], y_t.astype(BF16)) + bglu_ref[0])
        cols.append((y_t * gate).astype(BF16))
    ys_t = jnp.concatenate(cols, axis=1)
    br0 = lax.dot_general(ys_t, wbr0_ref[0, 0], TN_DIMS, preferred_element_type=F32)
    g0 = jax.nn.sigmoid(_dot(xb, wg0_ref[0]) + bg0_ref[0])
    merged = m12_ref[...] + g0 * br0
    h = _dot(merged.astype(BF16), wout_ref[0])
    x1 = _layer_norm(alpha * x + h, lng_ref[0], lnb_ref[0])
    x1_ref[...] = x1
    xpk_ref[...] = _pack_bf16_pairs(x1)
    x_hi = x1.astype(BF16)
    x_lo = (x1 - x_hi.astype(F32)).astype(BF16)
    w_hl = wr_ref[0]
    n_r = w_hl.shape[0] // 2
    part = lax.dot_general(w_hl, x_hi, NT_DIMS, preferred_element_type=F32)
    logits = (part[:n_r] + part[n_r:]
              + lax.dot_general(w_hl[:n_r], x_lo, NT_DIMS, preferred_element_type=F32) + br_ref[0])
    n_exp = n_r // 2
    i1, i2, w1, w2 = _route_t(logits[:n_exp], logits[n_exp:])
    route_ref[...] = jnp.concatenate([i1, i2, w1, w2, jnp.zeros((4, tn), F32)], axis=0)


def _back(x2, m12, y, wglu_t, bglu, w_br, w_gate, b_gate, wout, lng, lnb, wr, br, *, layer, tn, alpha,
          row_off):
    n, d = m12.shape
    n_grp = y.shape[0]
    rows = tn // LANES * GROUP_CH
    tile_off = row_off // tn
    lb = functools.partial(_layer_block, layer=layer)
    return pl.pallas_call(
        functools.partial(_back_kernel, alpha=alpha),
        grid=(n // tn,),
        in_specs=[pl.BlockSpec((tn, d), lambda i: (i + tile_off, 0)),
                  pl.BlockSpec((tn, d), lambda i: (i, 0)),
                  pl.BlockSpec((n_grp, rows, LANES), lambda i: (0, i, 0)),
                  lb(wglu_t), lb(bglu), lb(w_br, branch=0),
                  lb(w_gate, col_block=d, col=0), lb(b_gate, col_block=d, col=0), lb(wout),
                  lb(lng), lb(lnb), lb(wr), lb(br)],
        out_specs=[pl.BlockSpec((tn, d), lambda i: (i, 0)),
                   pl.BlockSpec((tn, d // 2), lambda i: (i, 0)),
                   pl.BlockSpec((8, tn), lambda i: (0, i))],
        out_shape=[jax.ShapeDtypeStruct((n, d), F32),
                   jax.ShapeDtypeStruct((n, d // 2), jnp.int32),
                   jax.ShapeDtypeStruct((8, n), F32)],
        compiler_params=pltpu.CompilerParams(dimension_semantics=("arbitrary",),
                                             vmem_limit_bytes=VMEM_LIMIT),
        name="mixer_back",
    )(x2, m12, y, wglu_t, bglu, w_br, w_gate, b_gate, wout, lng, lnb, wr, br)


MOE_TILE = 256
MAX_TILES_LANES = 256


def _pos_kernel(route_ref, pos_ref, meta_ref, cnt_ref, offs_ref, carry_ref, *, n_exp):
    phase = pl.program_id(0)
    i = pl.program_id(1)
    tp = route_ref.shape[1]
    sub = lax.broadcasted_iota(jnp.int32, (n_exp, tp), 0).astype(F32)
    i1 = route_ref[0:1, :]
    i2 = route_ref[1:2, :]
    sel = jnp.where((sub == i1) | (sub == i2), 1.0, 0.0)
    tile_cnt = jnp.sum(sel, axis=1, keepdims=True)

    @pl.when((phase == 0) & (i == 0))
    def _():
        cnt_ref[...] = jnp.zeros_like(cnt_ref)

    @pl.when(phase == 0)
    def _():
        cnt_ref[...] += jnp.broadcast_to(tile_cnt, cnt_ref.shape)

    @pl.when((phase == 1) & (i == 0))
    def _():
        cnt = cnt_ref[...]
        padded = jnp.ceil(cnt * (1.0 / MOE_TILE)) * float(MOE_TILE)
        sub_e = lax.broadcasted_iota(jnp.int32, cnt.shape, 0)
        lane_e = lax.broadcasted_iota(jnp.int32, cnt.shape, 1)
        row = jnp.sum(jnp.where(sub_e == lane_e, padded, 0.0), axis=0, keepdims=True)
        offs = jnp.sum(jnp.where(lane_e < sub_e, row, 0.0), axis=1, keepdims=True)
        offs_ref[...] = jnp.broadcast_to(offs, offs_ref.shape)
        carry_ref[...] = jnp.zeros_like(carry_ref)
        total = jnp.sum(jnp.where(lane_e < n_exp, row, 0.0), axis=1, keepdims=True)[0:1]
        t = lax.broadcasted_iota(jnp.int32, (n_exp, MAX_TILES_LANES), 1).astype(F32) * float(MOE_TILE)
        texp = jnp.sum(jnp.where(offs <= t, 1.0, 0.0), axis=0, keepdims=True) - 1.0
        valid = jnp.where(t[0:1] < total, 1.0, 0.0)
        lane_t = _lane_iota(texp.shape)
        first = jnp.where((lane_t == 0) | (texp != pltpu.roll(texp, 1, 1)), 1.0, 0.0)
        meta = jnp.concatenate([texp, valid, first, jnp.zeros((5, MAX_TILES_LANES), F32)], axis=0)
        meta_ref[...] = meta.astype(jnp.int32)

    @pl.when(phase == 1)
    def _():
        r = lax.broadcasted_iota(jnp.int32, (tp, tp), 0)
        c = lax.broadcasted_iota(jnp.int32, (tp, tp), 1)
        upper = jnp.where(r < c, 1.0, 0.0).astype(BF16)
        rank = _dot(sel.astype(BF16), upper)
        slot = offs_ref[:, 0:1] + carry_ref[:, 0:1] + rank
        pos_a = jnp.sum(jnp.where(sub == i1, slot, 0.0), axis=0, keepdims=True)
        pos_b = jnp.sum(jnp.where(sub == i2, slot, 0.0), axis=0, keepdims=True)
        pos = jnp.concatenate([pos_a, pos_b, jnp.zeros((6, tp), F32)], axis=0)
        pos_ref[...] = pos.astype(jnp.int32)
        carry_ref[...] += jnp.broadcast_to(tile_cnt, carry_ref.shape)


def _positions(route, *, n_exp, tp):
    n = route.shape[1]
    return pl.pallas_call(
        functools.partial(_pos_kernel, n_exp=n_exp),
        grid=(2, n // tp),
        in_specs=[pl.BlockSpec((8, tp), lambda p, i: (0, i))],
        out_specs=[pl.BlockSpec((8, tp), lambda p, i: (0, i * p)),
                   pl.BlockSpec((8, MAX_TILES_LANES), lambda p, i: (0, 0))],
        out_shape=[jax.ShapeDtypeStruct((8, n), jnp.int32),
                   jax.ShapeDtypeStruct((8, MAX_TILES_LANES), jnp.int32)],
        scratch_shapes=[pltpu.VMEM((n_exp, LANES), F32)] * 3,
        compiler_params=pltpu.CompilerParams(dimension_semantics=("arbitrary", "arbitrary")),
        name="moe_positions",
    )(route)


SC_CORES = 2
SC_SUBCORES = 16
SC_LANES = 16
SC_GATHER_ROWS = 64


def _sc_mesh():
    return plsc.VectorSubcoreMesh(core_axis_name="c", subcore_axis_name="s",
                                  num_cores=SC_CORES, num_subcores=SC_SUBCORES)


def _sc_invert(pos_a, pos_b, n_slots):
    n = pos_a.shape[0]

    def body(pa_hbm, pb_hbm, src_hbm, pa_v, pb_v, src_v):
        wid = lax.axis_index("s") * SC_CORES + lax.axis_index("c")

        @pl.when(wid == 0)
        def _():
            pltpu.sync_copy(pa_hbm, pa_v)
            pltpu.sync_copy(pb_hbm, pb_v)
            lane = lax.iota(jnp.int32, SC_LANES)

            @pl.loop(0, n_slots // SC_LANES)
            def _(j):
                off = pl.multiple_of(j * SC_LANES, SC_LANES)
                src_v[pl.ds(off, SC_LANES)] = lax.rem(lane + off, n)

            @pl.loop(0, n // SC_LANES)
            def _(j):
                off = pl.multiple_of(j * SC_LANES, SC_LANES)
                tok = lane + off
                plsc.store_scatter(src_v, [pa_v[pl.ds(off, SC_LANES)]], tok)
                plsc.store_scatter(src_v, [pb_v[pl.ds(off, SC_LANES)]], tok)

            pltpu.sync_copy(src_v, src_hbm)

    return pl.kernel(
        body, out_type=jax.ShapeDtypeStruct((n_slots,), jnp.int32), mesh=_sc_mesh(),
        scratch_types=[pltpu.VMEM((n,), jnp.int32), pltpu.VMEM((n,), jnp.int32),
                       pltpu.VMEM((n_slots,), jnp.int32)],
        compiler_params=pltpu.CompilerParams(needs_layout_passes=False),
        name="moe_invert",
    )(pos_a, pos_b)


def _sc_gather(table, idx):
    m = idx.shape[0]
    d = table.shape[1]
    n_workers = SC_CORES * SC_SUBCORES
    per_w = m // n_workers
    rows = SC_GATHER_ROWS

    n_pairs = per_w // (2 * rows)
    assert per_w == n_pairs * 2 * rows

    def body(table_hbm, idx_hbm, out_hbm, idx_v, buf_v, sems):
        wid = lax.axis_index("s") * SC_CORES + lax.axis_index("c")
        base = pl.multiple_of(wid * per_w, rows)
        pltpu.sync_copy(idx_hbm.at[pl.ds(base, per_w)], idx_v)

        def gather(chunk, slot):
            off = pl.multiple_of(chunk * rows, rows)
            return pltpu.make_async_copy(table_hbm.at[idx_v.at[pl.ds(off, rows)]], buf_v.at[slot],
                                         sems.at[slot])

        def write(chunk, slot):
            off = pl.multiple_of(chunk * rows, rows)
            pltpu.sync_copy(buf_v.at[slot], out_hbm.at[pl.ds(base + off, rows)])

        gather(0, 0).start()

        @pl.loop(0, n_pairs)
        def _(p):
            c0 = 2 * p
            gather(c0, 0).wait()
            gather(c0 + 1, 1).start()
            write(c0, 0)
            gather(c0 + 1, 1).wait()

            @pl.when(p + 1 < n_pairs)
            def _():
                gather(c0 + 2, 0).start()

            write(c0 + 1, 1)

    return pl.kernel(
        body, out_type=jax.ShapeDtypeStruct((m, d), table.dtype), mesh=_sc_mesh(),
        scratch_types=[pltpu.VMEM((per_w,), jnp.int32), pltpu.VMEM((2, rows, d), table.dtype),
                       pltpu.SemaphoreType.DMA((2,))],
        name="moe_gather",
    )(table, idx)


def _expert_kernel(meta_ref, xs_ref, wg_ref, wu_ref, wd_ref, ys_ref, wg_s, wu_s, wd_s):
    t = pl.program_id(0)
    valid = meta_ref[1, t] == 1

    @pl.when(valid & (meta_ref[2, t] == 1))
    def _():
        wg_s[...] = wg_ref[0, 0].astype(BF16)
        wu_s[...] = wu_ref[0, 0].astype(BF16)
        wd_s[...] = wd_ref[0, 0].astype(BF16)

    @pl.when(valid)
    def _():
        xb = _unpack_bf16_pairs(xs_ref[...]).astype(BF16)
        h = jax.nn.silu(_dot(xb, wg_s[...])) * _dot(xb, wu_s[...])
        ys_ref[...] = _pack_bf16_pairs(_dot(h.astype(BF16), wd_s[...]))

    @pl.when(jnp.logical_not(valid))
    def _():
        ys_ref[...] = jnp.zeros_like(ys_ref)


def _experts(meta, xs, wg, wu, wd, layer):
    n_slots, dp = xs.shape
    d, f = wg.shape[2], wg.shape[3]
    grid_spec = pltpu.PrefetchScalarGridSpec(
        num_scalar_prefetch=1,
        grid=(n_slots // MOE_TILE,),
        in_specs=[pl.BlockSpec((MOE_TILE, dp), lambda t, meta: (t, 0)),
                  pl.BlockSpec((1, 1, d, f), lambda t, meta: (layer, meta[0, t], 0, 0)),
                  pl.BlockSpec((1, 1, d, f), lambda t, meta: (layer, meta[0, t], 0, 0)),
                  pl.BlockSpec((1, 1, f, d), lambda t, meta: (layer, meta[0, t], 0, 0))],
        out_specs=pl.BlockSpec((MOE_TILE, dp), lambda t, meta: (t, 0)),
        scratch_shapes=[pltpu.VMEM((d, f), BF16), pltpu.VMEM((d, f), BF16), pltpu.VMEM((f, d), BF16)],
    )
    return pl.pallas_call(
        _expert_kernel, grid_spec=grid_spec,
        out_shape=jax.ShapeDtypeStruct((n_slots, dp), jnp.int32),
        compiler_params=pltpu.CompilerParams(dimension_semantics=("arbitrary",),
                                             vmem_limit_bytes=VMEM_LIMIT),
        name="moe_experts",
    )(meta, xs, wg, wu, wd)


def _combine_kernel(x_ref, ya_ref, yb_ref, route_ref, lng_ref, lnb_ref, o_ref, *, alpha):
    w = route_ref[...].T
    moe = w[:, 2:3] * _unpack_bf16_pairs(ya_ref[...]) + w[:, 3:4] * _unpack_bf16_pairs(yb_ref[...])
    o_ref[...] = _layer_norm(alpha * x_ref[...] + moe, lng_ref[0], lnb_ref[0])


def _combine(x1, yab, route, lng, lnb, *, layer, tn, alpha):
    n, d = x1.shape
    row = pl.BlockSpec((tn, d), lambda i: (i, 0))
    tiles = n // tn
    return pl.pallas_call(
        functools.partial(_combine_kernel, alpha=alpha),
        grid=(tiles,),
        in_specs=[row,
                  pl.BlockSpec((tn, d // 2), lambda i: (i, 0)),
                  pl.BlockSpec((tn, d // 2), lambda i: (i + tiles, 0)),
                  pl.BlockSpec((8, tn), lambda i: (0, i)),
                  _layer_block(lng, layer), _layer_block(lnb, layer)],
        out_specs=row,
        out_shape=jax.ShapeDtypeStruct((n, d), F32),
        compiler_params=pltpu.CompilerParams(dimension_semantics=("arbitrary",)),
        name="moe_combine",
    )(x1, yab, yab, route, lng, lnb)


def _moe(x1, xpk, route, wg, wu, wd, lng, lnb, *, layer, alpha):
    n, d = x1.shape
    n_exp = wg.shape[1]
    n_slots = (2 * n // MOE_TILE + n_exp) * MOE_TILE
    assert n_slots // MOE_TILE <= MAX_TILES_LANES
    pos, meta = _positions(route, n_exp=n_exp, tp=min(512, n))
    src = _sc_invert(pos[0], pos[1], n_slots)
    xs = _sc_gather(xpk, src)
    ys = _experts(meta, xs, wg, wu, wd, layer)
    yab = _sc_gather(ys, pos[:2].reshape(2 * n))
    return _combine(x1, yab, route, lng, lnb, layer=layer, tn=min(512, n), alpha=alpha)


def _dup(v):
    return jnp.concatenate([v, v], axis=-1)


def kernel(x, mem, w_in, w_gate, b_gate, ssm_lam_re, ssm_lam_im, ssm_log_step, ssm_b_re, ssm_b_im, ssm_c_re, ssm_c_im, ssm_d, w_glu, b_glu, gmlp_ln_g, gmlp_ln_b, w_spatial, b_spatial, w_kv, w_br, w_out, ln1_g, ln1_b, w_router_g, b_router_g, w_router_e, b_router_e, w_exp_gate, w_exp_up, w_exp_down, ln2_g, ln2_b):
    bsz, seq, d = x.shape
    depth = w_in.shape[0]
    n = bsz * seq
    n_grp = ssm_lam_re.shape[2]
    ssm_w = n_grp * GROUP_CH
    gmlp_w = gmlp_ln_g.shape[1]
    n_heads = 4
    n_moe_grp, exp_per_grp = b_router_e.shape[1], b_router_e.shape[2]
    alpha = (2.0 * depth) ** 0.25
    tn = 512

    ls = jnp.broadcast_to(ssm_log_step[..., None], ssm_lam_re.shape)
    six = [ssm_lam_re[:, 0], ssm_lam_im[:, 0], ssm_lam_re[:, 1], ssm_lam_im[:, 1], ls[:, 0], ls[:, 1]]
    prow = jnp.stack([_dup(v) for v in six] + [jnp.zeros_like(_dup(six[0]))] * 2, axis=2)
    pcol = jnp.stack(six + [jnp.zeros_like(six[0])] * 2, axis=-1)
    bt2 = jnp.concatenate([jnp.swapaxes(ssm_b_re, 2, 3), jnp.swapaxes(ssm_b_im, 2, 3)], axis=-1)
    c2 = jnp.concatenate([ssm_c_re, ssm_c_im], axis=-1)
    ct2 = jnp.concatenate([jnp.swapaxes(ssm_c_re, 2, 3), jnp.swapaxes(ssm_c_im, 2, 3)], axis=-1)
    wtab, pst, gmat, dec = _ssm_tables(prow, pcol, bt2, c2, ct2)
    dvec = jnp.repeat(ssm_d.reshape(depth, n_grp, 1, GROUP_CH), SSM_T, axis=-1)

    assert w_in.shape[2] == 4 * ssm_w and gmlp_w == ssm_w
    w_in_b = w_in.astype(BF16)
    wssm_t = jnp.swapaxes(w_in[:, :, :ssm_w], 1, 2).astype(BF16)
    w_gate_b = w_gate.astype(BF16)
    b_gate3 = b_gate.reshape(depth, 1, -1)
    w_br_b = w_br.astype(BF16)
    w_out_b = w_out.astype(BF16)
    w_kv_b = w_kv.astype(BF16)
    w_glu_t = jnp.swapaxes(w_glu, 1, 2).astype(BF16)
    b_glu_c = b_glu.reshape(depth, -1, 1)
    w_sp_b = w_spatial.astype(BF16)
    b_sp_t = jnp.swapaxes(b_spatial, 1, 2)
    row3 = lambda v: v.reshape(depth, 1, -1)
    wrg_t = jnp.repeat(jnp.swapaxes(w_router_g, 1, 2), exp_per_grp, axis=1)
    wre_t = jnp.transpose(w_router_e, (0, 1, 3, 2)).reshape(depth, n_moe_grp * exp_per_grp, d)
    wr = jnp.concatenate([wrg_t, wre_t], axis=1)
    wr_hi = wr.astype(BF16)
    wr_hl = jnp.concatenate([wr_hi, (wr - wr_hi.astype(F32)).astype(BF16)], axis=1)
    br = jnp.concatenate([jnp.repeat(b_router_g, exp_per_grp, axis=1),
                          b_router_e.reshape(depth, -1)], axis=1).reshape(depth, -1, 1)

    n_streams = 1
    ns = n // n_streams
    xin = [x.reshape(n, d)] * n_streams
    row_off = [s * ns for s in range(n_streams)]
    for l in range(depth):
        kv = _kv_proj(mem, w_kv_b, l)
        zs, m12s = [], []
        for s in range(n_streams):
            z, m12 = _front(
                xin[s], wssm_t, w_in_b, w_gate_b, b_gate3, row3(gmlp_ln_g), row3(gmlp_ln_b), w_sp_b,
                b_sp_t, kv, w_br_b, layer=l, tn=tn, n_heads=n_heads, tokens_per_batch=seq,
                n=ns, row_off=row_off[s], batch_off=s * (bsz // n_streams))
            zs.append(z)
            m12s.append(m12)
        ys = _ssm_apply_pairs(zs, wtab, pst, gmat, dec, dvec, l, seq // LANES)
        nxt = []
        for s in range(n_streams):
            x1, xpk, route = _back(
                xin[s], m12s[s], ys[s], w_glu_t, b_glu_c, w_br_b, w_gate_b, b_gate3, w_out_b,
                row3(ln1_g), row3(ln1_b), wr_hl, br, layer=l, tn=tn, alpha=alpha, row_off=row_off[s])
            nxt.append(_moe(x1, xpk, route, w_exp_gate, w_exp_up, w_exp_down, row3(ln2_g), row3(ln2_b),
                            layer=l, alpha=alpha))
        xin = nxt
        row_off = [0] * n_streams
    return jnp.concatenate(xin, axis=0).reshape(bsz, seq, d)
```

```python
import functools
import math

import jax
import jax.numpy as jnp
from jax import lax
from jax.experimental import pallas as pl
from jax.experimental.pallas import tpu as pltpu
from jax.experimental.pallas import tpu_sc as plsc

F32 = jnp.float32
BF16 = jnp.bfloat16

LANES = 128
HALF = LANES // 2
SSM_T = HALF
GROUP_CH = 16
N_STATE = 64
LN_EPS = 1e-5
VMEM_LIMIT = 56 * 1024 * 1024

NT_DIMS = (((1,), (1,)), ((), ()))
TN_DIMS = (((0,), (0,)), ((), ()))


def _dot(a, b):
    return jnp.dot(a, b, preferred_element_type=F32)


def _dot_hi(a, b):
    return jnp.dot(a, b, preferred_element_type=F32, precision=lax.Precision.HIGHEST)


def _layer_norm(x, g, b):
    mu = jnp.mean(x, axis=-1, keepdims=True)
    xc = x - mu
    var = jnp.mean(xc * xc, axis=-1, keepdims=True)
    return xc * lax.rsqrt(var + LN_EPS) * g + b


def _lane_iota(shape):
    return lax.broadcasted_iota(jnp.int32, shape, len(shape) - 1)


def _swap_halves(x):
    return pltpu.roll(x, HALF, 1)


def _ssm_tables_kernel(prow_ref, pcol_ref, bt_ref, c_ref, ct_ref,
                       wtab_ref, pst_ref, g_ref, dec_ref):
    T = SSM_T
    prow = prow_ref[0, 0]
    pcol = pcol_ref[0, 0]
    bt2 = bt_ref[0, 0]
    c2 = c_ref[0, 0]
    ct = ct_ref[0, 0]
    lane1 = _lane_iota((1, LANES))
    lo1 = lane1 < HALF
    sgn1 = jnp.where(lo1, -1.0, 1.0).astype(F32)

    lane_bc = _lane_iota((GROUP_CH, LANES))
    lo_bc = lane_bc < HALF
    c2s = jnp.where(lo_bc, -1.0, 1.0) * _swap_halves(c2)

    bbar = []
    a_row = []
    th_row = []
    for d in range(2):
        lre = prow[2 * d:2 * d + 1]
        lim = prow[2 * d + 1:2 * d + 2]
        dt = jnp.exp(prow[4 + d:5 + d])
        a = lre * dt
        th = lim * dt
        er = jnp.exp(a)
        lb_re = er * jnp.cos(th)
        lb_im = er * jnp.sin(th)
        num_re = lb_re - 1.0
        den = lre * lre + lim * lim
        coef_re = (num_re * lre + lb_im * lim) / den
        coef_im = (lb_im * lre - num_re * lim) / den
        bbar.append(coef_re * bt2 + (sgn1 * coef_im) * _swap_halves(bt2))
        a_row.append(a)
        th_row.append(th)

    s_col = lax.broadcasted_iota(jnp.int32, (T, 1), 0).astype(F32)
    for d in range(2):
        k = (T - 1.0) - s_col if d == 0 else s_col
        e = jnp.exp(k * a_row[d])
        ang = k * th_row[d]
        pr = e * jnp.cos(ang)
        pi = e * jnp.sin(ang)
        bsw = sgn1 * _swap_halves(bbar[d])
        for hp in range(GROUP_CH):
            blk = pr * bbar[d][hp:hp + 1] + pi * bsw[hp:hp + 1]
            pst_ref[0, 0, hp * T:(hp + 1) * T, d * LANES:(d + 1) * LANES] = blk.astype(pst_ref.dtype)

    for d in range(2):
        e = jnp.exp(float(T) * a_row[d])
        ang = float(T) * th_row[d]
        dec_ref[0, 0, :, d * LANES:(d + 1) * LANES] = e * jnp.where(lo1, jnp.cos(ang), jnp.sin(ang))

    lane_p = _lane_iota((N_STATE, LANES))
    lo_p = lane_p < HALF
    t_lane = jnp.where(lo_p, lane_p, lane_p - HALF).astype(F32)
    pw = []
    for d in range(2):
        dtc = jnp.exp(pcol[:, 4 + d:5 + d])
        a_c = pcol[:, 2 * d:2 * d + 1] * dtc
        th_c = pcol[:, 2 * d + 1:2 * d + 2] * dtc
        pw.append((a_c, th_c))
    for d in range(2):
        a_c, th_c = pw[d]
        k = t_lane + 1.0 if d == 0 else float(T) - t_lane
        e = jnp.exp(k * a_c)
        pr = e * jnp.cos(k * th_c)
        pi = e * jnp.sin(k * th_c)
        for j in range(GROUP_CH // 2):
            cre = jnp.where(lo_p, ct[:, 2 * j:2 * j + 1], ct[:, 2 * j + 1:2 * j + 2])
            cim = jnp.where(lo_p, ct[:, GROUP_CH + 2 * j:GROUP_CH + 2 * j + 1],
                            ct[:, GROUP_CH + 2 * j + 1:GROUP_CH + 2 * j + 2])
            g_re = cre * pr - cim * pi
            g_im = -(cre * pi + cim * pr)
            g_ref[0, 0, d * LANES:d * LANES + N_STATE, j * LANES:(j + 1) * LANES] = g_re.astype(g_ref.dtype)
            g_ref[0, 0, d * LANES + N_STATE:(d + 1) * LANES, j * LANES:(j + 1) * LANES] = g_im.astype(g_ref.dtype)

    res = []
    kb0 = None
    for d in range(2):
        a_c, th_c = pw[d]
        if d == 0:
            k = jnp.maximum(lane_p - HALF, 0).astype(F32)
        else:
            k = jnp.maximum(HALF - lane_p, 0).astype(F32)
        e = jnp.exp(k * a_c)
        rhs = jnp.concatenate([e * jnp.cos(k * th_c), -(e * jnp.sin(k * th_c))], axis=0)
        bsw = _swap_halves(bbar[d])
        b_re = jnp.where(lo_bc, bbar[d], bsw)
        b_im = jnp.where(lo_bc, bsw, bbar[d])
        bc = (b_re[:, None, :] * c2[None, :, :] + b_im[:, None, :] * c2s[None, :, :])
        bc = bc.reshape(GROUP_CH * GROUP_CH, LANES)
        res.append(_dot_hi(bc, rhs))
        if d == 1:
            lane_bcf = _lane_iota(bc.shape)
            kb0 = jnp.sum(jnp.where(lane_bcf < HALF, bc, 0.0), axis=1, keepdims=True)
    lane_w = _lane_iota(res[0].shape)
    wtab = jnp.where(lane_w >= HALF, res[0], res[1])
    wtab_ref[0, 0] = wtab + jnp.where(lane_w == HALF, kb0, 0.0)


def _ssm_tables(prow, pcol, bt2, c2, ct2):
    L, G = prow.shape[0], prow.shape[1]
    T = SSM_T
    blk = lambda shape: pl.BlockSpec((1, 1) + shape, lambda l, g: (l, g, 0, 0))
    return pl.pallas_call(
        _ssm_tables_kernel,
        grid=(L, G),
        in_specs=[blk((8, LANES)), blk((N_STATE, 8)), blk((GROUP_CH, LANES)),
                  blk((GROUP_CH, LANES)), blk((N_STATE, 2 * GROUP_CH))],
        out_specs=[blk((GROUP_CH * GROUP_CH, LANES)), blk((GROUP_CH * T, 2 * LANES)),
                   blk((2 * LANES, GROUP_CH * T)), blk((1, 2 * LANES))],
        out_shape=[jax.ShapeDtypeStruct((L, G, GROUP_CH * GROUP_CH, LANES), F32),
                   jax.ShapeDtypeStruct((L, G, GROUP_CH * T, 2 * LANES), BF16),
                   jax.ShapeDtypeStruct((L, G, 2 * LANES, GROUP_CH * T), BF16),
                   jax.ShapeDtypeStruct((L, G, 1, 2 * LANES), F32)],
        name="ssm_tables",
    )(prow, pcol, bt2, c2, ct2)


def _cmul_packed(x, d_re, d_sw):
    return x * d_re + _swap_halves(x) * d_sw


SSM_GROUPS_PER_STEP = 2


def _toeplitz_build(m_ref, wtab_ref, k):
    T = SSM_T
    lo_t = _lane_iota((T, LANES)) < HALF
    for hp in range(GROUP_CH):
        for j in range(GROUP_CH // 2):
            r = hp * GROUP_CH + 2 * j
            wa = jnp.broadcast_to(wtab_ref[0, k, r:r + 1, :], (T, LANES))
            wb = jnp.broadcast_to(wtab_ref[0, k, r + 1:r + 2, :], (T, LANES))
            ra = pltpu.roll(wa, HALF, 1, stride=1, stride_axis=0)
            rb = pltpu.roll(wb, 0, 1, stride=1, stride_axis=0)
            m_ref[hp * T:(hp + 1) * T, j * LANES:(j + 1) * LANES] = (
                jnp.where(lo_t, ra, rb).astype(m_ref.dtype))


def _ssm_group(z_refs, o_refs, k, m_ref, pst, gmat, dec, dvec, rows_per_batch):
    rows_s = z_refs[0].shape[1] // GROUP_CH
    n_rows = len(z_refs) * rows_s
    lane = _lane_iota((n_rows, LANES))
    lo = lane < HALF
    a0, a1 = [], []
    for j in range(GROUP_CH // 2):
        pe = jnp.concatenate([z[k, pl.ds(2 * j, rows_s, stride=GROUP_CH), :] for z in z_refs], axis=0)
        po = jnp.concatenate([z[k, pl.ds(2 * j + 1, rows_s, stride=GROUP_CH), :] for z in z_refs], axis=0)
        a0.append(jnp.where(lo, pe, _swap_halves(po)))
        a1.append(jnp.where(lo, _swap_halves(pe), po))
    a_f32 = jnp.concatenate([jnp.concatenate(a0, axis=1), jnp.concatenate(a1, axis=1)], axis=0)
    a_bf = a_f32.astype(BF16)

    out = _dot(a_bf, m_ref[...]) + dvec * a_f32
    s = _dot(a_bf, pst)
    sf0, sb0 = s[:n_rows, :LANES], s[:n_rows, LANES:]
    sf1, sb1 = s[n_rows:, :LANES], s[n_rows:, LANES:]

    lo1 = _lane_iota((1, LANES)) < HALF
    sgn = jnp.where(lo1, -1.0, 1.0).astype(F32)

    def split(p):
        sw = _swap_halves(p)
        return jnp.where(lo1, p, sw), sgn * jnp.where(lo1, sw, p)

    def square(p):
        d_re, d_sw = split(p)
        return _cmul_packed(p, d_re, d_sw)

    dec_f, dec_b = dec[:, :LANES], dec[:, LANES:]
    df_re, df_sw = split(dec_f)
    db_re, db_sw = split(dec_b)

    row = lax.broadcasted_iota(jnp.int32, (n_rows, LANES), 0)
    rib = row % rows_per_batch

    ef = _cmul_packed(sf0, df_re, df_sw) + sf1
    eb = sb0 + _cmul_packed(sb1, db_re, db_sw)
    pf, pb = square(dec_f), square(dec_b)
    step = 1
    while step < rows_per_batch:
        pf_re, pf_sw = split(pf)
        pb_re, pb_sw = split(pb)
        shf = jnp.where(rib >= step, pltpu.roll(ef, step, 0), 0.0)
        ef = ef + _cmul_packed(shf, pf_re, pf_sw)
        shb = jnp.where(rib < rows_per_batch - step, pltpu.roll(eb, n_rows - step, 0), 0.0)
        eb = eb + _cmul_packed(shb, pb_re, pb_sw)
        pf, pb = square(pf), square(pb)
        step *= 2
    hf0 = jnp.where(rib >= 1, pltpu.roll(ef, 1, 0), 0.0)
    hf1 = _cmul_packed(hf0, df_re, df_sw) + sf0
    hb1 = jnp.where(rib < rows_per_batch - 1, pltpu.roll(eb, n_rows - 1, 0), 0.0)
    hb0 = sb1 + _cmul_packed(hb1, db_re, db_sw)
    h_in = jnp.concatenate([jnp.concatenate([hf0, hb0], axis=1),
                            jnp.concatenate([hf1, hb1], axis=1)], axis=0)
    out = out + _dot(h_in.astype(BF16), gmat)

    o0, o1 = out[:n_rows], out[n_rows:]
    for j in range(GROUP_CH // 2):
        t0 = o0[:, j * LANES:(j + 1) * LANES]
        t1 = o1[:, j * LANES:(j + 1) * LANES]
        even = jnp.where(lo, t0, _swap_halves(t1))
        odd = jnp.where(lo, _swap_halves(t0), t1)
        for si, o_ref in enumerate(o_refs):
            o_ref[k, pl.ds(2 * j, rows_s, stride=GROUP_CH), :] = even[si * rows_s:(si + 1) * rows_s]
            o_ref[k, pl.ds(2 * j + 1, rows_s, stride=GROUP_CH), :] = odd[si * rows_s:(si + 1) * rows_s]


def _ssm_pair_kernel(*refs, n_streams, rows_per_batch):
    z_refs = refs[:n_streams]
    wtab_ref, wnext_ref, pst_ref, g_ref, dec_ref, dvec_ref = refs[n_streams:n_streams + 6]
    o_refs = refs[n_streams + 6:2 * n_streams + 6]
    m_a, m_b = refs[2 * n_streams + 6:2 * n_streams + 8]

    @pl.when(pl.program_id(0) == 0)
    def _():
        _toeplitz_build(m_a, wtab_ref, 0)

    def group(k, m_ref):
        _ssm_group(z_refs, o_refs, k, m_ref, pst_ref[0, k], g_ref[0, k], dec_ref[0, k], dvec_ref[0, k],
                   rows_per_batch)

    _toeplitz_build(m_b, wtab_ref, 1)
    group(0, m_a)
    _toeplitz_build(m_a, wnext_ref, 0)
    group(1, m_b)


def _ssm_apply_pairs(zs, wtab, pst, gmat, dec, dvec, layer, rows_per_batch):
    G, R, _ = zs[0].shape
    T = SSM_T
    gp = SSM_GROUPS_PER_STEP
    n_streams = len(zs)
    tab = lambda shape: pl.BlockSpec((1, gp) + shape, lambda g: (layer, g, 0, 0))
    nxt = pl.BlockSpec((1, 1, GROUP_CH * GROUP_CH, LANES),
                       lambda g: (layer, jnp.minimum(gp * g + gp, G - 1), 0, 0))
    seq_block = pl.BlockSpec((gp, R, LANES), lambda g: (g, 0, 0))
    return pl.pallas_call(
        functools.partial(_ssm_pair_kernel, n_streams=n_streams, rows_per_batch=rows_per_batch),
        grid=(G // gp,),
        in_specs=[seq_block] * n_streams + [
            tab((GROUP_CH * GROUP_CH, LANES)), nxt, tab((GROUP_CH * T, 2 * LANES)),
            tab((2 * LANES, GROUP_CH * T)), tab((1, 2 * LANES)), tab((1, GROUP_CH * T))],
        out_specs=[seq_block] * n_streams,
        out_shape=[jax.ShapeDtypeStruct(zs[0].shape, F32)] * n_streams,
        scratch_shapes=[pltpu.VMEM((GROUP_CH * T, GROUP_CH * T), BF16)] * 2,
        compiler_params=pltpu.CompilerParams(dimension_semantics=("arbitrary",),
                                             vmem_limit_bytes=VMEM_LIMIT),
        name="ssm_seq",
    )(*zs, wtab, wtab, pst, gmat, dec, dvec)


def _kv_kernel(mem_ref, w_ref, o_ref):
    o_ref[0] = _dot(mem_ref[0].astype(BF16), w_ref[0]).astype(o_ref.dtype)


def _kv_proj(mem, w_kv, layer):
    B, M, D = mem.shape
    width = w_kv.shape[2]
    return pl.pallas_call(
        _kv_kernel,
        grid=(B,),
        in_specs=[pl.BlockSpec((1, M, D), lambda b: (b, 0, 0)),
                  pl.BlockSpec((1, D, width), lambda b: (layer, 0, 0))],
        out_specs=pl.BlockSpec((1, M, width), lambda b: (b, 0, 0)),
        out_shape=jax.ShapeDtypeStruct((B, M, width), BF16),
        name="kv_proj",
    )(mem, w_kv)


def _front_kernel(x_ref, wssm_t_ref, wu_ref, wv_ref, wq_ref, wg1_ref, wg2_ref, bg1_ref, bg2_ref,
                  lng_ref, lnb_ref, ws_ref, bs_t_ref, kv_ref, wbr1_ref, wbr2_ref, z_ref, m_ref, *, n_heads):
    tn, d_model = x_ref.shape
    xb = x_ref[...].astype(BF16)

    u_t = lax.dot_general(wssm_t_ref[0], xb, NT_DIMS, preferred_element_type=F32)
    n_grp = u_t.shape[0] // GROUP_CH
    for a in range(tn // LANES):
        z_ref[:, a * GROUP_CH:(a + 1) * GROUP_CH, :] = (
            u_t[:, a * LANES:(a + 1) * LANES].reshape(n_grp, GROUP_CH, LANES))

    u = jax.nn.gelu(_dot(xb, wu_ref[0]))
    v = jax.nn.gelu(_dot(xb, wv_ref[0]))
    half = v.shape[1]
    vb = _layer_norm(v, lng_ref[0], lnb_ref[0]).astype(BF16)
    lane = _lane_iota((LANES, LANES))
    lo = lane < HALF
    bs_t = bs_t_ref[0]
    s_rows = []
    for a in range(tn // LANES):
        va = vb[a * LANES:(a + 1) * LANES]
        tiles = []
        for j in range(half // LANES):
            rhs = va[:, j * LANES:(j + 1) * LANES]
            ev = _dot(ws_ref[0, 2 * j], rhs)
            od = _dot(ws_ref[0, 2 * j + 1], rhs)
            bias = jnp.where(lo, bs_t[:, 2 * j:2 * j + 1], bs_t[:, 2 * j + 1:2 * j + 2])
            tiles.append(jnp.where(lo, ev, od) + bias)
        s_rows.append(jnp.concatenate(tiles, axis=1))
    y_gmlp = u * jnp.concatenate(s_rows, axis=0)

    q = _dot(xb, wq_ref[0])
    hd = q.shape[1] // n_heads
    kv = kv_ref[0]
    outs = []
    for h in range(n_heads):
        qh = q[:, h * hd:(h + 1) * hd].astype(BF16)
        kh = kv[:, h * hd:(h + 1) * hd]
        vh = kv[:, q.shape[1] + h * hd:q.shape[1] + (h + 1) * hd]
        sc = lax.dot_general(qh, kh, NT_DIMS, preferred_element_type=F32) * (hd ** -0.5)
        sc = sc - jnp.max(sc, axis=-1, keepdims=True)
        p = jnp.exp(sc)
        p = p / jnp.sum(p, axis=-1, keepdims=True)
        outs.append(_dot(p.astype(BF16), vh))
    y_mem = jnp.concatenate(outs, axis=1)

    g1 = jax.nn.sigmoid(_dot(xb, wg1_ref[0]) + bg1_ref[0])
    g2 = jax.nn.sigmoid(_dot(xb, wg2_ref[0]) + bg2_ref[0])
    m_ref[...] = (g1 * _dot(y_gmlp.astype(BF16), wbr1_ref[0, 0])
                  + g2 * _dot(y_mem.astype(BF16), wbr2_ref[0, 0]))


def _layer_block(arr, layer, col_block=None, col=0, branch=None):
    if branch is not None:
        return pl.BlockSpec((1, 1) + arr.shape[2:], lambda *_: (layer, branch, 0, 0))
    shape = arr.shape[1:]
    if col_block is not None:
        shape = shape[:-1] + (col_block,)
    nd = len(shape)
    return pl.BlockSpec((1,) + shape, lambda *_: (layer,) + (0,) * (nd - 1) + (col,))


def _front(x2, wssm_t, w_in, w_gate, b_gate, lng, lnb, ws, bs_t, kv, w_br, *, layer, tn, n_heads,
           tokens_per_batch, n, row_off, batch_off):
    d = x2.shape[1]
    n_grp = wssm_t.shape[1] // GROUP_CH
    width = wssm_t.shape[1]
    rows = tn // LANES * GROUP_CH
    tiles_per_batch = tokens_per_batch // tn
    tile_off = row_off // tn
    lb = functools.partial(_layer_block, layer=layer)
    return pl.pallas_call(
        functools.partial(_front_kernel, n_heads=n_heads),
        grid=(n // tn,),
        in_specs=[pl.BlockSpec((tn, d), lambda i: (i + tile_off, 0)),
                  lb(wssm_t),
                  lb(w_in, col_block=width, col=1), lb(w_in, col_block=width, col=2),
                  lb(w_in, col_block=width, col=3),
                  lb(w_gate, col_block=d, col=1), lb(w_gate, col_block=d, col=2),
                  lb(b_gate, col_block=d, col=1), lb(b_gate, col_block=d, col=2),
                  lb(lng), lb(lnb), lb(ws), lb(bs_t),
                  pl.BlockSpec((1,) + kv.shape[1:], lambda i: (i // tiles_per_batch + batch_off, 0, 0)),
                  lb(w_br, branch=1), lb(w_br, branch=2)],
        out_specs=[pl.BlockSpec((n_grp, rows, LANES), lambda i: (0, i, 0)),
                   pl.BlockSpec((tn, d), lambda i: (i, 0))],
        out_shape=[jax.ShapeDtypeStruct((n_grp, n // LANES * GROUP_CH, LANES), F32),
                   jax.ShapeDtypeStruct((n, d), F32)],
        compiler_params=pltpu.CompilerParams(dimension_semantics=("arbitrary",),
                                             vmem_limit_bytes=VMEM_LIMIT),
        name="mixer_front",
    )(x2, wssm_t, w_in, w_in, w_in, w_gate, w_gate, b_gate, b_gate, lng, lnb, ws, bs_t, kv, w_br, w_br)


def _route_t(lg, le):
    n_exp = float(le.shape[0])
    sub = lax.broadcasted_iota(jnp.int32, le.shape, 0).astype(F32)
    gmax = jnp.max(lg, axis=0, keepdims=True)
    denom = jnp.sum(jnp.exp(lg - gmax), axis=0, keepdims=True) * (1.0 / 8.0)
    pg_sel = 1.0 / denom
    first = jnp.min(jnp.where(lg == gmax, sub, n_exp), axis=0, keepdims=True)
    in_grp = jnp.floor(sub * 0.125) == jnp.floor(first * 0.125)
    neg = -jnp.inf
    l1 = jnp.where(in_grp, le, neg)
    m1 = jnp.max(l1, axis=0, keepdims=True)
    i1 = jnp.min(jnp.where(l1 == m1, sub, n_exp), axis=0, keepdims=True)
    l2 = jnp.where(sub == i1, neg, l1)
    m2 = jnp.max(l2, axis=0, keepdims=True)
    i2 = jnp.min(jnp.where(l2 == m2, sub, n_exp), axis=0, keepdims=True)
    e2 = jnp.exp(m2 - m1)
    p1 = 1.0 / (1.0 + e2)
    p2 = e2 / (1.0 + e2)
    return i1, i2, pg_sel * p1, pg_sel * p2


def _pack_bf16_pairs(x):
    c = x.shape[1] // 2
    rounded = x.astype(BF16).astype(F32)
    bits = lax.bitcast_convert_type(rounded, jnp.uint32)
    word = (bits[:, :c] >> 16) | bits[:, c:]
    return lax.bitcast_convert_type(word, jnp.int32)


def _unpack_bf16_pairs(w):
    bits = lax.bitcast_convert_type(w, jnp.uint32)
    lo = lax.bitcast_convert_type(bits << 16, F32)
    hi = lax.bitcast_convert_type(bits & jnp.uint32(0xFFFF0000), F32)
    return jnp.concatenate([lo, hi], axis=1)


def _back_kernel(x_ref, m12_ref, y_ref, wglu_t_ref, bglu_ref, wbr0_ref, wg0_ref, bg0_ref, wout_ref,
                 lng_ref, lnb_ref, wr_ref, br_ref, x1_ref, xpk_ref, route_ref, *, alpha):
    tn = x_ref.shape[0]
    x = x_ref[...]
    xb = x.astype(BF16)
    n_grp = y_ref.shape[0]
    cols = []
    for a in range(tn // LANES):
        y_t = y_ref[:, a * GROUP_CH:(a + 1) * GROUP_CH, :].reshape(n_grp * GROUP_CH, LANES)
        y_t = jax.nn.gelu(y_t)
        gate = jax.nn.sigmoid(_dot(wglu_t_ref[0], y_t.astype(BF16)) + bglu_ref[0])
        cols.append((y_t * gate).astype(BF16))
    ys_t = jnp.concatenate(cols, axis=1)
    br0 = lax.dot_general(ys_t, wbr0_ref[0, 0], TN_DIMS, preferred_element_type=F32)
    g0 = jax.nn.sigmoid(_dot(xb, wg0_ref[0]) + bg0_ref[0])
    merged = m12_ref[...] + g0 * br0
    h = _dot(merged.astype(BF16), wout_ref[0])
    x1 = _layer_norm(alpha * x + h, lng_ref[0], lnb_ref[0])
    x1_ref[...] = x1
    xpk_ref[...] = _pack_bf16_pairs(x1)
    x_hi = x1.astype(BF16)
    x_lo = (x1 - x_hi.astype(F32)).astype(BF16)
    w_hl = wr_ref[0]
    n_r = w_hl.shape[0] // 2
    part = lax.dot_general(w_hl, x_hi, NT_DIMS, preferred_element_type=F32)
    logits = (part[:n_r] + part[n_r:]
              + lax.dot_general(w_hl[:n_r], x_lo, NT_DIMS, preferred_element_type=F32) + br_ref[0])
    n_exp = n_r // 2
    i1, i2, w1, w2 = _route_t(logits[:n_exp], logits[n_exp:])
    route_ref[...] = jnp.concatenate([i1, i2, w1, w2, jnp.zeros((4, tn), F32)], axis=0)


def _back(x2, m12, y, wglu_t, bglu, w_br, w_gate, b_gate, wout, lng, lnb, wr, br, *, layer, tn, alpha,
          row_off):
    n, d = m12.shape
    n_grp = y.shape[0]
    rows = tn // LANES * GROUP_CH
    tile_off = row_off // tn
    lb = functools.partial(_layer_block, layer=layer)
    return pl.pallas_call(
        functools.partial(_back_kernel, alpha=alpha),
        grid=(n // tn,),
        in_specs=[pl.BlockSpec((tn, d), lambda i: (i + tile_off, 0)),
                  pl.BlockSpec((tn, d), lambda i: (i, 0)),
                  pl.BlockSpec((n_grp, rows, LANES), lambda i: (0, i, 0)),
                  lb(wglu_t), lb(bglu), lb(w_br, branch=0),
                  lb(w_gate, col_block=d, col=0), lb(b_gate, col_block=d, col=0), lb(wout),
                  lb(lng), lb(lnb), lb(wr), lb(br)],
        out_specs=[pl.BlockSpec((tn, d), lambda i: (i, 0)),
                   pl.BlockSpec((tn, d // 2), lambda i: (i, 0)),
                   pl.BlockSpec((8, tn), lambda i: (0, i))],
        out_shape=[jax.ShapeDtypeStruct((n, d), F32),
                   jax.ShapeDtypeStruct((n, d // 2), jnp.int32),
                   jax.ShapeDtypeStruct((8, n), F32)],
        compiler_params=pltpu.CompilerParams(dimension_semantics=("arbitrary",),
                                             vmem_limit_bytes=VMEM_LIMIT),
        name="mixer_back",
    )(x2, m12, y, wglu_t, bglu, w_br, w_gate, b_gate, wout, lng, lnb, wr, br)


MOE_TILE = 256
MAX_TILES_LANES = 256


def _pos_kernel(route_ref, pos_ref, meta_ref, cnt_ref, offs_ref, carry_ref, *, n_exp):
    phase = pl.program_id(0)
    i = pl.program_id(1)
    tp = route_ref.shape[1]
    sub = lax.broadcasted_iota(jnp.int32, (n_exp, tp), 0).astype(F32)
    i1 = route_ref[0:1, :]
    i2 = route_ref[1:2, :]
    sel = jnp.where((sub == i1) | (sub == i2), 1.0, 0.0)
    tile_cnt = jnp.sum(sel, axis=1, keepdims=True)

    @pl.when((phase == 0) & (i == 0))
    def _():
        cnt_ref[...] = jnp.zeros_like(cnt_ref)

    @pl.when(phase == 0)
    def _():
        cnt_ref[...] += jnp.broadcast_to(tile_cnt, cnt_ref.shape)

    @pl.when((phase == 1) & (i == 0))
    def _():
        cnt = cnt_ref[...]
        padded = jnp.ceil(cnt * (1.0 / MOE_TILE)) * float(MOE_TILE)
        sub_e = lax.broadcasted_iota(jnp.int32, cnt.shape, 0)
        lane_e = lax.broadcasted_iota(jnp.int32, cnt.shape, 1)
        row = jnp.sum(jnp.where(sub_e == lane_e, padded, 0.0), axis=0, keepdims=True)
        offs = jnp.sum(jnp.where(lane_e < sub_e, row, 0.0), axis=1, keepdims=True)
        offs_ref[...] = jnp.broadcast_to(offs, offs_ref.shape)
        carry_ref[...] = jnp.zeros_like(carry_ref)
        total = jnp.sum(jnp.where(lane_e < n_exp, row, 0.0), axis=1, keepdims=True)[0:1]
        t = lax.broadcasted_iota(jnp.int32, (n_exp, MAX_TILES_LANES), 1).astype(F32) * float(MOE_TILE)
        texp = jnp.sum(jnp.where(offs <= t, 1.0, 0.0), axis=0, keepdims=True) - 1.0
        valid = jnp.where(t[0:1] < total, 1.0, 0.0)
        lane_t = _lane_iota(texp.shape)
        first = jnp.where((lane_t == 0) | (texp != pltpu.roll(texp, 1, 1)), 1.0, 0.0)
        meta = jnp.concatenate([texp, valid, first, jnp.zeros((5, MAX_TILES_LANES), F32)], axis=0)
        meta_ref[...] = meta.astype(jnp.int32)

    @pl.when(phase == 1)
    def _():
        r = lax.broadcasted_iota(jnp.int32, (tp, tp), 0)
        c = lax.broadcasted_iota(jnp.int32, (tp, tp), 1)
        upper = jnp.where(r < c, 1.0, 0.0).astype(BF16)
        rank = _dot(sel.astype(BF16), upper)
        slot = offs_ref[:, 0:1] + carry_ref[:, 0:1] + rank
        pos_a = jnp.sum(jnp.where(sub == i1, slot, 0.0), axis=0, keepdims=True)
        pos_b = jnp.sum(jnp.where(sub == i2, slot, 0.0), axis=0, keepdims=True)
        pos = jnp.concatenate([pos_a, pos_b, jnp.zeros((6, tp), F32)], axis=0)
        pos_ref[...] = pos.astype(jnp.int32)
        carry_ref[...] += jnp.broadcast_to(tile_cnt, carry_ref.shape)


def _positions(route, *, n_exp, tp):
    n = route.shape[1]
    return pl.pallas_call(
        functools.partial(_pos_kernel, n_exp=n_exp),
        grid=(2, n // tp),
        in_specs=[pl.BlockSpec((8, tp), lambda p, i: (0, i))],
        out_specs=[pl.BlockSpec((8, tp), lambda p, i: (0, i * p)),
                   pl.BlockSpec((8, MAX_TILES_LANES), lambda p, i: (0, 0))],
        out_shape=[jax.ShapeDtypeStruct((8, n), jnp.int32),
                   jax.ShapeDtypeStruct((8, MAX_TILES_LANES), jnp.int32)],
        scratch_shapes=[pltpu.VMEM((n_exp, LANES), F32)] * 3,
        compiler_params=pltpu.CompilerParams(dimension_semantics=("arbitrary", "arbitrary")),
        name="moe_positions",
    )(route)


SC_CORES = 2
SC_SUBCORES = 16
SC_LANES = 16
SC_GATHER_ROWS = 64


def _sc_mesh():
    return plsc.VectorSubcoreMesh(core_axis_name="c", subcore_axis_name="s",
                                  num_cores=SC_CORES, num_subcores=SC_SUBCORES)


def _sc_invert(pos_a, pos_b, n_slots):
    n = pos_a.shape[0]

    def body(pa_hbm, pb_hbm, src_hbm, pa_v, pb_v, src_v):
        wid = lax.axis_index("s") * SC_CORES + lax.axis_index("c")

        @pl.when(wid == 0)
        def _():
            pltpu.sync_copy(pa_hbm, pa_v)
            pltpu.sync_copy(pb_hbm, pb_v)
            lane = lax.iota(jnp.int32, SC_LANES)

            @pl.loop(0, n_slots // SC_LANES)
            def _(j):
                off = pl.multiple_of(j * SC_LANES, SC_LANES)
                src_v[pl.ds(off, SC_LANES)] = lax.rem(lane + off, n)

            @pl.loop(0, n // SC_LANES)
            def _(j):
                off = pl.multiple_of(j * SC_LANES, SC_LANES)
                tok = lane + off
                plsc.store_scatter(src_v, [pa_v[pl.ds(off, SC_LANES)]], tok)
                plsc.store_scatter(src_v, [pb_v[pl.ds(off, SC_LANES)]], tok)

            pltpu.sync_copy(src_v, src_hbm)

    return pl.kernel(
        body, out_type=jax.ShapeDtypeStruct((n_slots,), jnp.int32), mesh=_sc_mesh(),
        scratch_types=[pltpu.VMEM((n,), jnp.int32), pltpu.VMEM((n,), jnp.int32),
                       pltpu.VMEM((n_slots,), jnp.int32)],
        compiler_params=pltpu.CompilerParams(needs_layout_passes=False),
        name="moe_invert",
    )(pos_a, pos_b)


def _sc_gather(table, idx):
    m = idx.shape[0]
    d = table.shape[1]
    n_workers = SC_CORES * SC_SUBCORES
    per_w = m // n_workers
    rows = SC_GATHER_ROWS

    n_pairs = per_w // (2 * rows)
    assert per_w == n_pairs * 2 * rows

    def body(table_hbm, idx_hbm, out_hbm, idx_v, buf_v, sems):
        wid = lax.axis_index("s") * SC_CORES + lax.axis_index("c")
        base = pl.multiple_of(wid * per_w, rows)
        pltpu.sync_copy(idx_hbm.at[pl.ds(base, per_w)], idx_v)

        def gather(chunk, slot):
            off = pl.multiple_of(chunk * rows, rows)
            return pltpu.make_async_copy(table_hbm.at[idx_v.at[pl.ds(off, rows)]], buf_v.at[slot],
                                         sems.at[slot])

        def write(chunk, slot):
            off = pl.multiple_of(chunk * rows, rows)
            pltpu.sync_copy(buf_v.at[slot], out_hbm.at[pl.ds(base + off, rows)])

        gather(0, 0).start()

        @pl.loop(0, n_pairs)
        def _(p):
            c0 = 2 * p
            gather(c0, 0).wait()
            gather(c0 + 1, 1).start()
            write(c0, 0)
            gather(c0 + 1, 1).wait()

            @pl.when(p + 1 < n_pairs)
            def _():
                gather(c0 + 2, 0).start()

            write(c0 + 1, 1)

    return pl.kernel(
        body, out_type=jax.ShapeDtypeStruct((m, d), table.dtype), mesh=_sc_mesh(),
        scratch_types=[pltpu.VMEM((per_w,), jnp.int32), pltpu.VMEM((2, rows, d), table.dtype),
                       pltpu.SemaphoreType.DMA((2,))],
        name="moe_gather",
    )(table, idx)


def _expert_kernel(meta_ref, xs_ref, wg_ref, wu_ref, wd_ref, ys_ref, wg_s, wu_s, wd_s):
    t = pl.program_id(0)
    valid = meta_ref[1, t] == 1

    @pl.when(valid & (meta_ref[2, t] == 1))
    def _():
        wg_s[...] = wg_ref[0, 0].astype(BF16)
        wu_s[...] = wu_ref[0, 0].astype(BF16)
        wd_s[...] = wd_ref[0, 0].astype(BF16)

    @pl.when(valid)
    def _():
        xb = _unpack_bf16_pairs(xs_ref[...]).astype(BF16)
        h = jax.nn.silu(_dot(xb, wg_s[...])) * _dot(xb, wu_s[...])
        ys_ref[...] = _pack_bf16_pairs(_dot(h.astype(BF16), wd_s[...]))

    @pl.when(jnp.logical_not(valid))
    def _():
        ys_ref[...] = jnp.zeros_like(ys_ref)


def _experts(meta, xs, wg, wu, wd, layer):
    n_slots, dp = xs.shape
    d, f = wg.shape[2], wg.shape[3]
    grid_spec = pltpu.PrefetchScalarGridSpec(
        num_scalar_prefetch=1,
        grid=(n_slots // MOE_TILE,),
        in_specs=[pl.BlockSpec((MOE_TILE, dp), lambda t, meta: (t, 0)),
                  pl.BlockSpec((1, 1, d, f), lambda t, meta: (layer, meta[0, t], 0, 0)),
                  pl.BlockSpec((1, 1, d, f), lambda t, meta: (layer, meta[0, t], 0, 0)),
                  pl.BlockSpec((1, 1, f, d), lambda t, meta: (layer, meta[0, t], 0, 0))],
        out_specs=pl.BlockSpec((MOE_TILE, dp), lambda t, meta: (t, 0)),
        scratch_shapes=[pltpu.VMEM((d, f), BF16), pltpu.VMEM((d, f), BF16), pltpu.VMEM((f, d), BF16)],
    )
    return pl.pallas_call(
        _expert_kernel, grid_spec=grid_spec,
        out_shape=jax.ShapeDtypeStruct((n_slots, dp), jnp.int32),
        compiler_params=pltpu.CompilerParams(dimension_semantics=("arbitrary",),
                                             vmem_limit_bytes=VMEM_LIMIT),
        name="moe_experts",
    )(meta, xs, wg, wu, wd)


def _combine_kernel(x_ref, ya_ref, yb_ref, route_ref, lng_ref, lnb_ref, o_ref, *, alpha):
    w = route_ref[...].T
    moe = w[:, 2:3] * _unpack_bf16_pairs(ya_ref[...]) + w[:, 3:4] * _unpack_bf16_pairs(yb_ref[...])
    o_ref[...] = _layer_norm(alpha * x_ref[...] + moe, lng_ref[0], lnb_ref[0])


def _combine(x1, yab, route, lng, lnb, *, layer, tn, alpha):
    n, d = x1.shape
    row = pl.BlockSpec((tn, d), lambda i: (i, 0))
    tiles = n // tn
    return pl.pallas_call(
        functools.partial(_combine_kernel, alpha=alpha),
        grid=(tiles,),
        in_specs=[row,
                  pl.BlockSpec((tn, d // 2), lambda i: (i, 0)),
                  pl.BlockSpec((tn, d // 2), lambda i: (i + tiles, 0)),
                  pl.BlockSpec((8, tn), lambda i: (0, i)),
                  _layer_block(lng, layer), _layer_block(lnb, layer)],
        out_specs=row,
        out_shape=jax.ShapeDtypeStruct((n, d), F32),
        compiler_params=pltpu.CompilerParams(dimension_semantics=("arbitrary",)),
        name="moe_combine",
    )(x1, yab, yab, route, lng, lnb)


def _moe(x1, xpk, route, wg, wu, wd, lng, lnb, *, layer, alpha):
    n, d = x1.shape
    n_exp = wg.shape[1]
    n_slots = (2 * n // MOE_TILE + n_exp) * MOE_TILE
    assert n_slots // MOE_TILE <= MAX_TILES_LANES
    pos, meta = _positions(route, n_exp=n_exp, tp=min(1024, n))
    src = _sc_invert(pos[0], pos[1], n_slots)
    xs = _sc_gather(xpk, src)
    ys = _experts(meta, xs, wg, wu, wd, layer)
    yab = _sc_gather(ys, pos[:2].reshape(2 * n))
    return _combine(x1, yab, route, lng, lnb, layer=layer, tn=min(512, n), alpha=alpha)


def _dup(v):
    return jnp.concatenate([v, v], axis=-1)


def kernel(x, mem, w_in, w_gate, b_gate, ssm_lam_re, ssm_lam_im, ssm_log_step, ssm_b_re, ssm_b_im, ssm_c_re, ssm_c_im, ssm_d, w_glu, b_glu, gmlp_ln_g, gmlp_ln_b, w_spatial, b_spatial, w_kv, w_br, w_out, ln1_g, ln1_b, w_router_g, b_router_g, w_router_e, b_router_e, w_exp_gate, w_exp_up, w_exp_down, ln2_g, ln2_b):
    bsz, seq, d = x.shape
    depth = w_in.shape[0]
    n = bsz * seq
    n_grp = ssm_lam_re.shape[2]
    ssm_w = n_grp * GROUP_CH
    gmlp_w = gmlp_ln_g.shape[1]
    n_heads = 4
    n_moe_grp, exp_per_grp = b_router_e.shape[1], b_router_e.shape[2]
    alpha = (2.0 * depth) ** 0.25
    tn = 512

    ls = jnp.broadcast_to(ssm_log_step[..., None], ssm_lam_re.shape)
    six = [ssm_lam_re[:, 0], ssm_lam_im[:, 0], ssm_lam_re[:, 1], ssm_lam_im[:, 1], ls[:, 0], ls[:, 1]]
    prow = jnp.stack([_dup(v) for v in six] + [jnp.zeros_like(_dup(six[0]))] * 2, axis=2)
    pcol = jnp.stack(six + [jnp.zeros_like(six[0])] * 2, axis=-1)
    bt2 = jnp.concatenate([jnp.swapaxes(ssm_b_re, 2, 3), jnp.swapaxes(ssm_b_im, 2, 3)], axis=-1)
    c2 = jnp.concatenate([ssm_c_re, ssm_c_im], axis=-1)
    ct2 = jnp.concatenate([jnp.swapaxes(ssm_c_re, 2, 3), jnp.swapaxes(ssm_c_im, 2, 3)], axis=-1)
    wtab, pst, gmat, dec = _ssm_tables(prow, pcol, bt2, c2, ct2)
    dvec = jnp.repeat(ssm_d.reshape(depth, n_grp, 1, GROUP_CH), SSM_T, axis=-1)

    assert w_in.shape[2] == 4 * ssm_w and gmlp_w == ssm_w
    w_in_b = w_in.astype(BF16)
    wssm_t = jnp.swapaxes(w_in[:, :, :ssm_w], 1, 2).astype(BF16)
    w_gate_b = w_gate.astype(BF16)
    b_gate3 = b_gate.reshape(depth, 1, -1)
    w_br_b = w_br.astype(BF16)
    w_out_b = w_out.astype(BF16)
    w_kv_b = w_kv.astype(BF16)
    w_glu_t = jnp.swapaxes(w_glu, 1, 2).astype(BF16)
    b_glu_c = b_glu.reshape(depth, -1, 1)
    w_sp_b = w_spatial.astype(BF16)
    b_sp_t = jnp.swapaxes(b_spatial, 1, 2)
    row3 = lambda v: v.reshape(depth, 1, -1)
    wrg_t = jnp.repeat(jnp.swapaxes(w_router_g, 1, 2), exp_per_grp, axis=1)
    wre_t = jnp.transpose(w_router_e, (0, 1, 3, 2)).reshape(depth, n_moe_grp * exp_per_grp, d)
    wr = jnp.concatenate([wrg_t, wre_t], axis=1)
    wr_hi = wr.astype(BF16)
    wr_hl = jnp.concatenate([wr_hi, (wr - wr_hi.astype(F32)).astype(BF16)], axis=1)
    br = jnp.concatenate([jnp.repeat(b_router_g, exp_per_grp, axis=1),
                          b_router_e.reshape(depth, -1)], axis=1).reshape(depth, -1, 1)

    n_streams = 1
    ns = n // n_streams
    xin = [x.reshape(n, d)] * n_streams
    row_off = [s * ns for s in range(n_streams)]
    for l in range(depth):
        kv = _kv_proj(mem, w_kv_b, l)
        zs, m12s = [], []
        for s in range(n_streams):
            z, m12 = _front(
                xin[s], wssm_t, w_in_b, w_gate_b, b_gate3, row3(gmlp_ln_g), row3(gmlp_ln_b), w_sp_b,
                b_sp_t, kv, w_br_b, layer=l, tn=tn, n_heads=n_heads, tokens_per_batch=seq,
                n=ns, row_off=row_off[s], batch_off=s * (bsz // n_streams))
            zs.append(z)
            m12s.append(m12)
        ys = _ssm_apply_pairs(zs, wtab, pst, gmat, dec, dvec, l, seq // LANES)
        nxt = []
        for s in range(n_streams):
            x1, xpk, route = _back(
                xin[s], m12s[s], ys[s], w_glu_t, b_glu_c, w_br_b, w_gate_b, b_gate3, w_out_b,
                row3(ln1_g), row3(ln1_b), wr_hl, br, layer=l, tn=tn, alpha=alpha, row_off=row_off[s])
            nxt.append(_moe(x1, xpk, route, w_exp_gate, w_exp_up, w_exp_down, row3(ln2_g), row3(ln2_b),
                            layer=l, alpha=alpha))
        xin = nxt
        row_off = [0] * n_streams
    return jnp.concatenate(xin, axis=0).reshape(bsz, seq, d)
```

```python
import functools
import math

import jax
import jax.numpy as jnp
from jax import lax
from jax.experimental import pallas as pl
from jax.experimental.pallas import tpu as pltpu
from jax.experimental.pallas import tpu_sc as plsc

F32 = jnp.float32
BF16 = jnp.bfloat16

LANES = 128
HALF = LANES // 2
SSM_T = HALF
GROUP_CH = 16
N_STATE = 64
LN_EPS = 1e-5
VMEM_LIMIT = 56 * 1024 * 1024

NT_DIMS = (((1,), (1,)), ((), ()))
TN_DIMS = (((0,), (0,)), ((), ()))


def _dot(a, b):
    return jnp.dot(a, b, preferred_element_type=F32)


def _dot_hi(a, b):
    return jnp.dot(a, b, preferred_element_type=F32, precision=lax.Precision.HIGHEST)


def _layer_norm(x, g, b):
    mu = jnp.mean(x, axis=-1, keepdims=True)
    xc = x - mu
    var = jnp.mean(xc * xc, axis=-1, keepdims=True)
    return xc * lax.rsqrt(var + LN_EPS) * g + b


def _lane_iota(shape):
    return lax.broadcasted_iota(jnp.int32, shape, len(shape) - 1)


def _swap_halves(x):
    return pltpu.roll(x, HALF, 1)


def _ssm_tables_kernel(prow_ref, pcol_ref, bt_ref, c_ref, ct_ref,
                       wtab_ref, pst_ref, g_ref, dec_ref):
    T = SSM_T
    prow = prow_ref[0, 0]
    pcol = pcol_ref[0, 0]
    bt2 = bt_ref[0, 0]
    c2 = c_ref[0, 0]
    ct = ct_ref[0, 0]
    lane1 = _lane_iota((1, LANES))
    lo1 = lane1 < HALF
    sgn1 = jnp.where(lo1, -1.0, 1.0).astype(F32)

    lane_bc = _lane_iota((GROUP_CH, LANES))
    lo_bc = lane_bc < HALF
    c2s = jnp.where(lo_bc, -1.0, 1.0) * _swap_halves(c2)

    bbar = []
    a_row = []
    th_row = []
    for d in range(2):
        lre = prow[2 * d:2 * d + 1]
        lim = prow[2 * d + 1:2 * d + 2]
        dt = jnp.exp(prow[4 + d:5 + d])
        a = lre * dt
        th = lim * dt
        er = jnp.exp(a)
        lb_re = er * jnp.cos(th)
        lb_im = er * jnp.sin(th)
        num_re = lb_re - 1.0
        den = lre * lre + lim * lim
        coef_re = (num_re * lre + lb_im * lim) / den
        coef_im = (lb_im * lre - num_re * lim) / den
        bbar.append(coef_re * bt2 + (sgn1 * coef_im) * _swap_halves(bt2))
        a_row.append(a)
        th_row.append(th)

    s_col = lax.broadcasted_iota(jnp.int32, (T, 1), 0).astype(F32)
    for d in range(2):
        k = (T - 1.0) - s_col if d == 0 else s_col
        e = jnp.exp(k * a_row[d])
        ang = k * th_row[d]
        pr = e * jnp.cos(ang)
        pi = e * jnp.sin(ang)
        bsw = sgn1 * _swap_halves(bbar[d])
        for hp in range(GROUP_CH):
            blk = pr * bbar[d][hp:hp + 1] + pi * bsw[hp:hp + 1]
            pst_ref[0, 0, hp * T:(hp + 1) * T, d * LANES:(d + 1) * LANES] = blk.astype(pst_ref.dtype)

    for d in range(2):
        e = jnp.exp(float(T) * a_row[d])
        ang = float(T) * th_row[d]
        dec_ref[0, 0, :, d * LANES:(d + 1) * LANES] = e * jnp.where(lo1, jnp.cos(ang), jnp.sin(ang))

    lane_p = _lane_iota((N_STATE, LANES))
    lo_p = lane_p < HALF
    t_lane = jnp.where(lo_p, lane_p, lane_p - HALF).astype(F32)
    pw = []
    for d in range(2):
        dtc = jnp.exp(pcol[:, 4 + d:5 + d])
        a_c = pcol[:, 2 * d:2 * d + 1] * dtc
        th_c = pcol[:, 2 * d + 1:2 * d + 2] * dtc
        pw.append((a_c, th_c))
    for d in range(2):
        a_c, th_c = pw[d]
        k = t_lane + 1.0 if d == 0 else float(T) - t_lane
        e = jnp.exp(k * a_c)
        pr = e * jnp.cos(k * th_c)
        pi = e * jnp.sin(k * th_c)
        for j in range(GROUP_CH // 2):
            cre = jnp.where(lo_p, ct[:, 2 * j:2 * j + 1], ct[:, 2 * j + 1:2 * j + 2])
            cim = jnp.where(lo_p, ct[:, GROUP_CH + 2 * j:GROUP_CH + 2 * j + 1],
                            ct[:, GROUP_CH + 2 * j + 1:GROUP_CH + 2 * j + 2])
            g_re = cre * pr - cim * pi
            g_im = -(cre * pi + cim * pr)
            g_ref[0, 0, d * LANES:d * LANES + N_STATE, j * LANES:(j + 1) * LANES] = g_re.astype(g_ref.dtype)
            g_ref[0, 0, d * LANES + N_STATE:(d + 1) * LANES, j * LANES:(j + 1) * LANES] = g_im.astype(g_ref.dtype)

    res = []
    kb0 = None
    for d in range(2):
        a_c, th_c = pw[d]
        if d == 0:
            k = jnp.maximum(lane_p - HALF, 0).astype(F32)
        else:
            k = jnp.maximum(HALF - lane_p, 0).astype(F32)
        e = jnp.exp(k * a_c)
        rhs = jnp.concatenate([e * jnp.cos(k * th_c), -(e * jnp.sin(k * th_c))], axis=0)
        bsw = _swap_halves(bbar[d])
        b_re = jnp.where(lo_bc, bbar[d], bsw)
        b_im = jnp.where(lo_bc, bsw, bbar[d])
        bc = (b_re[:, None, :] * c2[None, :, :] + b_im[:, None, :] * c2s[None, :, :])
        bc = bc.reshape(GROUP_CH * GROUP_CH, LANES)
        res.append(_dot_hi(bc, rhs))
        if d == 1:
            lane_bcf = _lane_iota(bc.shape)
            kb0 = jnp.sum(jnp.where(lane_bcf < HALF, bc, 0.0), axis=1, keepdims=True)
    lane_w = _lane_iota(res[0].shape)
    wtab = jnp.where(lane_w >= HALF, res[0], res[1])
    wtab_ref[0, 0] = wtab + jnp.where(lane_w == HALF, kb0, 0.0)


def _ssm_tables(prow, pcol, bt2, c2, ct2):
    L, G = prow.shape[0], prow.shape[1]
    T = SSM_T
    blk = lambda shape: pl.BlockSpec((1, 1) + shape, lambda l, g: (l, g, 0, 0))
    return pl.pallas_call(
        _ssm_tables_kernel,
        grid=(L, G),
        in_specs=[blk((8, LANES)), blk((N_STATE, 8)), blk((GROUP_CH, LANES)),
                  blk((GROUP_CH, LANES)), blk((N_STATE, 2 * GROUP_CH))],
        out_specs=[blk((GROUP_CH * GROUP_CH, LANES)), blk((GROUP_CH * T, 2 * LANES)),
                   blk((2 * LANES, GROUP_CH * T)), blk((1, 2 * LANES))],
        out_shape=[jax.ShapeDtypeStruct((L, G, GROUP_CH * GROUP_CH, LANES), F32),
                   jax.ShapeDtypeStruct((L, G, GROUP_CH * T, 2 * LANES), BF16),
                   jax.ShapeDtypeStruct((L, G, 2 * LANES, GROUP_CH * T), BF16),
                   jax.ShapeDtypeStruct((L, G, 1, 2 * LANES), F32)],
        name="ssm_tables",
    )(prow, pcol, bt2, c2, ct2)


def _cmul_packed(x, d_re, d_sw):
    return x * d_re + _swap_halves(x) * d_sw


SSM_GROUPS_PER_STEP = 2


def _toeplitz_build(m_ref, wtab_ref, k):
    T = SSM_T
    lo_t = _lane_iota((T, LANES)) < HALF
    for hp in range(GROUP_CH):
        for j in range(GROUP_CH // 2):
            r = hp * GROUP_CH + 2 * j
            wa = jnp.broadcast_to(wtab_ref[0, k, r:r + 1, :], (T, LANES))
            wb = jnp.broadcast_to(wtab_ref[0, k, r + 1:r + 2, :], (T, LANES))
            ra = pltpu.roll(wa, HALF, 1, stride=1, stride_axis=0)
            rb = pltpu.roll(wb, 0, 1, stride=1, stride_axis=0)
            m_ref[hp * T:(hp + 1) * T, j * LANES:(j + 1) * LANES] = (
                jnp.where(lo_t, ra, rb).astype(m_ref.dtype))


def _ssm_group(z_refs, o_refs, k, m_ref, pst, gmat, dec, dvec, rows_per_batch):
    rows_s = z_refs[0].shape[1] // GROUP_CH
    n_rows = len(z_refs) * rows_s
    lane = _lane_iota((n_rows, LANES))
    lo = lane < HALF
    a0, a1 = [], []
    for j in range(GROUP_CH // 2):
        pe = jnp.concatenate([z[k, pl.ds(2 * j, rows_s, stride=GROUP_CH), :] for z in z_refs], axis=0)
        po = jnp.concatenate([z[k, pl.ds(2 * j + 1, rows_s, stride=GROUP_CH), :] for z in z_refs], axis=0)
        a0.append(jnp.where(lo, pe, _swap_halves(po)))
        a1.append(jnp.where(lo, _swap_halves(pe), po))
    a_f32 = jnp.concatenate([jnp.concatenate(a0, axis=1), jnp.concatenate(a1, axis=1)], axis=0)
    a_bf = a_f32.astype(BF16)

    out = _dot(a_bf, m_ref[...]) + dvec * a_f32
    s = _dot(a_bf, pst)
    sf0, sb0 = s[:n_rows, :LANES], s[:n_rows, LANES:]
    sf1, sb1 = s[n_rows:, :LANES], s[n_rows:, LANES:]

    lo1 = _lane_iota((1, LANES)) < HALF
    sgn = jnp.where(lo1, -1.0, 1.0).astype(F32)

    def split(p):
        sw = _swap_halves(p)
        return jnp.where(lo1, p, sw), sgn * jnp.where(lo1, sw, p)

    def square(p):
        d_re, d_sw = split(p)
        return _cmul_packed(p, d_re, d_sw)

    dec_f, dec_b = dec[:, :LANES], dec[:, LANES:]
    df_re, df_sw = split(dec_f)
    db_re, db_sw = split(dec_b)

    row = lax.broadcasted_iota(jnp.int32, (n_rows, LANES), 0)
    rib = row % rows_per_batch

    ef = _cmul_packed(sf0, df_re, df_sw) + sf1
    eb = sb0 + _cmul_packed(sb1, db_re, db_sw)
    pf, pb = square(dec_f), square(dec_b)
    step = 1
    while step < rows_per_batch:
        pf_re, pf_sw = split(pf)
        pb_re, pb_sw = split(pb)
        shf = jnp.where(rib >= step, pltpu.roll(ef, step, 0), 0.0)
        ef = ef + _cmul_packed(shf, pf_re, pf_sw)
        shb = jnp.where(rib < rows_per_batch - step, pltpu.roll(eb, n_rows - step, 0), 0.0)
        eb = eb + _cmul_packed(shb, pb_re, pb_sw)
        pf, pb = square(pf), square(pb)
        step *= 2
    hf0 = jnp.where(rib >= 1, pltpu.roll(ef, 1, 0), 0.0)
    hf1 = _cmul_packed(hf0, df_re, df_sw) + sf0
    hb1 = jnp.where(rib < rows_per_batch - 1, pltpu.roll(eb, n_rows - 1, 0), 0.0)
    hb0 = sb1 + _cmul_packed(hb1, db_re, db_sw)
    h_in = jnp.concatenate([jnp.concatenate([hf0, hb0], axis=1),
                            jnp.concatenate([hf1, hb1], axis=1)], axis=0)
    out = out + _dot(h_in.astype(BF16), gmat)

    o0, o1 = out[:n_rows], out[n_rows:]
    for j in range(GROUP_CH // 2):
        t0 = o0[:, j * LANES:(j + 1) * LANES]
        t1 = o1[:, j * LANES:(j + 1) * LANES]
        even = jnp.where(lo, t0, _swap_halves(t1))
        odd = jnp.where(lo, _swap_halves(t0), t1)
        for si, o_ref in enumerate(o_refs):
            o_ref[k, pl.ds(2 * j, rows_s, stride=GROUP_CH), :] = even[si * rows_s:(si + 1) * rows_s]
            o_ref[k, pl.ds(2 * j + 1, rows_s, stride=GROUP_CH), :] = odd[si * rows_s:(si + 1) * rows_s]


def _ssm_pair_kernel(*refs, n_streams, rows_per_batch):
    z_refs = refs[:n_streams]
    wtab_ref, wnext_ref, pst_ref, g_ref, dec_ref, dvec_ref = refs[n_streams:n_streams + 6]
    o_refs = refs[n_streams + 6:2 * n_streams + 6]
    m_a, m_b = refs[2 * n_streams + 6:2 * n_streams + 8]

    @pl.when(pl.program_id(0) == 0)
    def _():
        _toeplitz_build(m_a, wtab_ref, 0)

    def group(k, m_ref):
        _ssm_group(z_refs, o_refs, k, m_ref, pst_ref[0, k], g_ref[0, k], dec_ref[0, k], dvec_ref[0, k],
                   rows_per_batch)

    _toeplitz_build(m_b, wtab_ref, 1)
    group(0, m_a)
    _toeplitz_build(m_a, wnext_ref, 0)
    group(1, m_b)


def _ssm_apply_pairs(zs, wtab, pst, gmat, dec, dvec, layer, rows_per_batch):
    G, R, _ = zs[0].shape
    T = SSM_T
    gp = SSM_GROUPS_PER_STEP
    n_streams = len(zs)
    tab = lambda shape: pl.BlockSpec((1, gp) + shape, lambda g: (layer, g, 0, 0))
    nxt = pl.BlockSpec((1, 1, GROUP_CH * GROUP_CH, LANES),
                       lambda g: (layer, jnp.minimum(gp * g + gp, G - 1), 0, 0))
    seq_block = pl.BlockSpec((gp, R, LANES), lambda g: (g, 0, 0))
    return pl.pallas_call(
        functools.partial(_ssm_pair_kernel, n_streams=n_streams, rows_per_batch=rows_per_batch),
        grid=(G // gp,),
        in_specs=[seq_block] * n_streams + [
            tab((GROUP_CH * GROUP_CH, LANES)), nxt, tab((GROUP_CH * T, 2 * LANES)),
            tab((2 * LANES, GROUP_CH * T)), tab((1, 2 * LANES)), tab((1, GROUP_CH * T))],
        out_specs=[seq_block] * n_streams,
        out_shape=[jax.ShapeDtypeStruct(zs[0].shape, F32)] * n_streams,
        scratch_shapes=[pltpu.VMEM((GROUP_CH * T, GROUP_CH * T), BF16)] * 2,
        compiler_params=pltpu.CompilerParams(dimension_semantics=("arbitrary",),
                                             vmem_limit_bytes=VMEM_LIMIT),
        name="ssm_seq",
    )(*zs, wtab, wtab, pst, gmat, dec, dvec)


def _kv_kernel(mem_ref, w_ref, o_ref):
    o_ref[0] = _dot(mem_ref[0].astype(BF16), w_ref[0]).astype(o_ref.dtype)


def _kv_proj(mem, w_kv, layer):
    B, M, D = mem.shape
    width = w_kv.shape[2]
    return pl.pallas_call(
        _kv_kernel,
        grid=(B,),
        in_specs=[pl.BlockSpec((1, M, D), lambda b: (b, 0, 0)),
                  pl.BlockSpec((1, D, width), lambda b: (layer, 0, 0))],
        out_specs=pl.BlockSpec((1, M, width), lambda b: (b, 0, 0)),
        out_shape=jax.ShapeDtypeStruct((B, M, width), BF16),
        name="kv_proj",
    )(mem, w_kv)


def _front_kernel(x_ref, wssm_t_ref, wu_ref, wv_ref, wq_ref, wg1_ref, wg2_ref, bg1_ref, bg2_ref,
                  lng_ref, lnb_ref, ws_ref, bs_t_ref, kv_ref, wbr1_ref, wbr2_ref, z_ref, m_ref, *, n_heads):
    tn, d_model = x_ref.shape
    xb = x_ref[...].astype(BF16)

    u_t = lax.dot_general(wssm_t_ref[0], xb, NT_DIMS, preferred_element_type=F32)
    n_grp = u_t.shape[0] // GROUP_CH
    for a in range(tn // LANES):
        z_ref[:, a * GROUP_CH:(a + 1) * GROUP_CH, :] = (
            u_t[:, a * LANES:(a + 1) * LANES].reshape(n_grp, GROUP_CH, LANES))

    u = jax.nn.gelu(_dot(xb, wu_ref[0]))
    v = jax.nn.gelu(_dot(xb, wv_ref[0]))
    half = v.shape[1]
    vb = _layer_norm(v, lng_ref[0], lnb_ref[0]).astype(BF16)
    lane = _lane_iota((LANES, LANES))
    lo = lane < HALF
    bs_t = bs_t_ref[0]
    s_rows = []
    for a in range(tn // LANES):
        va = vb[a * LANES:(a + 1) * LANES]
        tiles = []
        for j in range(half // LANES):
            rhs = va[:, j * LANES:(j + 1) * LANES]
            ev = _dot(ws_ref[0, 2 * j], rhs)
            od = _dot(ws_ref[0, 2 * j + 1], rhs)
            bias = jnp.where(lo, bs_t[:, 2 * j:2 * j + 1], bs_t[:, 2 * j + 1:2 * j + 2])
            tiles.append(jnp.where(lo, ev, od) + bias)
        s_rows.append(jnp.concatenate(tiles, axis=1))
    y_gmlp = u * jnp.concatenate(s_rows, axis=0)

    q = _dot(xb, wq_ref[0])
    hd = q.shape[1] // n_heads
    kv = kv_ref[0]
    outs = []
    for h in range(n_heads):
        qh = q[:, h * hd:(h + 1) * hd].astype(BF16)
        kh = kv[:, h * hd:(h + 1) * hd]
        vh = kv[:, q.shape[1] + h * hd:q.shape[1] + (h + 1) * hd]
        sc = lax.dot_general(qh, kh, NT_DIMS, preferred_element_type=F32) * (hd ** -0.5)
        sc = sc - jnp.max(sc, axis=-1, keepdims=True)
        p = jnp.exp(sc)
        p = p / jnp.sum(p, axis=-1, keepdims=True)
        outs.append(_dot(p.astype(BF16), vh))
    y_mem = jnp.concatenate(outs, axis=1)

    g1 = jax.nn.sigmoid(_dot(xb, wg1_ref[0]) + bg1_ref[0])
    g2 = jax.nn.sigmoid(_dot(xb, wg2_ref[0]) + bg2_ref[0])
    m_ref[...] = (g1 * _dot(y_gmlp.astype(BF16), wbr1_ref[0, 0])
                  + g2 * _dot(y_mem.astype(BF16), wbr2_ref[0, 0]))


def _layer_block(arr, layer, col_block=None, col=0, branch=None):
    if branch is not None:
        return pl.BlockSpec((1, 1) + arr.shape[2:], lambda *_: (layer, branch, 0, 0))
    shape = arr.shape[1:]
    if col_block is not None:
        shape = shape[:-1] + (col_block,)
    nd = len(shape)
    return pl.BlockSpec((1,) + shape, lambda *_: (layer,) + (0,) * (nd - 1) + (col,))


def _front(x2, wssm_t, w_in, w_gate, b_gate, lng, lnb, ws, bs_t, kv, w_br, *, layer, tn, n_heads,
           tokens_per_batch, n, row_off, batch_off):
    d = x2.shape[1]
    n_grp = wssm_t.shape[1] // GROUP_CH
    width = wssm_t.shape[1]
    rows = tn // LANES * GROUP_CH
    tiles_per_batch = tokens_per_batch // tn
    tile_off = row_off // tn
    lb = functools.partial(_layer_block, layer=layer)
    return pl.pallas_call(
        functools.partial(_front_kernel, n_heads=n_heads),
        grid=(n // tn,),
        in_specs=[pl.BlockSpec((tn, d), lambda i: (i + tile_off, 0)),
                  lb(wssm_t),
                  lb(w_in, col_block=width, col=1), lb(w_in, col_block=width, col=2),
                  lb(w_in, col_block=width, col=3),
                  lb(w_gate, col_block=d, col=1), lb(w_gate, col_block=d, col=2),
                  lb(b_gate, col_block=d, col=1), lb(b_gate, col_block=d, col=2),
                  lb(lng), lb(lnb), lb(ws), lb(bs_t),
                  pl.BlockSpec((1,) + kv.shape[1:], lambda i: (i // tiles_per_batch + batch_off, 0, 0)),
                  lb(w_br, branch=1), lb(w_br, branch=2)],
        out_specs=[pl.BlockSpec((n_grp, rows, LANES), lambda i: (0, i, 0)),
                   pl.BlockSpec((tn, d), lambda i: (i, 0))],
        out_shape=[jax.ShapeDtypeStruct((n_grp, n // LANES * GROUP_CH, LANES), F32),
                   jax.ShapeDtypeStruct((n, d), F32)],
        compiler_params=pltpu.CompilerParams(dimension_semantics=("arbitrary",),
                                             vmem_limit_bytes=VMEM_LIMIT),
        name="mixer_front",
    )(x2, wssm_t, w_in, w_in, w_in, w_gate, w_gate, b_gate, b_gate, lng, lnb, ws, bs_t, kv, w_br, w_br)


def _route_t(lg, le):
    n_exp = float(le.shape[0])
    sub = lax.broadcasted_iota(jnp.int32, le.shape, 0).astype(F32)
    gmax = jnp.max(lg, axis=0, keepdims=True)
    denom = jnp.sum(jnp.exp(lg - gmax), axis=0, keepdims=True) * (1.0 / 8.0)
    pg_sel = 1.0 / denom
    first = jnp.min(jnp.where(lg == gmax, sub, n_exp), axis=0, keepdims=True)
    in_grp = jnp.floor(sub * 0.125) == jnp.floor(first * 0.125)
    neg = -jnp.inf
    l1 = jnp.where(in_grp, le, neg)
    m1 = jnp.max(l1, axis=0, keepdims=True)
    i1 = jnp.min(jnp.where(l1 == m1, sub, n_exp), axis=0, keepdims=True)
    l2 = jnp.where(sub == i1, neg, l1)
    m2 = jnp.max(l2, axis=0, keepdims=True)
    i2 = jnp.min(jnp.where(l2 == m2, sub, n_exp), axis=0, keepdims=True)
    e2 = jnp.exp(m2 - m1)
    p1 = 1.0 / (1.0 + e2)
    p2 = e2 / (1.0 + e2)
    return i1, i2, pg_sel * p1, pg_sel * p2


def _pack_bf16_pairs(x):
    c = x.shape[1] // 2
    rounded = x.astype(BF16).astype(F32)
    bits = lax.bitcast_convert_type(rounded, jnp.uint32)
    word = (bits[:, :c] >> 16) | bits[:, c:]
    return lax.bitcast_convert_type(word, jnp.int32)


def _unpack_bf16_pairs(w):
    bits = lax.bitcast_convert_type(w, jnp.uint32)
    lo = lax.bitcast_convert_type(bits << 16, F32)
    hi = lax.bitcast_convert_type(bits & jnp.uint32(0xFFFF0000), F32)
    return jnp.concatenate([lo, hi], axis=1)


def _back_kernel(x_ref, m12_ref, y_ref, wglu_t_ref, bglu_ref, wbr0_ref, wg0_ref, bg0_ref, wout_ref,
                 lng_ref, lnb_ref, wr_ref, br_ref, x1_ref, xpk_ref, route_ref, *, alpha):
    tn = x_ref.shape[0]
    x = x_ref[...]
    xb = x.astype(BF16)
    n_grp = y_ref.shape[0]
    cols = []
    for a in range(tn // LANES):
        y_t = y_ref[:, a * GROUP_CH:(a + 1) * GROUP_CH, :].reshape(n_grp * GROUP_CH, LANES)
        y_t = jax.nn.gelu(y_t)
        gate = jax.nn.sigmoid(_dot(wglu_t_ref[0], y_t.astype(BF16)) + bglu_ref[0])
        cols.append((y_t * gate).astype(BF16))
    ys_t = jnp.concatenate(cols, axis=1)
    br0 = lax.dot_general(ys_t, wbr0_ref[0, 0], TN_DIMS, preferred_element_type=F32)
    g0 = jax.nn.sigmoid(_dot(xb, wg0_ref[0]) + bg0_ref[0])
    merged = m12_ref[...] + g0 * br0
    h = _dot(merged.astype(BF16), wout_ref[0])
    x1 = _layer_norm(alpha * x + h, lng_ref[0], lnb_ref[0])
    x1_ref[...] = x1
    xpk_ref[...] = _pack_bf16_pairs(x1)
    x_hi = x1.astype(BF16)
    x_lo = (x1 - x_hi.astype(F32)).astype(BF16)
    w_hl = wr_ref[0]
    n_r = w_hl.shape[0] // 2
    part = lax.dot_general(w_hl, x_hi, NT_DIMS, preferred_element_type=F32)
    logits = (part[:n_r] + part[n_r:]
              + lax.dot_general(w_hl[:n_r], x_lo, NT_DIMS, preferred_element_type=F32) + br_ref[0])
    n_exp = n_r // 2
    i1, i2, w1, w2 = _route_t(logits[:n_exp], logits[n_exp:])
    route_ref[...] = jnp.concatenate([i1, i2, w1, w2, jnp.zeros((4, tn), F32)], axis=0)


def _back(x2, m12, y, wglu_t, bglu, w_br, w_gate, b_gate, wout, lng, lnb, wr, br, *, layer, tn, alpha,
          row_off):
    n, d = m12.shape
    n_grp = y.shape[0]
    rows = tn // LANES * GROUP_CH
    tile_off = row_off // tn
    lb = functools.partial(_layer_block, layer=layer)
    return pl.pallas_call(
        functools.partial(_back_kernel, alpha=alpha),
        grid=(n // tn,),
        in_specs=[pl.BlockSpec((tn, d), lambda i: (i + tile_off, 0)),
                  pl.BlockSpec((tn, d), lambda i: (i, 0)),
                  pl.BlockSpec((n_grp, rows, LANES), lambda i: (0, i, 0)),
                  lb(wglu_t), lb(bglu), lb(w_br, branch=0),
                  lb(w_gate, col_block=d, col=0), lb(b_gate, col_block=d, col=0), lb(wout),
                  lb(lng), lb(lnb), lb(wr), lb(br)],
        out_specs=[pl.BlockSpec((tn, d), lambda i: (i, 0)),
                   pl.BlockSpec((tn, d // 2), lambda i: (i, 0)),
                   pl.BlockSpec((8, tn), lambda i: (0, i))],
        out_shape=[jax.ShapeDtypeStruct((n, d), F32),
                   jax.ShapeDtypeStruct((n, d // 2), jnp.int32),
                   jax.ShapeDtypeStruct((8, n), F32)],
        compiler_params=pltpu.CompilerParams(dimension_semantics=("arbitrary",),
                                             vmem_limit_bytes=VMEM_LIMIT),
        name="mixer_back",
    )(x2, m12, y, wglu_t, bglu, w_br, w_gate, b_gate, wout, lng, lnb, wr, br)


MOE_TILE = 256
MAX_TILES_LANES = 256


def _pos_kernel(route_ref, pos_ref, meta_ref, cnt_ref, offs_ref, carry_ref, *, n_exp):
    phase = pl.program_id(0)
    i = pl.program_id(1)
    tp = route_ref.shape[1]
    sub = lax.broadcasted_iota(jnp.int32, (n_exp, tp), 0).astype(F32)
    i1 = route_ref[0:1, :]
    i2 = route_ref[1:2, :]
    sel = jnp.where((sub == i1) | (sub == i2), 1.0, 0.0)
    tile_cnt = jnp.sum(sel, axis=1, keepdims=True)

    @pl.when((phase == 0) & (i == 0))
    def _():
        cnt_ref[...] = jnp.zeros_like(cnt_ref)

    @pl.when(phase == 0)
    def _():
        cnt_ref[...] += jnp.broadcast_to(tile_cnt, cnt_ref.shape)

    @pl.when((phase == 1) & (i == 0))
    def _():
        cnt = cnt_ref[...]
        padded = jnp.ceil(cnt * (1.0 / MOE_TILE)) * float(MOE_TILE)
        sub_e = lax.broadcasted_iota(jnp.int32, cnt.shape, 0)
        lane_e = lax.broadcasted_iota(jnp.int32, cnt.shape, 1)
        row = jnp.sum(jnp.where(sub_e == lane_e, padded, 0.0), axis=0, keepdims=True)
        offs = jnp.sum(jnp.where(lane_e < sub_e, row, 0.0), axis=1, keepdims=True)
        offs_ref[...] = jnp.broadcast_to(offs, offs_ref.shape)
        carry_ref[...] = jnp.zeros_like(carry_ref)
        total = jnp.sum(jnp.where(lane_e < n_exp, row, 0.0), axis=1, keepdims=True)[0:1]
        t = lax.broadcasted_iota(jnp.int32, (n_exp, MAX_TILES_LANES), 1).astype(F32) * float(MOE_TILE)
        texp = jnp.sum(jnp.where(offs <= t, 1.0, 0.0), axis=0, keepdims=True) - 1.0
        valid = jnp.where(t[0:1] < total, 1.0, 0.0)
        lane_t = _lane_iota(texp.shape)
        first = jnp.where((lane_t == 0) | (texp != pltpu.roll(texp, 1, 1)), 1.0, 0.0)
        meta = jnp.concatenate([texp, valid, first, jnp.zeros((5, MAX_TILES_LANES), F32)], axis=0)
        meta_ref[...] = meta.astype(jnp.int32)

    @pl.when(phase == 1)
    def _():
        r = lax.broadcasted_iota(jnp.int32, (tp, tp), 0)
        c = lax.broadcasted_iota(jnp.int32, (tp, tp), 1)
        upper = jnp.where(r < c, 1.0, 0.0).astype(BF16)
        rank = _dot(sel.astype(BF16), upper)
        slot = offs_ref[:, 0:1] + carry_ref[:, 0:1] + rank
        pos_a = jnp.sum(jnp.where(sub == i1, slot, 0.0), axis=0, keepdims=True)
        pos_b = jnp.sum(jnp.where(sub == i2, slot, 0.0), axis=0, keepdims=True)
        pos = jnp.concatenate([pos_a, pos_b, jnp.zeros((6, tp), F32)], axis=0)
        pos_ref[...] = pos.astype(jnp.int32)
        carry_ref[...] += jnp.broadcast_to(tile_cnt, carry_ref.shape)


def _positions(route, *, n_exp, tp):
    n = route.shape[1]
    return pl.pallas_call(
        functools.partial(_pos_kernel, n_exp=n_exp),
        grid=(2, n // tp),
        in_specs=[pl.BlockSpec((8, tp), lambda p, i: (0, i))],
        out_specs=[pl.BlockSpec((8, tp), lambda p, i: (0, i * p)),
                   pl.BlockSpec((8, MAX_TILES_LANES), lambda p, i: (0, 0))],
        out_shape=[jax.ShapeDtypeStruct((8, n), jnp.int32),
                   jax.ShapeDtypeStruct((8, MAX_TILES_LANES), jnp.int32)],
        scratch_shapes=[pltpu.VMEM((n_exp, LANES), F32)] * 3,
        compiler_params=pltpu.CompilerParams(dimension_semantics=("arbitrary", "arbitrary")),
        name="moe_positions",
    )(route)


SC_CORES = 2
SC_SUBCORES = 16
SC_LANES = 16
SC_GATHER_ROWS = 64


def _sc_mesh():
    return plsc.VectorSubcoreMesh(core_axis_name="c", subcore_axis_name="s",
                                  num_cores=SC_CORES, num_subcores=SC_SUBCORES)


def _sc_invert(pos_a, pos_b, n_slots):
    n = pos_a.shape[0]
    per_w = n_slots // (SC_CORES * SC_SUBCORES)
    assert per_w % SC_LANES == 0

    def body(pa_hbm, pb_hbm, src_hbm, pa_v, pb_v, src_v):
        wid = lax.axis_index("s") * SC_CORES + lax.axis_index("c")
        base = pl.multiple_of(wid * per_w, SC_LANES)
        pltpu.sync_copy(pa_hbm, pa_v)
        pltpu.sync_copy(pb_hbm, pb_v)
        lane = lax.iota(jnp.int32, SC_LANES)

        @pl.loop(0, per_w // SC_LANES)
        def _(j):
            off = pl.multiple_of(j * SC_LANES, SC_LANES)
            src_v[pl.ds(off, SC_LANES)] = lax.rem(lane + (base + off), n)

        @pl.loop(0, n // SC_LANES)
        def _(j):
            off = pl.multiple_of(j * SC_LANES, SC_LANES)
            tok = lane + off
            for p_v in (pa_v, pb_v):
                rel = p_v[pl.ds(off, SC_LANES)] - base
                mine = (rel >= 0) & (rel < per_w)
                plsc.store_scatter(src_v, [jnp.where(mine, rel, 0)], tok, mask=mine)

        pltpu.sync_copy(src_v, src_hbm.at[pl.ds(base, per_w)])

    return pl.kernel(
        body, out_type=jax.ShapeDtypeStruct((n_slots,), jnp.int32), mesh=_sc_mesh(),
        scratch_types=[pltpu.VMEM((n,), jnp.int32), pltpu.VMEM((n,), jnp.int32),
                       pltpu.VMEM((per_w,), jnp.int32)],
        compiler_params=pltpu.CompilerParams(needs_layout_passes=False),
        name="moe_invert",
    )(pos_a, pos_b)


def _sc_gather(table, idx):
    m = idx.shape[0]
    d = table.shape[1]
    n_workers = SC_CORES * SC_SUBCORES
    per_w = m // n_workers
    rows = SC_GATHER_ROWS

    n_pairs = per_w // (2 * rows)
    assert per_w == n_pairs * 2 * rows

    def body(table_hbm, idx_hbm, out_hbm, idx_v, buf_v, sems):
        wid = lax.axis_index("s") * SC_CORES + lax.axis_index("c")
        base = pl.multiple_of(wid * per_w, rows)
        pltpu.sync_copy(idx_hbm.at[pl.ds(base, per_w)], idx_v)

        def gather(chunk, slot):
            off = pl.multiple_of(chunk * rows, rows)
            return pltpu.make_async_copy(table_hbm.at[idx_v.at[pl.ds(off, rows)]], buf_v.at[slot],
                                         sems.at[slot])

        def write(chunk, slot):
            off = pl.multiple_of(chunk * rows, rows)
            pltpu.sync_copy(buf_v.at[slot], out_hbm.at[pl.ds(base + off, rows)])

        gather(0, 0).start()

        @pl.loop(0, n_pairs)
        def _(p):
            c0 = 2 * p
            gather(c0, 0).wait()
            gather(c0 + 1, 1).start()
            write(c0, 0)
            gather(c0 + 1, 1).wait()

            @pl.when(p + 1 < n_pairs)
            def _():
                gather(c0 + 2, 0).start()

            write(c0 + 1, 1)

    return pl.kernel(
        body, out_type=jax.ShapeDtypeStruct((m, d), table.dtype), mesh=_sc_mesh(),
        scratch_types=[pltpu.VMEM((per_w,), jnp.int32), pltpu.VMEM((2, rows, d), table.dtype),
                       pltpu.SemaphoreType.DMA((2,))],
        name="moe_gather",
    )(table, idx)


def _expert_kernel(meta_ref, xs_ref, wg_ref, wu_ref, wd_ref, ys_ref, wg_s, wu_s, wd_s):
    t = pl.program_id(0)
    valid = meta_ref[1, t] == 1

    @pl.when(valid & (meta_ref[2, t] == 1))
    def _():
        wg_s[...] = wg_ref[0, 0].astype(BF16)
        wu_s[...] = wu_ref[0, 0].astype(BF16)
        wd_s[...] = wd_ref[0, 0].astype(BF16)

    @pl.when(valid)
    def _():
        xb = _unpack_bf16_pairs(xs_ref[...]).astype(BF16)
        h = jax.nn.silu(_dot(xb, wg_s[...])) * _dot(xb, wu_s[...])
        ys_ref[...] = _pack_bf16_pairs(_dot(h.astype(BF16), wd_s[...]))

    @pl.when(jnp.logical_not(valid))
    def _():
        ys_ref[...] = jnp.zeros_like(ys_ref)


def _experts(meta, xs, wg, wu, wd, layer):
    n_slots, dp = xs.shape
    d, f = wg.shape[2], wg.shape[3]
    grid_spec = pltpu.PrefetchScalarGridSpec(
        num_scalar_prefetch=1,
        grid=(n_slots // MOE_TILE,),
        in_specs=[pl.BlockSpec((MOE_TILE, dp), lambda t, meta: (t, 0)),
                  pl.BlockSpec((1, 1, d, f), lambda t, meta: (layer, meta[0, t], 0, 0)),
                  pl.BlockSpec((1, 1, d, f), lambda t, meta: (layer, meta[0, t], 0, 0)),
                  pl.BlockSpec((1, 1, f, d), lambda t, meta: (layer, meta[0, t], 0, 0))],
        out_specs=pl.BlockSpec((MOE_TILE, dp), lambda t, meta: (t, 0)),
        scratch_shapes=[pltpu.VMEM((d, f), BF16), pltpu.VMEM((d, f), BF16), pltpu.VMEM((f, d), BF16)],
    )
    return pl.pallas_call(
        _expert_kernel, grid_spec=grid_spec,
        out_shape=jax.ShapeDtypeStruct((n_slots, dp), jnp.int32),
        compiler_params=pltpu.CompilerParams(dimension_semantics=("arbitrary",),
                                             vmem_limit_bytes=VMEM_LIMIT),
        name="moe_experts",
    )(meta, xs, wg, wu, wd)


def _combine_kernel(x_ref, ya_ref, yb_ref, route_ref, lng_ref, lnb_ref, o_ref, *, alpha):
    w = route_ref[...].T
    moe = w[:, 2:3] * _unpack_bf16_pairs(ya_ref[...]) + w[:, 3:4] * _unpack_bf16_pairs(yb_ref[...])
    o_ref[...] = _layer_norm(alpha * x_ref[...] + moe, lng_ref[0], lnb_ref[0])


def _combine(x1, yab, route, lng, lnb, *, layer, tn, alpha):
    n, d = x1.shape
    row = pl.BlockSpec((tn, d), lambda i: (i, 0))
    tiles = n // tn
    return pl.pallas_call(
        functools.partial(_combine_kernel, alpha=alpha),
        grid=(tiles,),
        in_specs=[row,
                  pl.BlockSpec((tn, d // 2), lambda i: (i, 0)),
                  pl.BlockSpec((tn, d // 2), lambda i: (i + tiles, 0)),
                  pl.BlockSpec((8, tn), lambda i: (0, i)),
                  _layer_block(lng, layer), _layer_block(lnb, layer)],
        out_specs=row,
        out_shape=jax.ShapeDtypeStruct((n, d), F32),
        compiler_params=pltpu.CompilerParams(dimension_semantics=("arbitrary",)),
        name="moe_combine",
    )(x1, yab, yab, route, lng, lnb)


def _moe(x1, xpk, route, wg, wu, wd, lng, lnb, *, layer, alpha):
    n, d = x1.shape
    n_exp = wg.shape[1]
    n_slots = (2 * n // MOE_TILE + n_exp) * MOE_TILE
    assert n_slots // MOE_TILE <= MAX_TILES_LANES
    pos, meta = _positions(route, n_exp=n_exp, tp=min(1024, n))
    src = _sc_invert(pos[0], pos[1], n_slots)
    xs = _sc_gather(xpk, src)
    ys = _experts(meta, xs, wg, wu, wd, layer)
    yab = _sc_gather(ys, pos[:2].reshape(2 * n))
    return _combine(x1, yab, route, lng, lnb, layer=layer, tn=min(512, n), alpha=alpha)


def _dup(v):
    return jnp.concatenate([v, v], axis=-1)


def kernel(x, mem, w_in, w_gate, b_gate, ssm_lam_re, ssm_lam_im, ssm_log_step, ssm_b_re, ssm_b_im, ssm_c_re, ssm_c_im, ssm_d, w_glu, b_glu, gmlp_ln_g, gmlp_ln_b, w_spatial, b_spatial, w_kv, w_br, w_out, ln1_g, ln1_b, w_router_g, b_router_g, w_router_e, b_router_e, w_exp_gate, w_exp_up, w_exp_down, ln2_g, ln2_b):
    bsz, seq, d = x.shape
    depth = w_in.shape[0]
    n = bsz * seq
    n_grp = ssm_lam_re.shape[2]
    ssm_w = n_grp * GROUP_CH
    gmlp_w = gmlp_ln_g.shape[1]
    n_heads = 4
    n_moe_grp, exp_per_grp = b_router_e.shape[1], b_router_e.shape[2]
    alpha = (2.0 * depth) ** 0.25
    tn = 512

    ls = jnp.broadcast_to(ssm_log_step[..., None], ssm_lam_re.shape)
    six = [ssm_lam_re[:, 0], ssm_lam_im[:, 0], ssm_lam_re[:, 1], ssm_lam_im[:, 1], ls[:, 0], ls[:, 1]]
    prow = jnp.stack([_dup(v) for v in six] + [jnp.zeros_like(_dup(six[0]))] * 2, axis=2)
    pcol = jnp.stack(six + [jnp.zeros_like(six[0])] * 2, axis=-1)
    bt2 = jnp.concatenate([jnp.swapaxes(ssm_b_re, 2, 3), jnp.swapaxes(ssm_b_im, 2, 3)], axis=-1)
    c2 = jnp.concatenate([ssm_c_re, ssm_c_im], axis=-1)
    ct2 = jnp.concatenate([jnp.swapaxes(ssm_c_re, 2, 3), jnp.swapaxes(ssm_c_im, 2, 3)], axis=-1)
    wtab, pst, gmat, dec = _ssm_tables(prow, pcol, bt2, c2, ct2)
    dvec = jnp.repeat(ssm_d.reshape(depth, n_grp, 1, GROUP_CH), SSM_T, axis=-1)

    assert w_in.shape[2] == 4 * ssm_w and gmlp_w == ssm_w
    w_in_b = w_in.astype(BF16)
    wssm_t = jnp.swapaxes(w_in[:, :, :ssm_w], 1, 2).astype(BF16)
    w_gate_b = w_gate.astype(BF16)
    b_gate3 = b_gate.reshape(depth, 1, -1)
    w_br_b = w_br.astype(BF16)
    w_out_b = w_out.astype(BF16)
    w_kv_b = w_kv.astype(BF16)
    w_glu_t = jnp.swapaxes(w_glu, 1, 2).astype(BF16)
    b_glu_c = b_glu.reshape(depth, -1, 1)
    w_sp_b = w_spatial.astype(BF16)
    b_sp_t = jnp.swapaxes(b_spatial, 1, 2)
    row3 = lambda v: v.reshape(depth, 1, -1)
    wrg_t = jnp.repeat(jnp.swapaxes(w_router_g, 1, 2), exp_per_grp, axis=1)
    wre_t = jnp.transpose(w_router_e, (0, 1, 3, 2)).reshape(depth, n_moe_grp * exp_per_grp, d)
    wr = jnp.concatenate([wrg_t, wre_t], axis=1)
    wr_hi = wr.astype(BF16)
    wr_hl = jnp.concatenate([wr_hi, (wr - wr_hi.astype(F32)).astype(BF16)], axis=1)
    br = jnp.concatenate([jnp.repeat(b_router_g, exp_per_grp, axis=1),
                          b_router_e.reshape(depth, -1)], axis=1).reshape(depth, -1, 1)

    n_streams = 1
    ns = n // n_streams
    xin = [x.reshape(n, d)] * n_streams
    row_off = [s * ns for s in range(n_streams)]
    for l in range(depth):
        kv = _kv_proj(mem, w_kv_b, l)
        zs, m12s = [], []
        for s in range(n_streams):
            z, m12 = _front(
                xin[s], wssm_t, w_in_b, w_gate_b, b_gate3, row3(gmlp_ln_g), row3(gmlp_ln_b), w_sp_b,
                b_sp_t, kv, w_br_b, layer=l, tn=tn, n_heads=n_heads, tokens_per_batch=seq,
                n=ns, row_off=row_off[s], batch_off=s * (bsz // n_streams))
            zs.append(z)
            m12s.append(m12)
        ys = _ssm_apply_pairs(zs, wtab, pst, gmat, dec, dvec, l, seq // LANES)
        nxt = []
        for s in range(n_streams):
            x1, xpk, route = _back(
                xin[s], m12s[s], ys[s], w_glu_t, b_glu_c, w_br_b, w_gate_b, b_gate3, w_out_b,
                row3(ln1_g), row3(ln1_b), wr_hl, br, layer=l, tn=tn, alpha=alpha, row_off=row_off[s])
            nxt.append(_moe(x1, xpk, route, w_exp_gate, w_exp_up, w_exp_down, row3(ln2_g), row3(ln2_b),
                            layer=l, alpha=alpha))
        xin = nxt
        row_off = [0] * n_streams
    return jnp.concatenate(xin, axis=0).reshape(bsz, seq, d)
```
